```python
import jax, jax.numpy as jnp
from jax import lax
import numpy as np

D_MODEL = 2048
BATCH = 2
SEQ = 4096
DEPTH = 2

CHUNK = 64
EPS = 1e-6
N_EVEN = (DEPTH + 1) // 2
N_ODD = DEPTH // 2

GMLP_BLOCK = 128
A_HEADS = 8
A_HEAD_DIM = D_MODEL // 2 // A_HEADS
D_A = A_HEADS * A_HEAD_DIM

POOL_WINDOWS = (2, 4, 8, 16)
B_GROUPS = len(POOL_WINDOWS)
B_GROUP_DIM = D_MODEL // 2 // B_GROUPS
D_B = B_GROUPS * B_GROUP_DIM

C_HEADS = 16
Q_LORA = 512
KV_LORA = 512
NOPE_DIM = 128
ROPE_DIM = 64
V_DIM = 128
ROPE_THETA = 10000.0
Q_BLOCK = 128

D_FF = -(-8 * D_MODEL // (3 * 256)) * 256

kernel_name = "hybrid_gmlp_pool_mla_stream_trunk"


def rmsnorm(x, g):
    xf = x.astype(jnp.float32)
    y = xf * lax.rsqrt(jnp.mean(xf * xf, axis=-1, keepdims=True) + EPS)
    return (y * g.astype(jnp.float32)).astype(x.dtype)


def apply_rope(x, cos, sin):
    xf = x.astype(jnp.float32)
    half = x.shape[-1] // 2
    x1, x2 = xf[..., :half], xf[..., half:]
    return jnp.concatenate([x1 * cos - x2 * sin, x2 * cos + x1 * sin], axis=-1).astype(x.dtype)


def gmlp_mixer(uv, g_v, w_s, b_s):
    B_, S_, _ = uv.shape
    uv = jax.nn.gelu(uv)
    u, v = uv[..., :D_A], uv[..., D_A:]
    v = rmsnorm(v, g_v)
    nb = S_ // GMLP_BLOCK
    v = v.reshape(B_, nb, GMLP_BLOCK, A_HEADS, A_HEAD_DIM)
    pos_chunk = jnp.arange(GMLP_BLOCK) // CHUNK
    mask = (pos_chunk[None, :] <= pos_chunk[:, None]).astype(w_s.dtype)
    w = w_s * mask[None]
    mix = jnp.einsum('hts,bnshd->bnthd', w, v) + jnp.swapaxes(b_s, 0, 1)[None, None, :, :, None]
    return u * mix.reshape(B_, S_, D_A)


def multi_scale_pool(z, w_pool, scale):
    B_, S_, _ = z.shape
    zg = z.reshape(B_, S_, B_GROUPS, B_GROUP_DIM)
    cs = jnp.cumsum(zg.astype(jnp.float32), axis=1)
    cs = jnp.pad(cs, ((0, 0), (1, 0), (0, 0), (0, 0)))
    t1 = jnp.arange(1, S_ + 1)
    outs = []
    for g, win in enumerate(POOL_WINDOWS):
        c = cs[:, :, g]
        lo = jnp.pad(c, ((0, 0), (win - 1, 0), (0, 0)))[:, :S_]
        cnt = jnp.minimum(t1, win).astype(jnp.float32)[None, :, None]
        outs.append((c[:, 1:] - lo) / cnt)
    pooled = jnp.stack(outs, axis=2)
    d = (pooled - zg.astype(jnp.float32)).astype(z.dtype)
    y = jnp.einsum('bsgc,gcd->bsgd', d, w_pool).reshape(B_, S_, D_B)
    return y * scale


def mla(h, w_in_c, g_cq, g_ckv, w_uq, w_ukv, w_out_c, cos, sin):
    B_, S_, _ = h.shape
    p = h @ w_in_c
    c_q = p[..., :Q_LORA]
    c_kv = p[..., Q_LORA:Q_LORA + KV_LORA]
    k_rope = apply_rope(p[..., Q_LORA + KV_LORA:], cos, sin)
    q = (rmsnorm(c_q, g_cq) @ w_uq).reshape(B_, S_, C_HEADS, NOPE_DIM + ROPE_DIM)
    q_nope = q[..., :NOPE_DIM]
    q_rope = apply_rope(q[..., NOPE_DIM:], cos[:, None, :], sin[:, None, :])
    kv = (rmsnorm(c_kv, g_ckv) @ w_ukv).reshape(B_, S_, C_HEADS, NOPE_DIM + V_DIM)
    k_nope, v = kv[..., :NOPE_DIM], kv[..., NOPE_DIM:]
    nq = S_ // Q_BLOCK
    key_chunk = jnp.arange(S_) // CHUNK
    sm_scale = (NOPE_DIM + ROPE_DIM) ** -0.5

    def to_blocks(t):
        return jnp.moveaxis(t.reshape(B_, nq, Q_BLOCK, *t.shape[2:]), 1, 0)

    def attend(args):
        qn, qr, blk = args
        s = (jnp.einsum('bqhd,bkhd->bhqk', qn, k_nope)
             + jnp.einsum('bqhr,bkr->bhqk', qr, k_rope)).astype(jnp.float32) * sm_scale
        q_chunk = (blk * Q_BLOCK + jnp.arange(Q_BLOCK)) // CHUNK
        mask = key_chunk[None, :] <= q_chunk[:, None]
        s = jnp.where(mask[None, None], s, -jnp.inf)
        pr = jax.nn.softmax(s, axis=-1).astype(v.dtype)
        return jnp.einsum('bhqk,bkhd->bqhd', pr, v)

    o = lax.map(attend, (to_blocks(q_nope), to_blocks(q_rope), jnp.arange(nq)))
    o = jnp.moveaxis(o, 0, 1).reshape(B_, S_, C_HEADS * V_DIM)
    return o @ w_out_c


def swiglu(h, w_gate, w_up, w_down):
    return (jax.nn.silu(h @ w_gate) * (h @ w_up)) @ w_down


def setup_inputs(seed: int = 0) -> dict:
    key = jax.random.key(seed)
    ks = jax.random.split(key, 24)
    f32 = jnp.float32

    def w(k, shape, fan_in):
        return jax.random.normal(k, shape, f32) * (fan_in ** -0.5)

    def gain(k, shape):
        return 1.0 + 0.05 * jax.random.normal(k, shape, f32)

    return {
        'x': jax.random.normal(ks[0], (BATCH, SEQ, D_MODEL), f32),
        'g_mix': gain(ks[1], (DEPTH, D_MODEL)),
        'g_ffn': gain(ks[2], (DEPTH, D_MODEL)),
        'g_final': gain(ks[3], (D_MODEL,)),
        'w_in_ab': w(ks[4], (N_EVEN, D_MODEL, 2 * D_A + D_B), D_MODEL),
        'g_v': gain(ks[5], (N_EVEN, D_A)),
        'w_s': w(ks[6], (N_EVEN, A_HEADS, GMLP_BLOCK, GMLP_BLOCK), GMLP_BLOCK),
        'b_s': 1.0 + 0.1 * jax.random.normal(ks[7], (N_EVEN, A_HEADS, GMLP_BLOCK), f32),
        'w_pool': w(ks[8], (N_EVEN, B_GROUPS, B_GROUP_DIM, B_GROUP_DIM), B_GROUP_DIM),
        'pool_scale': 1.0 + 0.1 * jax.random.normal(ks[9], (N_EVEN, D_B), f32),
        'w_out_ab': w(ks[10], (N_EVEN, D_A + D_B, D_MODEL), D_A + D_B),
        'w_in_c': w(ks[11], (N_ODD, D_MODEL, Q_LORA + KV_LORA + ROPE_DIM), D_MODEL),
        'g_cq': gain(ks[12], (N_ODD, Q_LORA)),
        'g_ckv': gain(ks[13], (N_ODD, KV_LORA)),
        'w_uq': w(ks[14], (N_ODD, Q_LORA, C_HEADS * (NOPE_DIM + ROPE_DIM)), Q_LORA),
        'w_ukv': w(ks[15], (N_ODD, KV_LORA, C_HEADS * (NOPE_DIM + V_DIM)), KV_LORA),
        'w_out_c': w(ks[16], (N_ODD, C_HEADS * V_DIM, D_MODEL), C_HEADS * V_DIM),
        'w_gate': w(ks[17], (DEPTH, D_MODEL, D_FF), D_MODEL),
        'w_up': w(ks[18], (DEPTH, D_MODEL, D_FF), D_MODEL),
        'w_down': w(ks[19], (DEPTH, D_FF, D_MODEL), D_FF),
    }


def reference(x, g_mix, g_ffn, g_final, w_in_ab, g_v, w_s, b_s, w_pool, pool_scale, w_out_ab,
              w_in_c, g_cq, g_ckv, w_uq, w_ukv, w_out_c, w_gate, w_up, w_down):
    S_ = x.shape[1]
    pos = jnp.arange(S_, dtype=jnp.float32)
    inv_freq = ROPE_THETA ** (-jnp.arange(0, ROPE_DIM, 2, dtype=jnp.float32) / ROPE_DIM)
    ang = pos[:, None] * inv_freq[None, :]
    cos, sin = jnp.cos(ang), jnp.sin(ang)
    for layer in range(DEPTH):
        i = layer // 2
        h = rmsnorm(x, g_mix[layer])
        if layer % 2 == 0:
            p = h @ w_in_ab[i]
            a = gmlp_mixer(p[..., :2 * D_A], g_v[i], w_s[i], b_s[i])
            b = multi_scale_pool(p[..., 2 * D_A:], w_pool[i], pool_scale[i])
            mix = jnp.concatenate([a, b], axis=-1) @ w_out_ab[i]
        else:
            mix = mla(h, w_in_c[i], g_cq[i], g_ckv[i], w_uq[i], w_ukv[i], w_out_c[i], cos, sin)
        x = x + mix
        x = x + swiglu(rmsnorm(x, g_ffn[layer]), w_gate[layer], w_up[layer], w_down[layer])
    return rmsnorm(x, g_final)
```

```python
import functools

import jax
import jax.numpy as jnp
from jax import lax
from jax.experimental import pallas as pl
from jax.experimental.pallas import tpu as pltpu

F32 = jnp.float32
BF16 = jnp.bfloat16

CHUNK = 64
EPS = 1e-6
GMLP_BLOCK = 128
A_HEADS = 8
POOL_WINDOWS = (2, 4, 8, 16)
POOL_HALO = 16
C_HEADS = 16
Q_LORA = 512
KV_LORA = 512
NOPE_DIM = 128
ROPE_DIM = 64
V_DIM = 128
ROPE_THETA = 10000.0
QK_PAD = 256

V7X_LANES = 128
V7X_SCOPED_VMEM_BYTES = 60000 * 1024


def _rms(xf, g):
    ms = jnp.mean(xf * xf, axis=-1, keepdims=True)
    return xf * lax.rsqrt(ms + EPS) * g


def _gelu_tanh(x):
    c = 0.7978845608028654
    return 0.5 * x * (1.0 + jnp.tanh(c * (x + 0.044715 * (x * x * x))))


def _dot(a, b):
    return jnp.dot(a, b, preferred_element_type=F32)


def _const_spec(shape):
    nd = len(shape)
    return pl.BlockSpec(shape, lambda *_: (0,) * nd, pipeline_mode=pl.Buffered(1))


def _params(semantics, vmem_bytes):
    return pltpu.CompilerParams(dimension_semantics=semantics,
                                vmem_limit_bytes=min(int(vmem_bytes), V7X_SCOPED_VMEM_BYTES))


def _rope_table_kernel(invf_ref, cos_ref, sin_ref, *, ts):
    i = pl.program_id(0)
    pos = (lax.broadcasted_iota(jnp.int32, (ts, V7X_LANES), 0) + i * ts).astype(F32)
    lane = lax.broadcasted_iota(jnp.int32, (ts, V7X_LANES), 1)
    ang = pos * invf_ref[...]
    c = jnp.cos(ang)
    s = jnp.sin(ang)
    half = ROPE_DIM // 2
    cos_ref[...] = jnp.where(lane < ROPE_DIM, c, 0.0)
    sin_ref[...] = jnp.where(lane < half, -s, jnp.where(lane < ROPE_DIM, s, 0.0))


def _rope_tables(invf_row, seq, ts):
    return pl.pallas_call(
        functools.partial(_rope_table_kernel, ts=ts),
        grid=(seq // ts,),
        in_specs=[pl.BlockSpec((1, V7X_LANES), lambda i: (0, 0))],
        out_specs=[pl.BlockSpec((ts, V7X_LANES), lambda i: (i, 0)),
                   pl.BlockSpec((ts, V7X_LANES), lambda i: (i, 0))],
        out_shape=[jax.ShapeDtypeStruct((seq, V7X_LANES), F32)] * 2,
        compiler_params=_params(("parallel",), 8 * ts * V7X_LANES * 4 + (4 << 20)),
        name="rope_tables",
    )(invf_row)


def _l0_mixer_kernel(x_ref, gmix_ref, win_ref, gv_ref, ws_ref, bs_ref, wpool_ref, psc_ref, wout_ref,
                     o_ref, h_ref, u_ref, v_ref, z_ref, cat_ref, *, tm, tiles_per_seq, d_a, d_g):
    i = pl.program_id(0)
    seq_tile = i % tiles_per_seq
    nb = tm // GMLP_BLOCK

    h_ref[...] = _rms(x_ref[...], gmix_ref[...]).astype(BF16)

    u_ref[...] = _gelu_tanh(_dot(h_ref[...], win_ref[:, 0:d_a]))
    v_ref[...] = _rms(_gelu_tanh(_dot(h_ref[...], win_ref[:, d_a:2 * d_a])), gv_ref[...]).astype(BF16)

    tt = lax.broadcasted_iota(jnp.int32, (GMLP_BLOCK, GMLP_BLOCK), 0)
    ss = lax.broadcasted_iota(jnp.int32, (GMLP_BLOCK, GMLP_BLOCK), 1)
    tri = (ss // CHUNK <= tt // CHUNK).astype(F32)
    hd_w = d_a // A_HEADS
    for hd in range(A_HEADS):
        c0 = hd * hd_w
        wm = (ws_ref[hd] * tri).astype(BF16)
        rhs = jnp.concatenate(
            [v_ref[n * GMLP_BLOCK:(n + 1) * GMLP_BLOCK, c0:c0 + hd_w] for n in range(nb)], axis=1)
        mix = _dot(wm, rhs)
        bias = bs_ref[hd]
        for n in range(nb):
            r0 = n * GMLP_BLOCK
            a = u_ref[r0:r0 + GMLP_BLOCK, c0:c0 + hd_w] * (mix[:, n * hd_w:(n + 1) * hd_w] + bias)
            cat_ref[r0:r0 + GMLP_BLOCK, c0:c0 + hd_w] = a.astype(BF16)

    @pl.when(seq_tile == 0)
    def _():
        z_ref[0:POOL_HALO, :] = jnp.zeros((POOL_HALO, z_ref.shape[1]), F32)

    z_ref[POOL_HALO:POOL_HALO + tm, :] = _dot(h_ref[...], win_ref[:, 2 * d_a:])
    pos1 = lax.broadcasted_iota(jnp.int32, (tm, d_g), 0) + (seq_tile * tm + 1)
    for g, win in enumerate(POOL_WINDOWS):
        zz = z_ref[:, g * d_g:(g + 1) * d_g]
        s = zz
        k = 1
        while k < win:
            s = s + pltpu.roll(s, k, axis=0)
            k *= 2
        cnt = jnp.minimum(pos1, win).astype(F32)
        d = (s[POOL_HALO:] / cnt - zz[POOL_HALO:]).astype(BF16)
        y = _dot(d, wpool_ref[g]) * psc_ref[:, g * d_g:(g + 1) * d_g]
        cat_ref[:, d_a + g * d_g:d_a + (g + 1) * d_g] = y.astype(BF16)
    z_ref[0:POOL_HALO, :] = z_ref[tm:tm + POOL_HALO, :]

    o_ref[...] = x_ref[...] + _dot(cat_ref[...], wout_ref[...])


def _l0_mixer(x2d, gmix, w_in, g_v, w_s, b_s_b, w_pool, pscale, w_out, *, seq, tm):
    t, d = x2d.shape
    n_in = w_in.shape[1]
    d_a = g_v.shape[1]
    d_b = pscale.shape[1]
    d_g = d_b // len(POOL_WINDOWS)
    assert n_in == 2 * d_a + d_b and seq % tm == 0 and tm % GMLP_BLOCK == 0
    vmem = (4 * tm * d * 4
            + w_in.size * 2 + w_out.size * 2 + w_pool.size * 2 + (w_s.size + b_s_b.size) * 4
            + tm * d * 2 + tm * d_a * 4 + tm * d_a * 2 + (tm + POOL_HALO) * d_b * 4 + tm * (d_a + d_b) * 2
            + 4 * tm * d_a * 4)
    return pl.pallas_call(
        functools.partial(_l0_mixer_kernel, tm=tm, tiles_per_seq=seq // tm, d_a=d_a, d_g=d_g),
        grid=(t // tm,),
        in_specs=[pl.BlockSpec((tm, d), lambda i: (i, 0)),
                  _const_spec(gmix.shape), _const_spec(w_in.shape), _const_spec(g_v.shape),
                  _const_spec(w_s.shape), _const_spec(b_s_b.shape), _const_spec(w_pool.shape),
                  _const_spec(pscale.shape), _const_spec(w_out.shape)],
        out_specs=pl.BlockSpec((tm, d), lambda i: (i, 0)),
        out_shape=jax.ShapeDtypeStruct((t, d), F32),
        scratch_shapes=[pltpu.VMEM((tm, d), BF16),
                        pltpu.VMEM((tm, d_a), F32),
                        pltpu.VMEM((tm, d_a), BF16),
                        pltpu.VMEM((tm + POOL_HALO, d_b), F32),
                        pltpu.VMEM((tm, d_a + d_b), BF16)],
        compiler_params=_params(("arbitrary",), vmem),
        name="l0_mixer",
    )(x2d, gmix, w_in, g_v, w_s, b_s_b, w_pool, pscale, w_out)


def _ffn_kernel(*refs, pre_proj, final_norm):
    it = iter(refs)
    x_ref = next(it)
    attn_ref, wo_ref = (next(it), next(it)) if pre_proj else (None, None)
    g_ref, wg_ref, wu_ref, wd_ref = next(it), next(it), next(it), next(it)
    gfin_ref = next(it) if final_norm else None
    o_ref, h_ref = next(it), next(it)
    f = pl.program_id(1)

    @pl.when(f == 0)
    def _():
        x = x_ref[...]
        if pre_proj:
            x = x + _dot(attn_ref[...], wo_ref[...])
        h_ref[...] = _rms(x, g_ref[...]).astype(BF16)
        o_ref[...] = x

    h = h_ref[...]
    gate = _dot(h, wg_ref[...])
    up = _dot(h, wu_ref[...])
    act = (gate * jax.nn.sigmoid(gate) * up).astype(BF16)
    o_ref[...] += _dot(act, wd_ref[...])

    if final_norm:
        @pl.when(f == pl.num_programs(1) - 1)
        def _():
            o_ref[...] = _rms(o_ref[...], gfin_ref[...])


def _ffn(x2d, g, w_gate, w_up, w_down, *, tm, tf, attn=None, w_o=None, g_final=None):
    t, d = x2d.shape
    dff = w_gate.shape[1]
    assert t % tm == 0 and dff % tf == 0
    pre_proj = attn is not None
    final_norm = g_final is not None
    args = [x2d]
    in_specs = [pl.BlockSpec((tm, d), lambda i, f: (i, 0))]
    vmem = 4 * tm * d * 4 + tm * d * 2 + 2 * 3 * d * tf * 2 + 4 * tm * tf * 4
    if pre_proj:
        args += [attn, w_o]
        in_specs += [pl.BlockSpec((tm, attn.shape[1]), lambda i, f: (i, 0)), _const_spec(w_o.shape)]
        vmem += 2 * tm * attn.shape[1] * 2 + w_o.size * 2 + tm * d * 4
    args += [g, w_gate, w_up, w_down]
    in_specs += [_const_spec(g.shape),
                 pl.BlockSpec((d, tf), lambda i, f: (0, f)),
                 pl.BlockSpec((d, tf), lambda i, f: (0, f)),
                 pl.BlockSpec((tf, d), lambda i, f: (f, 0))]
    if final_norm:
        args.append(g_final)
        in_specs.append(_const_spec(g_final.shape))
    return pl.pallas_call(
        functools.partial(_ffn_kernel, pre_proj=pre_proj, final_norm=final_norm),
        grid=(t // tm, dff // tf),
        in_specs=in_specs,
        out_specs=pl.BlockSpec((tm, d), lambda i, f: (i, 0)),
        out_shape=jax.ShapeDtypeStruct((t, d), F32),
        scratch_shapes=[pltpu.VMEM((tm, d), BF16)],
        compiler_params=_params(("parallel", "arbitrary"), vmem),
        name="ffn_final" if final_norm else "ffn",
    )(*args)


def _mla_proj_kernel(x_ref, gmix_ref, win_ref, gcq_ref, gckv_ref, wuq_ref, wukv_ref, cos_ref, sin_ref,
                     q_ref, kn_ref, kr_ref, v_ref, h_ref, cq_ref, ckv_ref, *, n_kv_cols, col_chunk):
    def rope(a):
        return a * cos_ref[...] + pltpu.roll(a, ROPE_DIM, axis=1) * sin_ref[...]

    h_ref[...] = _rms(x_ref[...], gmix_ref[...]).astype(BF16)
    p = _dot(h_ref[...], win_ref[...])
    cq_ref[...] = _rms(p[:, 0:Q_LORA], gcq_ref[...]).astype(BF16)
    ckv_ref[...] = _rms(p[:, Q_LORA:Q_LORA + KV_LORA], gckv_ref[...]).astype(BF16)
    kr_ref[...] = rope(p[:, Q_LORA + KV_LORA:]).astype(BF16)

    for hd in range(C_HEADS):
        qh = _dot(cq_ref[...], wuq_ref[:, hd * QK_PAD:(hd + 1) * QK_PAD])
        q_ref[:, hd * QK_PAD:hd * QK_PAD + NOPE_DIM] = qh[:, 0:NOPE_DIM].astype(BF16)
        q_ref[:, hd * QK_PAD + NOPE_DIM:(hd + 1) * QK_PAD] = rope(qh[:, NOPE_DIM:]).astype(BF16)

    for c in range(n_kv_cols // col_chunk):
        sl = slice(c * col_chunk, (c + 1) * col_chunk)
        kn_ref[:, sl] = _dot(ckv_ref[...], wukv_ref[:, sl]).astype(BF16)
        v_ref[:, sl] = _dot(ckv_ref[...], wukv_ref[:, n_kv_cols + c * col_chunk:
                                                      n_kv_cols + (c + 1) * col_chunk]).astype(BF16)


def _mla_proj(x2d, gmix, w_in_ext, g_cq_s, g_ckv, w_uq_ext, w_ukv_r, cos_t, sin_t, *, seq, tm):
    t, d = x2d.shape
    n_q = w_uq_ext.shape[1]
    n_kv = w_ukv_r.shape[1] // 2
    tiles_per_seq = seq // tm
    assert seq % tm == 0 and w_in_ext.shape[1] == Q_LORA + KV_LORA + 2 * ROPE_DIM
    vmem = (2 * tm * d * 4 + (w_in_ext.size + w_uq_ext.size + w_ukv_r.size) * 2
            + 2 * 2 * tm * V7X_LANES * 4
            + 2 * tm * (n_q + 2 * n_kv + V7X_LANES) * 2
            + tm * d * 2 + tm * (Q_LORA + KV_LORA) * 2
            + 4 * tm * w_in_ext.shape[1] * 4)
    row = lambda i: (i, 0)
    return pl.pallas_call(
        functools.partial(_mla_proj_kernel, n_kv_cols=n_kv, col_chunk=512),
        grid=(t // tm,),
        in_specs=[pl.BlockSpec((tm, d), row),
                  _const_spec(gmix.shape), _const_spec(w_in_ext.shape), _const_spec(g_cq_s.shape),
                  _const_spec(g_ckv.shape), _const_spec(w_uq_ext.shape), _const_spec(w_ukv_r.shape),
                  pl.BlockSpec((tm, V7X_LANES), lambda i: (i % tiles_per_seq, 0)),
                  pl.BlockSpec((tm, V7X_LANES), lambda i: (i % tiles_per_seq, 0))],
        out_specs=[pl.BlockSpec((tm, n_q), row), pl.BlockSpec((tm, n_kv), row),
                   pl.BlockSpec((tm, V7X_LANES), row), pl.BlockSpec((tm, n_kv), row)],
        out_shape=[jax.ShapeDtypeStruct((t, n_q), BF16), jax.ShapeDtypeStruct((t, n_kv), BF16),
                   jax.ShapeDtypeStruct((t, V7X_LANES), BF16), jax.ShapeDtypeStruct((t, n_kv), BF16)],
        scratch_shapes=[pltpu.VMEM((tm, d), BF16), pltpu.VMEM((tm, Q_LORA), BF16),
                        pltpu.VMEM((tm, KV_LORA), BF16)],
        compiler_params=_params(("parallel",), vmem),
        name="mla_proj",
    )(x2d, gmix, w_in_ext, g_cq_s, g_ckv, w_uq_ext, w_ukv_r, cos_t, sin_t)


def _attn_kernel(q_ref, kn_ref, kr_ref, v_ref, o_ref, kcat_ref, m_ref, l_ref, acc_ref, *, tq):
    qi = pl.program_id(2)

    @pl.when(qi == 0)
    def _():
        kcat_ref[:, 0:NOPE_DIM] = kn_ref[...]
        kcat_ref[:, NOPE_DIM:] = kr_ref[...]

    m_ref[...] = jnp.full(m_ref.shape, -jnp.inf, F32)
    l_ref[...] = jnp.zeros(l_ref.shape, F32)
    acc_ref[...] = jnp.zeros(acc_ref.shape, F32)
    q = q_ref[...]
    reps = tq // V7X_LANES

    def step(j, masked):
        k0 = pl.multiple_of(j * tq, tq)
        s = lax.dot_general(q, kcat_ref[pl.ds(k0, tq), :], (((1,), (1,)), ((), ())),
                            preferred_element_type=F32)
        if masked:
            qc = lax.broadcasted_iota(jnp.int32, (tq, tq), 0) // CHUNK
            kc = lax.broadcasted_iota(jnp.int32, (tq, tq), 1) // CHUNK
            s = jnp.where(kc <= qc, s, -jnp.inf)
        m_prev = m_ref[...]
        m_new = jnp.maximum(m_prev, jnp.max(s, axis=1, keepdims=True))
        alpha = jnp.exp(m_prev - m_new)
        p = jnp.exp(s - pltpu.repeat(m_new, reps, axis=1))
        l_ref[...] = alpha * l_ref[...] + jnp.sum(p, axis=1, keepdims=True)
        acc_ref[...] = alpha * acc_ref[...] + _dot(p.astype(BF16), v_ref[pl.ds(k0, tq), :])
        m_ref[...] = m_new

    def body(j, carry):
        step(j, False)
        return carry

    lax.fori_loop(0, qi, body, 0)
    step(qi, True)
    o_ref[...] = (acc_ref[...] / l_ref[...]).astype(o_ref.dtype)


def _attention(q, kn, kr, v, *, batch, seq, tq):
    t = q.shape[0]
    nq = seq // tq
    assert seq % tq == 0 and tq % CHUNK == 0 and V_DIM == V7X_LANES and NOPE_DIM == V7X_LANES
    vmem = (2 * tq * QK_PAD * 2 + 2 * 2 * seq * V7X_LANES * 2 + 2 * seq * V7X_LANES * 2
            + 2 * tq * V_DIM * 2 + seq * QK_PAD * 2 + 3 * tq * V7X_LANES * 4 + 6 * tq * tq * 4)
    return pl.pallas_call(
        functools.partial(_attn_kernel, tq=tq),
        grid=(batch, C_HEADS, nq),
        in_specs=[pl.BlockSpec((tq, QK_PAD), lambda b, h, i: (b * nq + i, h)),
                  pl.BlockSpec((seq, NOPE_DIM), lambda b, h, i: (b, h)),
                  pl.BlockSpec((seq, V7X_LANES), lambda b, h, i: (b, 0)),
                  pl.BlockSpec((seq, V_DIM), lambda b, h, i: (b, h))],
        out_specs=pl.BlockSpec((tq, V_DIM), lambda b, h, i: (b * nq + i, h)),
        out_shape=jax.ShapeDtypeStruct((t, C_HEADS * V_DIM), BF16),
        scratch_shapes=[pltpu.VMEM((seq, QK_PAD), BF16),
                        pltpu.VMEM((tq, V7X_LANES), F32), pltpu.VMEM((tq, V7X_LANES), F32),
                        pltpu.VMEM((tq, V_DIM), F32)],
        compiler_params=_params(("parallel", "parallel", "arbitrary"), vmem),
        name="attention",
    )(q, kn, kr, v)


def _swap_halves(w):
    half = w.shape[-1] // 2
    return jnp.concatenate([w[..., half:], w[..., :half]], axis=-1)


def kernel(x, g_mix, g_ffn, g_final, w_in_ab, g_v, w_s, b_s, w_pool, pool_scale, w_out_ab, w_in_c, g_cq, g_ckv, w_uq, w_ukv, w_out_c, w_gate, w_up, w_down):
    batch, seq, d = x.shape
    x2d = x.reshape(batch * seq, d)
    row = lambda a: a.reshape(1, -1)

    b_s_b = jnp.broadcast_to(b_s[0][:, :, None], (A_HEADS, GMLP_BLOCK, GMLP_BLOCK))
    x1 = _l0_mixer(x2d, row(g_mix[0]), w_in_ab[0].astype(BF16), row(g_v[0]), w_s[0], b_s_b,
                   w_pool[0].astype(BF16), row(pool_scale[0]), w_out_ab[0].astype(BF16),
                   seq=seq, tm=512)
    x2 = _ffn(x1, row(g_ffn[0]), w_gate[0].astype(BF16), w_up[0].astype(BF16), w_down[0].astype(BF16),
              tm=512, tf=512)

    inv_freq = ROPE_THETA ** (-jnp.arange(0, ROPE_DIM, 2, dtype=F32) / ROPE_DIM)
    cos_t, sin_t = _rope_tables(jnp.tile(inv_freq, V7X_LANES // inv_freq.shape[0]).reshape(1, -1),
                                seq, 512)
    wc = w_in_c[0]
    w_kr = wc[:, Q_LORA + KV_LORA:]
    w_in_ext = jnp.concatenate([wc, _swap_halves(w_kr)], axis=1).astype(BF16)
    wq = w_uq[0].reshape(Q_LORA, C_HEADS, NOPE_DIM + ROPE_DIM)
    w_uq_ext = jnp.concatenate([wq, _swap_halves(wq[..., NOPE_DIM:])], axis=-1)
    w_uq_ext = w_uq_ext.reshape(Q_LORA, C_HEADS * QK_PAD).astype(BF16)
    wkv = w_ukv[0].reshape(KV_LORA, C_HEADS, NOPE_DIM + V_DIM)
    w_ukv_r = jnp.concatenate([wkv[..., :NOPE_DIM].reshape(KV_LORA, -1),
                               wkv[..., NOPE_DIM:].reshape(KV_LORA, -1)], axis=1).astype(BF16)
    sm_scale = (NOPE_DIM + ROPE_DIM) ** -0.5
    q, kn, kr, v = _mla_proj(x2, row(g_mix[1]), w_in_ext, row(g_cq[0]) * sm_scale, row(g_ckv[0]),
                             w_uq_ext, w_ukv_r, cos_t, sin_t, seq=seq, tm=512)
    o = _attention(q, kn, kr, v, batch=batch, seq=seq, tq=512)
    out = _ffn(x2, row(g_ffn[1]), w_gate[1].astype(BF16), w_up[1].astype(BF16), w_down[1].astype(BF16),
               tm=512, tf=512, attn=o, w_o=w_out_c[0].astype(BF16), g_final=row(g_final))
    return out.reshape(batch, seq, d)
```

```python
import functools

import jax
import jax.numpy as jnp
from jax import lax
from jax.experimental import pallas as pl
from jax.experimental.pallas import tpu as pltpu

F32 = jnp.float32
BF16 = jnp.bfloat16

CHUNK = 64
EPS = 1e-6
GMLP_BLOCK = 128
A_HEADS = 8
POOL_WINDOWS = (2, 4, 8, 16)
POOL_HALO = 16
C_HEADS = 16
Q_LORA = 512
KV_LORA = 512
NOPE_DIM = 128
ROPE_DIM = 64
V_DIM = 128
ROPE_THETA = 10000.0
QK_PAD = 256

V7X_LANES = 128
V7X_SCOPED_VMEM_BYTES = 60000 * 1024


def _rms(xf, g):
    ms = jnp.mean(xf * xf, axis=-1, keepdims=True)
    return xf * lax.rsqrt(ms + EPS) * g


def _gelu_tanh(x):
    c = 0.7978845608028654
    return 0.5 * x * (1.0 + jnp.tanh(c * (x + 0.044715 * (x * x * x))))


def _dot(a, b):
    return jnp.dot(a, b, preferred_element_type=F32)


def _const_spec(shape):
    nd = len(shape)
    return pl.BlockSpec(shape, lambda *_: (0,) * nd, pipeline_mode=pl.Buffered(1))


def _params(semantics, vmem_bytes):
    return pltpu.CompilerParams(dimension_semantics=semantics,
                                vmem_limit_bytes=min(int(vmem_bytes), V7X_SCOPED_VMEM_BYTES))


def _rope_table_kernel(invf_ref, cos_ref, sin_ref, cos_tr_ref, sin_tr_ref, *, ts):
    i = pl.program_id(0)
    pos = (lax.broadcasted_iota(jnp.int32, (ts, V7X_LANES), 0) + i * ts).astype(F32)
    lane = lax.broadcasted_iota(jnp.int32, (ts, V7X_LANES), 1)
    ang = pos * invf_ref[...]
    c = jnp.cos(ang)
    s = jnp.sin(ang)
    half = ROPE_DIM // 2
    cos_t = jnp.where(lane < ROPE_DIM, c, 0.0)
    sin_t = jnp.where(lane < half, -s, jnp.where(lane < ROPE_DIM, s, 0.0))
    cos_ref[...] = cos_t
    sin_ref[...] = sin_t
    cos_tr_ref[...] = cos_t.T[0:ROPE_DIM]
    sin_tr_ref[...] = sin_t.T[0:ROPE_DIM]


def _rope_tables(invf_row, seq, ts):
    return pl.pallas_call(
        functools.partial(_rope_table_kernel, ts=ts),
        grid=(seq // ts,),
        in_specs=[pl.BlockSpec((1, V7X_LANES), lambda i: (0, 0))],
        out_specs=[pl.BlockSpec((ts, V7X_LANES), lambda i: (i, 0)),
                   pl.BlockSpec((ts, V7X_LANES), lambda i: (i, 0)),
                   pl.BlockSpec((ROPE_DIM, ts), lambda i: (0, i)),
                   pl.BlockSpec((ROPE_DIM, ts), lambda i: (0, i))],
        out_shape=[jax.ShapeDtypeStruct((seq, V7X_LANES), F32)] * 2
                  + [jax.ShapeDtypeStruct((ROPE_DIM, seq), F32)] * 2,
        compiler_params=_params(("parallel",), 16 * ts * V7X_LANES * 4 + (4 << 20)),
        name="rope_tables",
    )(invf_row)


def _l0_mixer_kernel(x_ref, gmix_ref, win_ref, gv_ref, ws_ref, bs_ref, wpool_ref, psc_ref, wout_ref,
                     o_ref, h_ref, u_ref, v_ref, z_ref, cat_ref, *, tm, tiles_per_seq, d_a, d_g):
    i = pl.program_id(0)
    seq_tile = i % tiles_per_seq
    nb = tm // GMLP_BLOCK

    h_ref[...] = _rms(x_ref[...], gmix_ref[...]).astype(BF16)

    u_ref[...] = _gelu_tanh(_dot(h_ref[...], win_ref[:, 0:d_a]))
    v_ref[...] = _rms(_gelu_tanh(_dot(h_ref[...], win_ref[:, d_a:2 * d_a])), gv_ref[...]).astype(BF16)

    tt = lax.broadcasted_iota(jnp.int32, (GMLP_BLOCK, GMLP_BLOCK), 0)
    ss = lax.broadcasted_iota(jnp.int32, (GMLP_BLOCK, GMLP_BLOCK), 1)
    tri = (ss // CHUNK <= tt // CHUNK).astype(F32)
    hd_w = d_a // A_HEADS
    for hd in range(A_HEADS):
        c0 = hd * hd_w
        wm = (ws_ref[hd] * tri).astype(BF16)
        rhs = jnp.concatenate(
            [v_ref[n * GMLP_BLOCK:(n + 1) * GMLP_BLOCK, c0:c0 + hd_w] for n in range(nb)], axis=1)
        mix = _dot(wm, rhs)
        bias = bs_ref[hd]
        for n in range(nb):
            r0 = n * GMLP_BLOCK
            a = u_ref[r0:r0 + GMLP_BLOCK, c0:c0 + hd_w] * (mix[:, n * hd_w:(n + 1) * hd_w] + bias)
            cat_ref[r0:r0 + GMLP_BLOCK, c0:c0 + hd_w] = a.astype(BF16)

    @pl.when(seq_tile == 0)
    def _():
        z_ref[0:POOL_HALO, :] = jnp.zeros((POOL_HALO, z_ref.shape[1]), F32)

    z_ref[POOL_HALO:POOL_HALO + tm, :] = _dot(h_ref[...], win_ref[:, 2 * d_a:])
    pos1 = lax.broadcasted_iota(jnp.int32, (tm, d_g), 0) + (seq_tile * tm + 1)
    for g, win in enumerate(POOL_WINDOWS):
        zz = z_ref[:, g * d_g:(g + 1) * d_g]
        s = zz
        k = 1
        while k < win:
            s = s + pltpu.roll(s, k, axis=0)
            k *= 2
        cnt = jnp.minimum(pos1, win).astype(F32)
        d = (s[POOL_HALO:] / cnt - zz[POOL_HALO:]).astype(BF16)
        y = _dot(d, wpool_ref[g]) * psc_ref[:, g * d_g:(g + 1) * d_g]
        cat_ref[:, d_a + g * d_g:d_a + (g + 1) * d_g] = y.astype(BF16)
    z_ref[0:POOL_HALO, :] = z_ref[tm:tm + POOL_HALO, :]

    o_ref[...] = x_ref[...] + _dot(cat_ref[...], wout_ref[...])


def _l0_mixer(x2d, gmix, w_in, g_v, w_s, b_s_b, w_pool, pscale, w_out, *, seq, tm):
    t, d = x2d.shape
    n_in = w_in.shape[1]
    d_a = g_v.shape[1]
    d_b = pscale.shape[1]
    d_g = d_b // len(POOL_WINDOWS)
    assert n_in == 2 * d_a + d_b and seq % tm == 0 and tm % GMLP_BLOCK == 0
    vmem = (4 * tm * d * 4
            + w_in.size * 2 + w_out.size * 2 + w_pool.size * 2 + (w_s.size + b_s_b.size) * 4
            + tm * d * 2 + tm * d_a * 4 + tm * d_a * 2 + (tm + POOL_HALO) * d_b * 4 + tm * (d_a + d_b) * 2
            + 4 * tm * d_a * 4)
    return pl.pallas_call(
        functools.partial(_l0_mixer_kernel, tm=tm, tiles_per_seq=seq // tm, d_a=d_a, d_g=d_g),
        grid=(t // tm,),
        in_specs=[pl.BlockSpec((tm, d), lambda i: (i, 0)),
                  _const_spec(gmix.shape), _const_spec(w_in.shape), _const_spec(g_v.shape),
                  _const_spec(w_s.shape), _const_spec(b_s_b.shape), _const_spec(w_pool.shape),
                  _const_spec(pscale.shape), _const_spec(w_out.shape)],
        out_specs=pl.BlockSpec((tm, d), lambda i: (i, 0)),
        out_shape=jax.ShapeDtypeStruct((t, d), F32),
        scratch_shapes=[pltpu.VMEM((tm, d), BF16),
                        pltpu.VMEM((tm, d_a), F32),
                        pltpu.VMEM((tm, d_a), BF16),
                        pltpu.VMEM((tm + POOL_HALO, d_b), F32),
                        pltpu.VMEM((tm, d_a + d_b), BF16)],
        compiler_params=_params(("arbitrary",), vmem),
        name="l0_mixer",
    )(x2d, gmix, w_in, g_v, w_s, b_s_b, w_pool, pscale, w_out)


def _ffn_kernel(*refs, pre_proj, final_norm):
    it = iter(refs)
    x_ref = next(it)
    attn_ref, wo_ref = (next(it), next(it)) if pre_proj else (None, None)
    g_ref, wg_ref, wu_ref, wd_ref = next(it), next(it), next(it), next(it)
    gfin_ref = next(it) if final_norm else None
    o_ref, h_ref = next(it), next(it)
    f = pl.program_id(1)

    @pl.when(f == 0)
    def _():
        x = x_ref[...]
        if pre_proj:
            x = x + _dot(attn_ref[...], wo_ref[...])
        h_ref[...] = _rms(x, g_ref[...]).astype(BF16)
        o_ref[...] = x

    h = h_ref[...]
    gate = _dot(h, wg_ref[...])
    up = _dot(h, wu_ref[...])
    act = (gate * jax.nn.sigmoid(gate) * up).astype(BF16)
    o_ref[...] += _dot(act, wd_ref[...])

    if final_norm:
        @pl.when(f == pl.num_programs(1) - 1)
        def _():
            o_ref[...] = _rms(o_ref[...], gfin_ref[...])


def _ffn(x2d, g, w_gate, w_up, w_down, *, tm, tf, attn=None, w_o=None, g_final=None):
    t, d = x2d.shape
    dff = w_gate.shape[1]
    assert t % tm == 0 and dff % tf == 0
    pre_proj = attn is not None
    final_norm = g_final is not None
    args = [x2d]
    in_specs = [pl.BlockSpec((tm, d), lambda i, f: (i, 0))]
    vmem = 4 * tm * d * 4 + tm * d * 2 + 2 * 3 * d * tf * 2 + 4 * tm * tf * 4
    if pre_proj:
        args += [attn, w_o]
        in_specs += [pl.BlockSpec((tm, attn.shape[1]), lambda i, f: (i, 0)), _const_spec(w_o.shape)]
        vmem += 2 * tm * attn.shape[1] * 2 + w_o.size * 2 + tm * d * 4
    args += [g, w_gate, w_up, w_down]
    in_specs += [_const_spec(g.shape),
                 pl.BlockSpec((d, tf), lambda i, f: (0, f)),
                 pl.BlockSpec((d, tf), lambda i, f: (0, f)),
                 pl.BlockSpec((tf, d), lambda i, f: (f, 0))]
    if final_norm:
        args.append(g_final)
        in_specs.append(_const_spec(g_final.shape))
    return pl.pallas_call(
        functools.partial(_ffn_kernel, pre_proj=pre_proj, final_norm=final_norm),
        grid=(t // tm, dff // tf),
        in_specs=in_specs,
        out_specs=pl.BlockSpec((tm, d), lambda i, f: (i, 0)),
        out_shape=jax.ShapeDtypeStruct((t, d), F32),
        scratch_shapes=[pltpu.VMEM((tm, d), BF16)],
        compiler_params=_params(("parallel", "arbitrary"), vmem),
        name="ffn_final" if final_norm else "ffn",
    )(*args)


def _mla_proj_kernel(x_ref, gmix_ref, win_ref, gcq_ref, gckv_ref, wuq_tr_ref, wkn_ref, wv_tr_ref,
                     cos_ref, sin_ref, cos_tr_ref, sin_tr_ref,
                     q_tr_ref, kn_ref, kr_ref, v_tr_ref, h_ref, cq_tr_ref, ckv_ref, ckv_tr_ref, *, chunk):
    h_ref[...] = _rms(x_ref[...], gmix_ref[...]).astype(BF16)
    p = _dot(h_ref[...], win_ref[...])
    cq_tr_ref[...] = _rms(p[:, 0:Q_LORA], gcq_ref[...]).T.astype(BF16)
    ckv = _rms(p[:, Q_LORA:Q_LORA + KV_LORA], gckv_ref[...])
    ckv_ref[...] = ckv.astype(BF16)
    ckv_tr_ref[...] = ckv.T.astype(BF16)
    a = p[:, Q_LORA + KV_LORA:]
    kr_ref[...] = (a * cos_ref[...] + pltpu.roll(a, ROPE_DIM, axis=1) * sin_ref[...]).astype(BF16)

    rope_lo, rope_hi = NOPE_DIM, NOPE_DIM + ROPE_DIM
    for hd in range(C_HEADS):
        r0 = hd * QK_PAD
        qh = _dot(wuq_tr_ref[r0:r0 + QK_PAD, :], cq_tr_ref[...])
        q_tr_ref[r0:r0 + rope_lo, :] = qh[0:rope_lo].astype(BF16)
        roped = qh[rope_lo:rope_hi] * cos_tr_ref[...] + qh[rope_hi:] * sin_tr_ref[...]
        q_tr_ref[r0 + rope_lo:r0 + rope_hi, :] = roped.astype(BF16)
        q_tr_ref[r0 + rope_hi:r0 + QK_PAD, :] = jnp.zeros((QK_PAD - rope_hi, qh.shape[1]), BF16)

    for c in range(kn_ref.shape[1] // chunk):
        sl = slice(c * chunk, (c + 1) * chunk)
        kn_ref[:, sl] = _dot(ckv_ref[...], wkn_ref[:, sl]).astype(BF16)
        v_tr_ref[sl, :] = _dot(wv_tr_ref[sl, :], ckv_tr_ref[...]).astype(BF16)


def _mla_proj(x2d, gmix, w_in_ext, g_cq_s, g_ckv, w_uq_tr, w_kn, w_v_tr, cos_t, sin_t, cos_tr, sin_tr,
              *, seq, tm):
    t, d = x2d.shape
    n_q = w_uq_tr.shape[0]
    n_kv = w_kn.shape[1]
    tiles_per_seq = seq // tm
    assert seq % tm == 0 and w_in_ext.shape[1] == Q_LORA + KV_LORA + 2 * ROPE_DIM
    vmem = (2 * tm * d * 4 + (w_in_ext.size + w_uq_tr.size + w_kn.size + w_v_tr.size) * 2
            + 2 * 2 * tm * (V7X_LANES + ROPE_DIM) * 4
            + 2 * tm * (n_q + 2 * n_kv + V7X_LANES) * 2
            + tm * d * 2 + tm * (Q_LORA + 2 * KV_LORA) * 2
            + 4 * tm * w_in_ext.shape[1] * 4)
    row = lambda i: (i, 0)
    col = lambda i: (0, i)
    return pl.pallas_call(
        functools.partial(_mla_proj_kernel, chunk=512),
        grid=(t // tm,),
        in_specs=[pl.BlockSpec((tm, d), row),
                  _const_spec(gmix.shape), _const_spec(w_in_ext.shape), _const_spec(g_cq_s.shape),
                  _const_spec(g_ckv.shape), _const_spec(w_uq_tr.shape), _const_spec(w_kn.shape),
                  _const_spec(w_v_tr.shape),
                  pl.BlockSpec((tm, V7X_LANES), lambda i: (i % tiles_per_seq, 0)),
                  pl.BlockSpec((tm, V7X_LANES), lambda i: (i % tiles_per_seq, 0)),
                  pl.BlockSpec((ROPE_DIM, tm), lambda i: (0, i % tiles_per_seq)),
                  pl.BlockSpec((ROPE_DIM, tm), lambda i: (0, i % tiles_per_seq))],
        out_specs=[pl.BlockSpec((n_q, tm), col), pl.BlockSpec((tm, n_kv), row),
                   pl.BlockSpec((tm, V7X_LANES), row), pl.BlockSpec((n_kv, tm), col)],
        out_shape=[jax.ShapeDtypeStruct((n_q, t), BF16), jax.ShapeDtypeStruct((t, n_kv), BF16),
                   jax.ShapeDtypeStruct((t, V7X_LANES), BF16), jax.ShapeDtypeStruct((n_kv, t), BF16)],
        scratch_shapes=[pltpu.VMEM((tm, d), BF16), pltpu.VMEM((Q_LORA, tm), BF16),
                        pltpu.VMEM((tm, KV_LORA), BF16), pltpu.VMEM((KV_LORA, tm), BF16)],
        compiler_params=_params(("parallel",), vmem),
        name="mla_proj",
    )(x2d, gmix, w_in_ext, g_cq_s, g_ckv, w_uq_tr, w_kn, w_v_tr, cos_t, sin_t, cos_tr, sin_tr)


def _attn_kernel(q_tr_ref, kn_ref, kr_ref, v_tr_ref, o_ref, kcat_ref, m_ref, l_ref, acc_ref, *, tk, tq, skew):
    seq = kcat_ref.shape[0]
    kcat_ref[:, 0:NOPE_DIM] = kn_ref[...]
    kcat_ref[:, NOPE_DIM:] = kr_ref[...]

    units = []
    for j in range(seq // tk):
        for i in range(seq // tq):
            k_lo, q_lo = j * tk, i * tq
            k_hi = min(k_lo + tk, q_lo + tq)
            if k_hi > k_lo:
                units.append((j == 0, k_lo, k_hi, q_lo))

    def scores(unit):
        _, k_lo, k_hi, q_lo = unit
        s = _dot(kcat_ref[k_lo:k_hi, :], q_tr_ref[:, q_lo:q_lo + tq])
        if k_hi > q_lo + CHUNK:
            kc = (lax.broadcasted_iota(jnp.int32, s.shape, 0) + k_lo) // CHUNK
            qc = (lax.broadcasted_iota(jnp.int32, s.shape, 1) + q_lo) // CHUNK
            s = jnp.where(kc <= qc, s, -jnp.inf)
        return s

    pending = [scores(u) for u in units[:skew]]
    for n, (first, k_lo, k_hi, q_lo) in enumerate(units):
        if n + skew < len(units):
            pending.append(scores(units[n + skew]))
        s = pending.pop(0)
        qs = slice(q_lo, q_lo + tq)
        v_j = v_tr_ref[:, k_lo:k_hi]
        m_blk = jnp.max(s, axis=0, keepdims=True)
        if first:
            m_new = m_blk
            p = jnp.exp2(s - m_new)
            l_ref[:, qs] = jnp.sum(p, axis=0, keepdims=True)
            acc_ref[:, qs] = _dot(v_j, p.astype(BF16))
        else:
            m_prev = m_ref[:, qs]
            m_new = jnp.maximum(m_prev, m_blk)
            alpha = jnp.exp2(m_prev - m_new)
            p = jnp.exp2(s - m_new)
            l_ref[:, qs] = alpha * l_ref[:, qs] + jnp.sum(p, axis=0, keepdims=True)
            acc_ref[:, qs] = alpha * acc_ref[:, qs] + _dot(v_j, p.astype(BF16))
        m_ref[:, qs] = m_new

    for i in range(seq // tq):
        qs = slice(i * tq, (i + 1) * tq)
        o_ref[qs, :] = (acc_ref[:, qs] / l_ref[:, qs]).T.astype(o_ref.dtype)


def _attention(q_tr, kn, kr, v_tr, *, batch, seq, tk, tq, skew):
    t = kn.shape[0]
    assert seq % tk == 0 and seq % tq == 0 and tk % CHUNK == 0 and tq % CHUNK == 0
    assert V_DIM == V7X_LANES and NOPE_DIM == V7X_LANES
    vmem = (2 * (QK_PAD + V_DIM) * seq * 2 + 2 * 2 * seq * V7X_LANES * 2 + 2 * seq * V_DIM * 2
            + seq * QK_PAD * 2 + 2 * 8 * seq * 4 + V_DIM * seq * 4 + 24 * tk * tq * 4)
    return pl.pallas_call(
        functools.partial(_attn_kernel, tk=tk, tq=tq, skew=skew),
        grid=(batch, C_HEADS),
        in_specs=[pl.BlockSpec((QK_PAD, seq), lambda b, h: (h, b)),
                  pl.BlockSpec((seq, NOPE_DIM), lambda b, h: (b, h)),
                  pl.BlockSpec((seq, V7X_LANES), lambda b, h: (b, 0)),
                  pl.BlockSpec((V_DIM, seq), lambda b, h: (h, b))],
        out_specs=pl.BlockSpec((seq, V_DIM), lambda b, h: (b, h)),
        out_shape=jax.ShapeDtypeStruct((t, C_HEADS * V_DIM), BF16),
        scratch_shapes=[pltpu.VMEM((seq, QK_PAD), BF16),
                        pltpu.VMEM((1, seq), F32), pltpu.VMEM((1, seq), F32),
                        pltpu.VMEM((V_DIM, seq), F32)],
        compiler_params=_params(("parallel", "parallel"), vmem),
        name="attention",
    )(q_tr, kn, kr, v_tr)


def _swap_halves(w):
    half = w.shape[-1] // 2
    return jnp.concatenate([w[..., half:], w[..., :half]], axis=-1)


def kernel(x, g_mix, g_ffn, g_final, w_in_ab, g_v, w_s, b_s, w_pool, pool_scale, w_out_ab, w_in_c, g_cq, g_ckv, w_uq, w_ukv, w_out_c, w_gate, w_up, w_down):
    batch, seq, d = x.shape
    x2d = x.reshape(batch * seq, d)
    row = lambda a: a.reshape(1, -1)

    b_s_b = jnp.broadcast_to(b_s[0][:, :, None], (A_HEADS, GMLP_BLOCK, GMLP_BLOCK))
    x1 = _l0_mixer(x2d, row(g_mix[0]), w_in_ab[0].astype(BF16), row(g_v[0]), w_s[0], b_s_b,
                   w_pool[0].astype(BF16), row(pool_scale[0]), w_out_ab[0].astype(BF16),
                   seq=seq, tm=512)
    x2 = _ffn(x1, row(g_ffn[0]), w_gate[0].astype(BF16), w_up[0].astype(BF16), w_down[0].astype(BF16),
              tm=512, tf=512)

    inv_freq = ROPE_THETA ** (-jnp.arange(0, ROPE_DIM, 2, dtype=F32) / ROPE_DIM)
    cos_t, sin_t, cos_tr, sin_tr = _rope_tables(
        jnp.tile(inv_freq, V7X_LANES // inv_freq.shape[0]).reshape(1, -1), seq, 512)
    wc = w_in_c[0]
    w_kr = wc[:, Q_LORA + KV_LORA:]
    w_in_ext = jnp.concatenate([wc, _swap_halves(w_kr)], axis=1).astype(BF16)
    wq = w_uq[0].reshape(Q_LORA, C_HEADS, NOPE_DIM + ROPE_DIM)
    w_uq_ext = jnp.concatenate([wq, _swap_halves(wq[..., NOPE_DIM:])], axis=-1)
    w_uq_tr = w_uq_ext.reshape(Q_LORA, C_HEADS * QK_PAD).T.astype(BF16)
    wkv = w_ukv[0].reshape(KV_LORA, C_HEADS, NOPE_DIM + V_DIM)
    w_kn = wkv[..., :NOPE_DIM].reshape(KV_LORA, -1).astype(BF16)
    w_v_tr = wkv[..., NOPE_DIM:].reshape(KV_LORA, -1).T.astype(BF16)
    score_scale = (NOPE_DIM + ROPE_DIM) ** -0.5 * 1.4426950408889634
    q_tr, kn, kr, v_tr = _mla_proj(x2, row(g_mix[1]), w_in_ext, row(g_cq[0]) * score_scale,
                                   row(g_ckv[0]), w_uq_tr, w_kn, w_v_tr, cos_t, sin_t, cos_tr, sin_tr,
                                   seq=seq, tm=512)
    o = _attention(q_tr, kn, kr, v_tr, batch=batch, seq=seq, tk=512, tq=256, skew=3)
    out = _ffn(x2, row(g_ffn[1]), w_gate[1].astype(BF16), w_up[1].astype(BF16), w_down[1].astype(BF16),
               tm=512, tf=512, attn=o, w_o=w_out_c[0].astype(BF16), g_final=row(g_final))
    return out.reshape(batch, seq, d)
```

```python
import functools

import jax
import jax.numpy as jnp
from jax import lax
from jax.experimental import pallas as pl
from jax.experimental.pallas import tpu as pltpu

F32 = jnp.float32
BF16 = jnp.bfloat16

CHUNK = 64
EPS = 1e-6
GMLP_BLOCK = 128
A_HEADS = 8
POOL_WINDOWS = (2, 4, 8, 16)
POOL_HALO = 16
C_HEADS = 16
Q_LORA = 512
KV_LORA = 512
NOPE_DIM = 128
ROPE_DIM = 64
V_DIM = 128
ROPE_THETA = 10000.0
QK_PAD = 256

V7X_LANES = 128
V7X_SCOPED_VMEM_BYTES = 60000 * 1024


def _rms(xf, g):
    ms = jnp.mean(xf * xf, axis=-1, keepdims=True)
    return xf * lax.rsqrt(ms + EPS) * g


def _gelu_tanh(x):
    c = 0.7978845608028654
    return 0.5 * x * (1.0 + jnp.tanh(c * (x + 0.044715 * (x * x * x))))


def _dot(a, b):
    return jnp.dot(a, b, preferred_element_type=F32)


def _const_spec(shape):
    nd = len(shape)
    return pl.BlockSpec(shape, lambda *_: (0,) * nd, pipeline_mode=pl.Buffered(1))


def _params(semantics, vmem_bytes):
    return pltpu.CompilerParams(dimension_semantics=semantics,
                                vmem_limit_bytes=min(int(vmem_bytes), V7X_SCOPED_VMEM_BYTES))


def _rope_table_kernel(invf_ref, cos_ref, sin_ref, cos_tr_ref, sin_tr_ref, *, ts):
    i = pl.program_id(0)
    pos = (lax.broadcasted_iota(jnp.int32, (ts, V7X_LANES), 0) + i * ts).astype(F32)
    lane = lax.broadcasted_iota(jnp.int32, (ts, V7X_LANES), 1)
    ang = pos * invf_ref[...]
    c = jnp.cos(ang)
    s = jnp.sin(ang)
    half = ROPE_DIM // 2
    cos_t = jnp.where(lane < ROPE_DIM, c, 0.0)
    sin_t = jnp.where(lane < half, -s, jnp.where(lane < ROPE_DIM, s, 0.0))
    cos_ref[...] = cos_t
    sin_ref[...] = sin_t
    cos_tr_ref[...] = cos_t.T[0:ROPE_DIM]
    sin_tr_ref[...] = sin_t.T[0:ROPE_DIM]


def _rope_tables(invf_row, seq, ts):
    return pl.pallas_call(
        functools.partial(_rope_table_kernel, ts=ts),
        grid=(seq // ts,),
        in_specs=[pl.BlockSpec((1, V7X_LANES), lambda i: (0, 0))],
        out_specs=[pl.BlockSpec((ts, V7X_LANES), lambda i: (i, 0)),
                   pl.BlockSpec((ts, V7X_LANES), lambda i: (i, 0)),
                   pl.BlockSpec((ROPE_DIM, ts), lambda i: (0, i)),
                   pl.BlockSpec((ROPE_DIM, ts), lambda i: (0, i))],
        out_shape=[jax.ShapeDtypeStruct((seq, V7X_LANES), F32)] * 2
                  + [jax.ShapeDtypeStruct((ROPE_DIM, seq), F32)] * 2,
        compiler_params=_params(("parallel",), 16 * ts * V7X_LANES * 4 + (4 << 20)),
        name="rope_tables",
    )(invf_row)


def _l0_mixer_kernel(x_ref, gmix_ref, win_ref, gv_ref, ws_ref, bs_ref, wpool_ref, psc_ref, wout_ref,
                     o_ref, h_ref, u_ref, v_ref, z_ref, cat_ref, *, tm, tiles_per_seq, d_a, d_g):
    i = pl.program_id(0)
    seq_tile = i % tiles_per_seq
    nb = tm // GMLP_BLOCK

    h_ref[...] = _rms(x_ref[...], gmix_ref[...]).astype(BF16)

    u_ref[...] = _gelu_tanh(_dot(h_ref[...], win_ref[:, 0:d_a]))
    v_ref[...] = _rms(_gelu_tanh(_dot(h_ref[...], win_ref[:, d_a:2 * d_a])), gv_ref[...]).astype(BF16)

    tt = lax.broadcasted_iota(jnp.int32, (GMLP_BLOCK, GMLP_BLOCK), 0)
    ss = lax.broadcasted_iota(jnp.int32, (GMLP_BLOCK, GMLP_BLOCK), 1)
    tri = (ss // CHUNK <= tt // CHUNK).astype(F32)
    hd_w = d_a // A_HEADS
    for hd in range(A_HEADS):
        c0 = hd * hd_w
        wm = (ws_ref[hd] * tri).astype(BF16)
        rhs = jnp.concatenate(
            [v_ref[n * GMLP_BLOCK:(n + 1) * GMLP_BLOCK, c0:c0 + hd_w] for n in range(nb)], axis=1)
        mix = _dot(wm, rhs)
        bias = bs_ref[hd]
        for n in range(nb):
            r0 = n * GMLP_BLOCK
            a = u_ref[r0:r0 + GMLP_BLOCK, c0:c0 + hd_w] * (mix[:, n * hd_w:(n + 1) * hd_w] + bias)
            cat_ref[r0:r0 + GMLP_BLOCK, c0:c0 + hd_w] = a.astype(BF16)

    @pl.when(seq_tile == 0)
    def _():
        z_ref[0:POOL_HALO, :] = jnp.zeros((POOL_HALO, z_ref.shape[1]), F32)

    z_ref[POOL_HALO:POOL_HALO + tm, :] = _dot(h_ref[...], win_ref[:, 2 * d_a:])
    pos1 = lax.broadcasted_iota(jnp.int32, (tm, d_g), 0) + (seq_tile * tm + 1)
    for g, win in enumerate(POOL_WINDOWS):
        zz = z_ref[:, g * d_g:(g + 1) * d_g]
        s = zz
        k = 1
        while k < win:
            s = s + pltpu.roll(s, k, axis=0)
            k *= 2
        cnt = jnp.minimum(pos1, win).astype(F32)
        d = (s[POOL_HALO:] / cnt - zz[POOL_HALO:]).astype(BF16)
        y = _dot(d, wpool_ref[g]) * psc_ref[:, g * d_g:(g + 1) * d_g]
        cat_ref[:, d_a + g * d_g:d_a + (g + 1) * d_g] = y.astype(BF16)
    z_ref[0:POOL_HALO, :] = z_ref[tm:tm + POOL_HALO, :]

    o_ref[...] = x_ref[...] + _dot(cat_ref[...], wout_ref[...])


def _l0_mixer(x2d, gmix, w_in, g_v, w_s, b_s_b, w_pool, pscale, w_out, *, seq, tm):
    t, d = x2d.shape
    n_in = w_in.shape[1]
    d_a = g_v.shape[1]
    d_b = pscale.shape[1]
    d_g = d_b // len(POOL_WINDOWS)
    assert n_in == 2 * d_a + d_b and seq % tm == 0 and tm % GMLP_BLOCK == 0
    vmem = (4 * tm * d * 4
            + w_in.size * 2 + w_out.size * 2 + w_pool.size * 2 + (w_s.size + b_s_b.size) * 4
            + tm * d * 2 + tm * d_a * 4 + tm * d_a * 2 + (tm + POOL_HALO) * d_b * 4 + tm * (d_a + d_b) * 2
            + 4 * tm * d_a * 4)
    return pl.pallas_call(
        functools.partial(_l0_mixer_kernel, tm=tm, tiles_per_seq=seq // tm, d_a=d_a, d_g=d_g),
        grid=(t // tm,),
        in_specs=[pl.BlockSpec((tm, d), lambda i: (i, 0)),
                  _const_spec(gmix.shape), _const_spec(w_in.shape), _const_spec(g_v.shape),
                  _const_spec(w_s.shape), _const_spec(b_s_b.shape), _const_spec(w_pool.shape),
                  _const_spec(pscale.shape), _const_spec(w_out.shape)],
        out_specs=pl.BlockSpec((tm, d), lambda i: (i, 0)),
        out_shape=jax.ShapeDtypeStruct((t, d), F32),
        scratch_shapes=[pltpu.VMEM((tm, d), BF16),
                        pltpu.VMEM((tm, d_a), F32),
                        pltpu.VMEM((tm, d_a), BF16),
                        pltpu.VMEM((tm + POOL_HALO, d_b), F32),
                        pltpu.VMEM((tm, d_a + d_b), BF16)],
        compiler_params=_params(("arbitrary",), vmem),
        name="l0_mixer",
    )(x2d, gmix, w_in, g_v, w_s, b_s_b, w_pool, pscale, w_out)


def _ffn_kernel(*refs, pre_proj, final_norm, n_cast):
    it = iter(refs)
    x_ref = next(it)
    attn_ref, wo_ref = (next(it), next(it)) if pre_proj else (None, None)
    g_ref, wg_ref, wu_ref, wd_ref = next(it), next(it), next(it), next(it)
    gfin_ref = next(it) if final_norm else None
    cast_in = [next(it) for _ in range(n_cast)]
    o_ref = next(it)
    cast_out = [next(it) for _ in range(n_cast)]
    h_ref = next(it)
    f = pl.program_id(1)

    @pl.when(f == 0)
    def _():
        x = x_ref[...]
        if pre_proj:
            x = x + _dot(attn_ref[...], wo_ref[...])
        h_ref[...] = _rms(x, g_ref[...]).astype(BF16)
        o_ref[...] = x

    h = h_ref[...]
    gate = _dot(h, wg_ref[...].astype(BF16))
    up = _dot(h, wu_ref[...].astype(BF16))
    act = (gate * jax.nn.sigmoid(gate) * up).astype(BF16)
    o_ref[...] += _dot(act, wd_ref[...].astype(BF16))

    for src_ref, dst_ref in zip(cast_in, cast_out):
        dst_ref[...] = src_ref[...].astype(BF16)

    if final_norm:
        @pl.when(f == pl.num_programs(1) - 1)
        def _():
            o_ref[...] = _rms(o_ref[...], gfin_ref[...])


def _ffn(x2d, g, w_gate, w_up, w_down, *, tm, tf, attn=None, w_o=None, g_final=None, cast_next=()):
    t, d = x2d.shape
    dff = w_gate.shape[1]
    assert t % tm == 0 and dff % tf == 0
    ni, nf = t // tm, dff // tf
    pre_proj = attn is not None
    final_norm = g_final is not None
    wbytes = w_gate.dtype.itemsize
    args = [x2d]
    in_specs = [pl.BlockSpec((tm, d), lambda i, f: (i, 0))]
    vmem = 4 * tm * d * 4 + tm * d * 2 + 2 * 3 * d * tf * wbytes + 3 * d * tf * 2 + 4 * tm * tf * 4
    if pre_proj:
        args += [attn, w_o]
        in_specs += [pl.BlockSpec((tm, attn.shape[1]), lambda i, f: (i, 0)), _const_spec(w_o.shape)]
        vmem += 2 * tm * attn.shape[1] * 2 + w_o.size * 2 + tm * d * 4
    args += [g, w_gate, w_up, w_down]
    in_specs += [_const_spec(g.shape),
                 pl.BlockSpec((d, tf), lambda i, f: (0, f)),
                 pl.BlockSpec((d, tf), lambda i, f: (0, f)),
                 pl.BlockSpec((tf, d), lambda i, f: (f, 0))]
    if final_norm:
        args.append(g_final)
        in_specs.append(_const_spec(g_final.shape))
    out_specs = [pl.BlockSpec((tm, d), lambda i, f: (i, 0))]
    out_shape = [jax.ShapeDtypeStruct((t, d), F32)]
    for w in cast_next:
        r, c = w.shape
        if r % ni == 0 and c % nf == 0:
            blk, imap = (r // ni, c // nf), (lambda i, f: (i, f))
        else:
            assert r % nf == 0 and c % ni == 0
            blk, imap = (r // nf, c // ni), (lambda i, f: (f, i))
        args.append(w)
        in_specs.append(pl.BlockSpec(blk, imap))
        out_specs.append(pl.BlockSpec(blk, imap))
        out_shape.append(jax.ShapeDtypeStruct(w.shape, BF16))
        vmem += 2 * blk[0] * blk[1] * 6
    res = pl.pallas_call(
        functools.partial(_ffn_kernel, pre_proj=pre_proj, final_norm=final_norm, n_cast=len(cast_next)),
        grid=(ni, nf),
        in_specs=in_specs,
        out_specs=out_specs,
        out_shape=out_shape,
        scratch_shapes=[pltpu.VMEM((tm, d), BF16)],
        compiler_params=_params(("parallel", "arbitrary"), vmem),
        name="ffn_final" if final_norm else "ffn",
    )(*args)
    return res if cast_next else res[0]


def _mla_proj_kernel(x_ref, gmix_ref, win_ref, gcq_ref, gckv_ref, wuq_tr_ref, wkn_ref, wv_tr_ref,
                     cos_ref, sin_ref, cos_tr_ref, sin_tr_ref,
                     q_tr_ref, kn_ref, kr_ref, v_tr_ref, h_ref, cq_tr_ref, ckv_ref, ckv_tr_ref, *, chunk):
    h_ref[...] = _rms(x_ref[...], gmix_ref[...]).astype(BF16)
    p = _dot(h_ref[...], win_ref[...])
    cq_tr_ref[...] = _rms(p[:, 0:Q_LORA], gcq_ref[...]).T.astype(BF16)
    ckv = _rms(p[:, Q_LORA:Q_LORA + KV_LORA], gckv_ref[...])
    ckv_ref[...] = ckv.astype(BF16)
    ckv_tr_ref[...] = ckv.T.astype(BF16)
    a = p[:, Q_LORA + KV_LORA:]
    kr_ref[...] = (a * cos_ref[...] + pltpu.roll(a, ROPE_DIM, axis=1) * sin_ref[...]).astype(BF16)

    rope_lo, rope_hi = NOPE_DIM, NOPE_DIM + ROPE_DIM
    for hd in range(C_HEADS):
        r0 = hd * QK_PAD
        qh = _dot(wuq_tr_ref[r0:r0 + QK_PAD, :], cq_tr_ref[...])
        q_tr_ref[r0:r0 + rope_lo, :] = qh[0:rope_lo].astype(BF16)
        roped = qh[rope_lo:rope_hi] * cos_tr_ref[...] + qh[rope_hi:] * sin_tr_ref[...]
        q_tr_ref[r0 + rope_lo:r0 + rope_hi, :] = roped.astype(BF16)
        q_tr_ref[r0 + rope_hi:r0 + QK_PAD, :] = jnp.zeros((QK_PAD - rope_hi, qh.shape[1]), BF16)

    for c in range(kn_ref.shape[1] // chunk):
        sl = slice(c * chunk, (c + 1) * chunk)
        kn_ref[:, sl] = _dot(ckv_ref[...], wkn_ref[:, sl]).astype(BF16)
        v_tr_ref[sl, :] = _dot(wv_tr_ref[sl, :], ckv_tr_ref[...]).astype(BF16)


def _mla_proj(x2d, gmix, w_in_ext, g_cq_s, g_ckv, w_uq_tr, w_kn, w_v_tr, cos_t, sin_t, cos_tr, sin_tr,
              *, seq, tm):
    t, d = x2d.shape
    n_q = w_uq_tr.shape[0]
    n_kv = w_kn.shape[1]
    tiles_per_seq = seq // tm
    assert seq % tm == 0 and w_in_ext.shape[1] == Q_LORA + KV_LORA + 2 * ROPE_DIM
    vmem = (2 * tm * d * 4 + (w_in_ext.size + w_uq_tr.size + w_kn.size + w_v_tr.size) * 2
            + 2 * 2 * tm * (V7X_LANES + ROPE_DIM) * 4
            + 2 * tm * (n_q + 2 * n_kv + V7X_LANES) * 2
            + tm * d * 2 + tm * (Q_LORA + 2 * KV_LORA) * 2
            + 4 * tm * w_in_ext.shape[1] * 4)
    row = lambda i: (i, 0)
    col = lambda i: (0, i)
    return pl.pallas_call(
        functools.partial(_mla_proj_kernel, chunk=512),
        grid=(t // tm,),
        in_specs=[pl.BlockSpec((tm, d), row),
                  _const_spec(gmix.shape), _const_spec(w_in_ext.shape), _const_spec(g_cq_s.shape),
                  _const_spec(g_ckv.shape), _const_spec(w_uq_tr.shape), _const_spec(w_kn.shape),
                  _const_spec(w_v_tr.shape),
                  pl.BlockSpec((tm, V7X_LANES), lambda i: (i % tiles_per_seq, 0)),
                  pl.BlockSpec((tm, V7X_LANES), lambda i: (i % tiles_per_seq, 0)),
                  pl.BlockSpec((ROPE_DIM, tm), lambda i: (0, i % tiles_per_seq)),
                  pl.BlockSpec((ROPE_DIM, tm), lambda i: (0, i % tiles_per_seq))],
        out_specs=[pl.BlockSpec((n_q, tm), col), pl.BlockSpec((tm, n_kv), row),
                   pl.BlockSpec((tm, V7X_LANES), row), pl.BlockSpec((n_kv, tm), col)],
        out_shape=[jax.ShapeDtypeStruct((n_q, t), BF16), jax.ShapeDtypeStruct((t, n_kv), BF16),
                   jax.ShapeDtypeStruct((t, V7X_LANES), BF16), jax.ShapeDtypeStruct((n_kv, t), BF16)],
        scratch_shapes=[pltpu.VMEM((tm, d), BF16), pltpu.VMEM((Q_LORA, tm), BF16),
                        pltpu.VMEM((tm, KV_LORA), BF16), pltpu.VMEM((KV_LORA, tm), BF16)],
        compiler_params=_params(("parallel",), vmem),
        name="mla_proj",
    )(x2d, gmix, w_in_ext, g_cq_s, g_ckv, w_uq_tr, w_kn, w_v_tr, cos_t, sin_t, cos_tr, sin_tr)


def _attn_kernel(q_tr_ref, kn_ref, kr_ref, v_tr_ref, o_ref, kcat_ref, m_ref, l_ref, acc_ref, *, tk, tq, skew):
    seq = kcat_ref.shape[0]
    kcat_ref[:, 0:NOPE_DIM] = kn_ref[...]
    kcat_ref[:, NOPE_DIM:] = kr_ref[...]

    units = []
    for j in range(seq // tk):
        for i in range(seq // tq):
            k_lo, q_lo = j * tk, i * tq
            k_hi = min(k_lo + tk, q_lo + tq)
            if k_hi > k_lo:
                units.append((j == 0, k_lo, k_hi, q_lo))

    def scores(unit):
        _, k_lo, k_hi, q_lo = unit
        s = _dot(kcat_ref[k_lo:k_hi, :], q_tr_ref[:, q_lo:q_lo + tq])
        if k_hi > q_lo + CHUNK:
            kc = (lax.broadcasted_iota(jnp.int32, s.shape, 0) + k_lo) // CHUNK
            qc = (lax.broadcasted_iota(jnp.int32, s.shape, 1) + q_lo) // CHUNK
            s = jnp.where(kc <= qc, s, -jnp.inf)
        return s

    pending = [scores(u) for u in units[:skew]]
    for n, (first, k_lo, k_hi, q_lo) in enumerate(units):
        if n + skew < len(units):
            pending.append(scores(units[n + skew]))
        s = pending.pop(0)
        qs = slice(q_lo, q_lo + tq)
        v_j = v_tr_ref[:, k_lo:k_hi]
        m_blk = jnp.max(s, axis=0, keepdims=True)
        if first:
            m_new = m_blk
            p = jnp.exp2(s - m_new)
            l_ref[:, qs] = jnp.sum(p, axis=0, keepdims=True)
            acc_ref[:, qs] = _dot(v_j, p.astype(BF16))
        else:
            m_prev = m_ref[:, qs]
            m_new = jnp.maximum(m_prev, m_blk)
            alpha = jnp.exp2(m_prev - m_new)
            p = jnp.exp2(s - m_new)
            l_ref[:, qs] = alpha * l_ref[:, qs] + jnp.sum(p, axis=0, keepdims=True)
            acc_ref[:, qs] = alpha * acc_ref[:, qs] + _dot(v_j, p.astype(BF16))
        m_ref[:, qs] = m_new

    for i in range(seq // tq):
        qs = slice(i * tq, (i + 1) * tq)
        o_ref[qs, :] = (acc_ref[:, qs] / l_ref[:, qs]).T.astype(o_ref.dtype)


def _attention(q_tr, kn, kr, v_tr, *, batch, seq, tk, tq, skew):
    t = kn.shape[0]
    assert seq % tk == 0 and seq % tq == 0 and tk % CHUNK == 0 and tq % CHUNK == 0
    assert V_DIM == V7X_LANES and NOPE_DIM == V7X_LANES
    vmem = (2 * (QK_PAD + V_DIM) * seq * 2 + 2 * 2 * seq * V7X_LANES * 2 + 2 * seq * V_DIM * 2
            + seq * QK_PAD * 2 + 2 * 8 * seq * 4 + V_DIM * seq * 4 + 24 * tk * tq * 4)
    return pl.pallas_call(
        functools.partial(_attn_kernel, tk=tk, tq=tq, skew=skew),
        grid=(batch, C_HEADS),
        in_specs=[pl.BlockSpec((QK_PAD, seq), lambda b, h: (h, b)),
                  pl.BlockSpec((seq, NOPE_DIM), lambda b, h: (b, h)),
                  pl.BlockSpec((seq, V7X_LANES), lambda b, h: (b, 0)),
                  pl.BlockSpec((V_DIM, seq), lambda b, h: (h, b))],
        out_specs=pl.BlockSpec((seq, V_DIM), lambda b, h: (b, h)),
        out_shape=jax.ShapeDtypeStruct((t, C_HEADS * V_DIM), BF16),
        scratch_shapes=[pltpu.VMEM((seq, QK_PAD), BF16),
                        pltpu.VMEM((1, seq), F32), pltpu.VMEM((1, seq), F32),
                        pltpu.VMEM((V_DIM, seq), F32)],
        compiler_params=_params(("parallel", "parallel"), vmem),
        name="attention",
    )(q_tr, kn, kr, v_tr)


def _swap_halves(w):
    half = w.shape[-1] // 2
    return jnp.concatenate([w[..., half:], w[..., :half]], axis=-1)


def kernel(x, g_mix, g_ffn, g_final, w_in_ab, g_v, w_s, b_s, w_pool, pool_scale, w_out_ab, w_in_c, g_cq, g_ckv, w_uq, w_ukv, w_out_c, w_gate, w_up, w_down):
    batch, seq, d = x.shape
    x2d = x.reshape(batch * seq, d)
    row = lambda a: a.reshape(1, -1)

    b_s_b = jnp.broadcast_to(b_s[0][:, :, None], (A_HEADS, GMLP_BLOCK, GMLP_BLOCK))
    x1 = _l0_mixer(x2d, row(g_mix[0]), w_in_ab[0].astype(BF16), row(g_v[0]), w_s[0], b_s_b,
                   w_pool[0].astype(BF16), row(pool_scale[0]), w_out_ab[0].astype(BF16),
                   seq=seq, tm=512)
    x2, w_gate1, w_up1, w_down1 = _ffn(x1, row(g_ffn[0]), w_gate[0], w_up[0], w_down[0], tm=1024, tf=256,
                                       cast_next=(w_gate[1], w_up[1], w_down[1]))

    inv_freq = ROPE_THETA ** (-jnp.arange(0, ROPE_DIM, 2, dtype=F32) / ROPE_DIM)
    cos_t, sin_t, cos_tr, sin_tr = _rope_tables(
        jnp.tile(inv_freq, V7X_LANES // inv_freq.shape[0]).reshape(1, -1), seq, 512)
    wc = w_in_c[0]
    w_kr = wc[:, Q_LORA + KV_LORA:]
    w_in_ext = jnp.concatenate([wc, _swap_halves(w_kr)], axis=1).astype(BF16)
    wq = w_uq[0].reshape(Q_LORA, C_HEADS, NOPE_DIM + ROPE_DIM)
    w_uq_ext = jnp.concatenate([wq, _swap_halves(wq[..., NOPE_DIM:])], axis=-1)
    w_uq_tr = w_uq_ext.reshape(Q_LORA, C_HEADS * QK_PAD).T.astype(BF16)
    wkv = w_ukv[0].reshape(KV_LORA, C_HEADS, NOPE_DIM + V_DIM)
    w_kn = wkv[..., :NOPE_DIM].reshape(KV_LORA, -1).astype(BF16)
    w_v_tr = wkv[..., NOPE_DIM:].reshape(KV_LORA, -1).T.astype(BF16)
    score_scale = (NOPE_DIM + ROPE_DIM) ** -0.5 * 1.4426950408889634
    q_tr, kn, kr, v_tr = _mla_proj(x2, row(g_mix[1]), w_in_ext, row(g_cq[0]) * score_scale,
                                   row(g_ckv[0]), w_uq_tr, w_kn, w_v_tr, cos_t, sin_t, cos_tr, sin_tr,
                                   seq=seq, tm=512)
    o = _attention(q_tr, kn, kr, v_tr, batch=batch, seq=seq, tk=512, tq=256, skew=3)
    out = _ffn(x2, row(g_ffn[1]), w_gate1, w_up1, w_down1,
               tm=512, tf=512, attn=o, w_o=w_out_c[0].astype(BF16), g_final=row(g_final))
    return out.reshape(batch, seq, d)
```

```python
import functools

import jax
import jax.numpy as jnp
from jax import lax
from jax.experimental import pallas as pl
from jax.experimental.pallas import tpu as pltpu

F32 = jnp.float32
BF16 = jnp.bfloat16

CHUNK = 64
EPS = 1e-6
GMLP_BLOCK = 128
A_HEADS = 8
POOL_WINDOWS = (2, 4, 8, 16)
POOL_HALO = 16
C_HEADS = 16
Q_LORA = 512
KV_LORA = 512
NOPE_DIM = 128
ROPE_DIM = 64
V_DIM = 128
ROPE_THETA = 10000.0
QK_PAD = 256

V7X_LANES = 128
V7X_SCOPED_VMEM_BYTES = 60000 * 1024


def _rms(xf, g):
    ms = jnp.mean(xf * xf, axis=-1, keepdims=True)
    return xf * lax.rsqrt(ms + EPS) * g


def _gelu_tanh(x):
    c = 0.7978845608028654
    return 0.5 * x * (1.0 + jnp.tanh(c * (x + 0.044715 * (x * x * x))))


def _dot(a, b):
    return jnp.dot(a, b, preferred_element_type=F32)


def _const_spec(shape):
    nd = len(shape)
    return pl.BlockSpec(shape, lambda *_: (0,) * nd, pipeline_mode=pl.Buffered(1))


def _params(semantics, vmem_bytes):
    return pltpu.CompilerParams(dimension_semantics=semantics,
                                vmem_limit_bytes=min(int(vmem_bytes), V7X_SCOPED_VMEM_BYTES))


def _rope_table_kernel(invf_ref, cos_ref, sin_ref, cos_tr_ref, sin_tr_ref, *, ts):
    i = pl.program_id(0)
    pos = (lax.broadcasted_iota(jnp.int32, (ts, V7X_LANES), 0) + i * ts).astype(F32)
    lane = lax.broadcasted_iota(jnp.int32, (ts, V7X_LANES), 1)
    ang = pos * invf_ref[...]
    c = jnp.cos(ang)
    s = jnp.sin(ang)
    half = ROPE_DIM // 2
    cos_t = jnp.where(lane < ROPE_DIM, c, 0.0)
    sin_t = jnp.where(lane < half, -s, jnp.where(lane < ROPE_DIM, s, 0.0))
    cos_ref[...] = cos_t
    sin_ref[...] = sin_t
    cos_tr_ref[...] = cos_t.T[0:ROPE_DIM]
    sin_tr_ref[...] = sin_t.T[0:ROPE_DIM]


def _rope_tables(invf_row, seq, ts):
    return pl.pallas_call(
        functools.partial(_rope_table_kernel, ts=ts),
        grid=(seq // ts,),
        in_specs=[pl.BlockSpec((1, V7X_LANES), lambda i: (0, 0))],
        out_specs=[pl.BlockSpec((ts, V7X_LANES), lambda i: (i, 0)),
                   pl.BlockSpec((ts, V7X_LANES), lambda i: (i, 0)),
                   pl.BlockSpec((ROPE_DIM, ts), lambda i: (0, i)),
                   pl.BlockSpec((ROPE_DIM, ts), lambda i: (0, i))],
        out_shape=[jax.ShapeDtypeStruct((seq, V7X_LANES), F32)] * 2
                  + [jax.ShapeDtypeStruct((ROPE_DIM, seq), F32)] * 2,
        compiler_params=_params(("parallel",), 16 * ts * V7X_LANES * 4 + (4 << 20)),
        name="rope_tables",
    )(invf_row)


def _l0_mixer_kernel(x_ref, gmix_ref, win_ref, gv_ref, ws_ref, bs_ref, wpool_ref, psc_ref, wout_ref,
                     o_ref, h_ref, u_ref, v_ref, z_ref, cat_ref, *, tm, tiles_per_seq, d_a, d_g):
    i = pl.program_id(0)
    seq_tile = i % tiles_per_seq
    nb = tm // GMLP_BLOCK

    h_ref[...] = _rms(x_ref[...], gmix_ref[...]).astype(BF16)

    u_ref[...] = _gelu_tanh(_dot(h_ref[...], win_ref[:, 0:d_a]))
    v_ref[...] = _rms(_gelu_tanh(_dot(h_ref[...], win_ref[:, d_a:2 * d_a])), gv_ref[...]).astype(BF16)

    tt = lax.broadcasted_iota(jnp.int32, (GMLP_BLOCK, GMLP_BLOCK), 0)
    ss = lax.broadcasted_iota(jnp.int32, (GMLP_BLOCK, GMLP_BLOCK), 1)
    tri = (ss // CHUNK <= tt // CHUNK).astype(F32)
    hd_w = d_a // A_HEADS
    for hd in range(A_HEADS):
        c0 = hd * hd_w
        wm = (ws_ref[hd] * tri).astype(BF16)
        rhs = jnp.concatenate(
            [v_ref[n * GMLP_BLOCK:(n + 1) * GMLP_BLOCK, c0:c0 + hd_w] for n in range(nb)], axis=1)
        mix = _dot(wm, rhs)
        bias = bs_ref[hd]
        for n in range(nb):
            r0 = n * GMLP_BLOCK
            a = u_ref[r0:r0 + GMLP_BLOCK, c0:c0 + hd_w] * (mix[:, n * hd_w:(n + 1) * hd_w] + bias)
            cat_ref[r0:r0 + GMLP_BLOCK, c0:c0 + hd_w] = a.astype(BF16)

    @pl.when(seq_tile == 0)
    def _():
        z_ref[0:POOL_HALO, :] = jnp.zeros((POOL_HALO, z_ref.shape[1]), F32)

    z_ref[POOL_HALO:POOL_HALO + tm, :] = _dot(h_ref[...], win_ref[:, 2 * d_a:])
    pos1 = lax.broadcasted_iota(jnp.int32, (tm, d_g), 0) + (seq_tile * tm + 1)
    for g, win in enumerate(POOL_WINDOWS):
        zz = z_ref[:, g * d_g:(g + 1) * d_g]
        s = zz
        k = 1
        while k < win:
            s = s + pltpu.roll(s, k, axis=0)
            k *= 2
        cnt = jnp.minimum(pos1, win).astype(F32)
        d = (s[POOL_HALO:] / cnt - zz[POOL_HALO:]).astype(BF16)
        y = _dot(d, wpool_ref[g]) * psc_ref[:, g * d_g:(g + 1) * d_g]
        cat_ref[:, d_a + g * d_g:d_a + (g + 1) * d_g] = y.astype(BF16)
    z_ref[0:POOL_HALO, :] = z_ref[tm:tm + POOL_HALO, :]

    o_ref[...] = x_ref[...] + _dot(cat_ref[...], wout_ref[...])


def _l0_mixer(x2d, gmix, w_in, g_v, w_s, b_s_b, w_pool, pscale, w_out, *, seq, tm):
    t, d = x2d.shape
    n_in = w_in.shape[1]
    d_a = g_v.shape[1]
    d_b = pscale.shape[1]
    d_g = d_b // len(POOL_WINDOWS)
    assert n_in == 2 * d_a + d_b and seq % tm == 0 and tm % GMLP_BLOCK == 0
    vmem = (4 * tm * d * 4
            + w_in.size * 2 + w_out.size * 2 + w_pool.size * 2 + (w_s.size + b_s_b.size) * 4
            + tm * d * 2 + tm * d_a * 4 + tm * d_a * 2 + (tm + POOL_HALO) * d_b * 4 + tm * (d_a + d_b) * 2
            + 4 * tm * d_a * 4)
    return pl.pallas_call(
        functools.partial(_l0_mixer_kernel, tm=tm, tiles_per_seq=seq // tm, d_a=d_a, d_g=d_g),
        grid=(t // tm,),
        in_specs=[pl.BlockSpec((tm, d), lambda i: (i, 0)),
                  _const_spec(gmix.shape), _const_spec(w_in.shape), _const_spec(g_v.shape),
                  _const_spec(w_s.shape), _const_spec(b_s_b.shape), _const_spec(w_pool.shape),
                  _const_spec(pscale.shape), _const_spec(w_out.shape)],
        out_specs=pl.BlockSpec((tm, d), lambda i: (i, 0)),
        out_shape=jax.ShapeDtypeStruct((t, d), F32),
        scratch_shapes=[pltpu.VMEM((tm, d), BF16),
                        pltpu.VMEM((tm, d_a), F32),
                        pltpu.VMEM((tm, d_a), BF16),
                        pltpu.VMEM((tm + POOL_HALO, d_b), F32),
                        pltpu.VMEM((tm, d_a + d_b), BF16)],
        compiler_params=_params(("arbitrary",), vmem),
        name="l0_mixer",
    )(x2d, gmix, w_in, g_v, w_s, b_s_b, w_pool, pscale, w_out)


def _ffn_kernel(*refs, pre_proj, final_norm, n_cast):
    it = iter(refs)
    x_ref = next(it)
    attn_ref, wo_ref = (next(it), next(it)) if pre_proj else (None, None)
    g_ref, wg_ref, wu_ref, wd_ref = next(it), next(it), next(it), next(it)
    gfin_ref = next(it) if final_norm else None
    cast_in = [next(it) for _ in range(n_cast)]
    o_ref = next(it)
    cast_out = [next(it) for _ in range(n_cast)]
    h_ref = next(it)
    f = pl.program_id(1)

    @pl.when(f == 0)
    def _():
        x = x_ref[...]
        if pre_proj:
            x = x + _dot(attn_ref[...], wo_ref[...])
        h_ref[...] = _rms(x, g_ref[...]).astype(BF16)
        o_ref[...] = x

    h = h_ref[...]
    gate = _dot(h, wg_ref[...].astype(BF16))
    up = _dot(h, wu_ref[...].astype(BF16))
    act = (gate * jax.nn.sigmoid(gate) * up).astype(BF16)
    o_ref[...] += _dot(act, wd_ref[...].astype(BF16))

    for src_ref, dst_ref in zip(cast_in, cast_out):
        dst_ref[...] = src_ref[...].astype(BF16)

    if final_norm:
        @pl.when(f == pl.num_programs(1) - 1)
        def _():
            o_ref[...] = _rms(o_ref[...], gfin_ref[...])


def _ffn(x2d, g, w_gate, w_up, w_down, *, tm, tf, layer=None, attn=None, w_o=None, g_final=None,
         cast_next=(), cast_layer=None):
    t, d = x2d.shape
    dff = w_gate.shape[-1]
    assert t % tm == 0 and dff % tf == 0
    ni, nf = t // tm, dff // tf
    pre_proj = attn is not None
    final_norm = g_final is not None
    wbytes = w_gate.dtype.itemsize

    def wspec(w, lyr, blk, imap):
        if w.ndim == 2:
            return pl.BlockSpec(blk, imap)
        return pl.BlockSpec((None,) + blk, lambda i, f: (lyr,) + imap(i, f))

    args = [x2d]
    in_specs = [pl.BlockSpec((tm, d), lambda i, f: (i, 0))]
    vmem = 4 * tm * d * 4 + tm * d * 2 + 2 * 3 * d * tf * wbytes + 3 * d * tf * 2 + 4 * tm * tf * 4
    if pre_proj:
        args += [attn, w_o]
        in_specs += [pl.BlockSpec((tm, attn.shape[1]), lambda i, f: (i, 0)), _const_spec(w_o.shape)]
        vmem += 2 * tm * attn.shape[1] * 2 + w_o.size * 2 + tm * d * 4
    args += [g, w_gate, w_up, w_down]
    in_specs += [_const_spec(g.shape),
                 wspec(w_gate, layer, (d, tf), lambda i, f: (0, f)),
                 wspec(w_up, layer, (d, tf), lambda i, f: (0, f)),
                 wspec(w_down, layer, (tf, d), lambda i, f: (f, 0))]
    if final_norm:
        args.append(g_final)
        in_specs.append(_const_spec(g_final.shape))
    out_specs = [pl.BlockSpec((tm, d), lambda i, f: (i, 0))]
    out_shape = [jax.ShapeDtypeStruct((t, d), F32)]
    for w in cast_next:
        r, c = w.shape[-2:]
        if r % ni == 0 and c % nf == 0:
            blk, imap = (r // ni, c // nf), (lambda i, f: (i, f))
        else:
            assert r % nf == 0 and c % ni == 0
            blk, imap = (r // nf, c // ni), (lambda i, f: (f, i))
        args.append(w)
        in_specs.append(wspec(w, cast_layer, blk, imap))
        out_specs.append(pl.BlockSpec(blk, imap))
        out_shape.append(jax.ShapeDtypeStruct((r, c), BF16))
        vmem += 2 * blk[0] * blk[1] * 6
    res = pl.pallas_call(
        functools.partial(_ffn_kernel, pre_proj=pre_proj, final_norm=final_norm, n_cast=len(cast_next)),
        grid=(ni, nf),
        in_specs=in_specs,
        out_specs=out_specs,
        out_shape=out_shape,
        scratch_shapes=[pltpu.VMEM((tm, d), BF16)],
        compiler_params=_params(("parallel", "arbitrary"), vmem),
        name="ffn_final" if final_norm else "ffn",
    )(*args)
    return res if cast_next else res[0]


def _mla_proj_kernel(x_ref, gmix_ref, win_ref, gcq_ref, gckv_ref, wuq_tr_ref, wkn_ref, wv_tr_ref,
                     cos_ref, sin_ref, cos_tr_ref, sin_tr_ref,
                     q_tr_ref, kn_ref, kr_ref, v_tr_ref, h_ref, cq_tr_ref, ckv_ref, ckv_tr_ref, *, chunk):
    h_ref[...] = _rms(x_ref[...], gmix_ref[...]).astype(BF16)
    p = _dot(h_ref[...], win_ref[...])
    cq_tr_ref[...] = _rms(p[:, 0:Q_LORA], gcq_ref[...]).T.astype(BF16)
    ckv = _rms(p[:, Q_LORA:Q_LORA + KV_LORA], gckv_ref[...])
    ckv_ref[...] = ckv.astype(BF16)
    ckv_tr_ref[...] = ckv.T.astype(BF16)
    a = p[:, Q_LORA + KV_LORA:]
    kr_ref[...] = (a * cos_ref[...] + pltpu.roll(a, ROPE_DIM, axis=1) * sin_ref[...]).astype(BF16)

    rope_lo, rope_hi = NOPE_DIM, NOPE_DIM + ROPE_DIM
    for hd in range(C_HEADS):
        r0 = hd * QK_PAD
        qh = _dot(wuq_tr_ref[r0:r0 + QK_PAD, :], cq_tr_ref[...])
        q_tr_ref[r0:r0 + rope_lo, :] = qh[0:rope_lo].astype(BF16)
        roped = qh[rope_lo:rope_hi] * cos_tr_ref[...] + qh[rope_hi:] * sin_tr_ref[...]
        q_tr_ref[r0 + rope_lo:r0 + rope_hi, :] = roped.astype(BF16)
        q_tr_ref[r0 + rope_hi:r0 + QK_PAD, :] = jnp.zeros((QK_PAD - rope_hi, qh.shape[1]), BF16)

    for c in range(kn_ref.shape[1] // chunk):
        sl = slice(c * chunk, (c + 1) * chunk)
        kn_ref[:, sl] = _dot(ckv_ref[...], wkn_ref[:, sl]).astype(BF16)
        v_tr_ref[sl, :] = _dot(wv_tr_ref[sl, :], ckv_tr_ref[...]).astype(BF16)


def _mla_proj(x2d, gmix, w_in_ext, g_cq_s, g_ckv, w_uq_tr, w_kn, w_v_tr, cos_t, sin_t, cos_tr, sin_tr,
              *, seq, tm):
    t, d = x2d.shape
    n_q = w_uq_tr.shape[0]
    n_kv = w_kn.shape[1]
    tiles_per_seq = seq // tm
    assert seq % tm == 0 and w_in_ext.shape[1] == Q_LORA + KV_LORA + 2 * ROPE_DIM
    vmem = (2 * tm * d * 4 + (w_in_ext.size + w_uq_tr.size + w_kn.size + w_v_tr.size) * 2
            + 2 * 2 * tm * (V7X_LANES + ROPE_DIM) * 4
            + 2 * tm * (n_q + 2 * n_kv + V7X_LANES) * 2
            + tm * d * 2 + tm * (Q_LORA + 2 * KV_LORA) * 2
            + 4 * tm * w_in_ext.shape[1] * 4)
    row = lambda i: (i, 0)
    col = lambda i: (0, i)
    return pl.pallas_call(
        functools.partial(_mla_proj_kernel, chunk=512),
        grid=(t // tm,),
        in_specs=[pl.BlockSpec((tm, d), row),
                  _const_spec(gmix.shape), _const_spec(w_in_ext.shape), _const_spec(g_cq_s.shape),
                  _const_spec(g_ckv.shape), _const_spec(w_uq_tr.shape), _const_spec(w_kn.shape),
                  _const_spec(w_v_tr.shape),
                  pl.BlockSpec((tm, V7X_LANES), lambda i: (i % tiles_per_seq, 0)),
                  pl.BlockSpec((tm, V7X_LANES), lambda i: (i % tiles_per_seq, 0)),
                  pl.BlockSpec((ROPE_DIM, tm), lambda i: (0, i % tiles_per_seq)),
                  pl.BlockSpec((ROPE_DIM, tm), lambda i: (0, i % tiles_per_seq))],
        out_specs=[pl.BlockSpec((n_q, tm), col), pl.BlockSpec((tm, n_kv), row),
                   pl.BlockSpec((tm, V7X_LANES), row), pl.BlockSpec((n_kv, tm), col)],
        out_shape=[jax.ShapeDtypeStruct((n_q, t), BF16), jax.ShapeDtypeStruct((t, n_kv), BF16),
                   jax.ShapeDtypeStruct((t, V7X_LANES), BF16), jax.ShapeDtypeStruct((n_kv, t), BF16)],
        scratch_shapes=[pltpu.VMEM((tm, d), BF16), pltpu.VMEM((Q_LORA, tm), BF16),
                        pltpu.VMEM((tm, KV_LORA), BF16), pltpu.VMEM((KV_LORA, tm), BF16)],
        compiler_params=_params(("parallel",), vmem),
        name="mla_proj",
    )(x2d, gmix, w_in_ext, g_cq_s, g_ckv, w_uq_tr, w_kn, w_v_tr, cos_t, sin_t, cos_tr, sin_tr)


def _attn_kernel(*refs, tk, tq, skew, n_cast):
    q_tr_ref, kn_ref, kr_ref, v_tr_ref = refs[0:4]
    cast_in = refs[4:4 + n_cast]
    o_ref = refs[4 + n_cast]
    cast_out = refs[5 + n_cast:5 + 2 * n_cast]
    kcat_ref, m_ref, l_ref, acc_ref = refs[5 + 2 * n_cast:]
    for src_ref, dst_ref in zip(cast_in, cast_out):
        dst_ref[...] = src_ref[...].astype(BF16)
    seq = kcat_ref.shape[0]
    kcat_ref[:, 0:NOPE_DIM] = kn_ref[...]
    kcat_ref[:, NOPE_DIM:] = kr_ref[...]

    units = []
    for j in range(seq // tk):
        for i in range(seq // tq):
            k_lo, q_lo = j * tk, i * tq
            k_hi = min(k_lo + tk, q_lo + tq)
            if k_hi > k_lo:
                units.append((j == 0, k_lo, k_hi, q_lo))

    def scores(unit):
        _, k_lo, k_hi, q_lo = unit
        s = _dot(kcat_ref[k_lo:k_hi, :], q_tr_ref[:, q_lo:q_lo + tq])
        if k_hi > q_lo + CHUNK:
            kc = (lax.broadcasted_iota(jnp.int32, s.shape, 0) + k_lo) // CHUNK
            qc = (lax.broadcasted_iota(jnp.int32, s.shape, 1) + q_lo) // CHUNK
            s = jnp.where(kc <= qc, s, -jnp.inf)
        return s

    pending = [scores(u) for u in units[:skew]]
    for n, (first, k_lo, k_hi, q_lo) in enumerate(units):
        if n + skew < len(units):
            pending.append(scores(units[n + skew]))
        s = pending.pop(0)
        qs = slice(q_lo, q_lo + tq)
        v_j = v_tr_ref[:, k_lo:k_hi]
        m_blk = jnp.max(s, axis=0, keepdims=True)
        if first:
            m_new = m_blk
            p = jnp.exp2(s - m_new)
            l_ref[:, qs] = jnp.sum(p, axis=0, keepdims=True)
            acc_ref[:, qs] = _dot(v_j, p.astype(BF16))
        else:
            m_prev = m_ref[:, qs]
            m_new = jnp.maximum(m_prev, m_blk)
            alpha = jnp.exp2(m_prev - m_new)
            p = jnp.exp2(s - m_new)
            l_ref[:, qs] = alpha * l_ref[:, qs] + jnp.sum(p, axis=0, keepdims=True)
            acc_ref[:, qs] = alpha * acc_ref[:, qs] + _dot(v_j, p.astype(BF16))
        m_ref[:, qs] = m_new

    for i in range(seq // tq):
        qs = slice(i * tq, (i + 1) * tq)
        o_ref[qs, :] = (acc_ref[:, qs] / l_ref[:, qs]).T.astype(o_ref.dtype)


def _attention(q_tr, kn, kr, v_tr, *, batch, seq, tk, tq, skew, cast_next=(), cast_layer=None):
    t = kn.shape[0]
    steps = batch * C_HEADS
    assert seq % tk == 0 and seq % tq == 0 and tk % CHUNK == 0 and tq % CHUNK == 0
    assert V_DIM == V7X_LANES and NOPE_DIM == V7X_LANES
    vmem = (2 * (QK_PAD + V_DIM) * seq * 2 + 2 * 2 * seq * V7X_LANES * 2 + 2 * seq * V_DIM * 2
            + seq * QK_PAD * 2 + 2 * 8 * seq * 4 + V_DIM * seq * 4 + 24 * tk * tq * 4)
    in_specs = [pl.BlockSpec((QK_PAD, seq), lambda b, h: (h, b)),
                pl.BlockSpec((seq, NOPE_DIM), lambda b, h: (b, h)),
                pl.BlockSpec((seq, V7X_LANES), lambda b, h: (b, 0)),
                pl.BlockSpec((V_DIM, seq), lambda b, h: (h, b))]
    out_specs = [pl.BlockSpec((seq, V_DIM), lambda b, h: (b, h))]
    out_shape = [jax.ShapeDtypeStruct((t, C_HEADS * V_DIM), BF16)]
    for w in cast_next:
        _, r, c = w.shape
        assert r % (steps * 16) == 0
        in_specs.append(pl.BlockSpec((None, r // steps, c), lambda b, h: (cast_layer, b * C_HEADS + h, 0)))
        out_specs.append(pl.BlockSpec((r // steps, c), lambda b, h: (b * C_HEADS + h, 0)))
        out_shape.append(jax.ShapeDtypeStruct((r, c), BF16))
        vmem += 2 * (r // steps) * c * 6
    res = pl.pallas_call(
        functools.partial(_attn_kernel, tk=tk, tq=tq, skew=skew, n_cast=len(cast_next)),
        grid=(batch, C_HEADS),
        in_specs=in_specs,
        out_specs=out_specs,
        out_shape=out_shape,
        scratch_shapes=[pltpu.VMEM((seq, QK_PAD), BF16),
                        pltpu.VMEM((1, seq), F32), pltpu.VMEM((1, seq), F32),
                        pltpu.VMEM((V_DIM, seq), F32)],
        compiler_params=_params(("parallel", "parallel"), vmem),
        name="attention",
    )(q_tr, kn, kr, v_tr, *cast_next)
    return res if cast_next else res[0]


def _swap_halves(w):
    half = w.shape[-1] // 2
    return jnp.concatenate([w[..., half:], w[..., :half]], axis=-1)


def kernel(x, g_mix, g_ffn, g_final, w_in_ab, g_v, w_s, b_s, w_pool, pool_scale, w_out_ab, w_in_c, g_cq, g_ckv, w_uq, w_ukv, w_out_c, w_gate, w_up, w_down):
    batch, seq, d = x.shape
    x2d = x.reshape(batch * seq, d)
    row = lambda a: a.reshape(1, -1)

    b_s_b = jnp.broadcast_to(b_s[0][:, :, None], (A_HEADS, GMLP_BLOCK, GMLP_BLOCK))
    x1 = _l0_mixer(x2d, row(g_mix[0]), w_in_ab[0].astype(BF16), row(g_v[0]), w_s[0], b_s_b,
                   w_pool[0].astype(BF16), row(pool_scale[0]), w_out_ab[0].astype(BF16),
                   seq=seq, tm=512)
    x2 = _ffn(x1, row(g_ffn[0]), w_gate, w_up, w_down, layer=0, tm=1024, tf=256)

    inv_freq = ROPE_THETA ** (-jnp.arange(0, ROPE_DIM, 2, dtype=F32) / ROPE_DIM)
    cos_t, sin_t, cos_tr, sin_tr = _rope_tables(
        jnp.tile(inv_freq, V7X_LANES // inv_freq.shape[0]).reshape(1, -1), seq, 512)
    wc = w_in_c[0]
    w_kr = wc[:, Q_LORA + KV_LORA:]
    w_in_ext = jnp.concatenate([wc, _swap_halves(w_kr)], axis=1).astype(BF16)
    wq = w_uq[0].reshape(Q_LORA, C_HEADS, NOPE_DIM + ROPE_DIM)
    w_uq_ext = jnp.concatenate([wq, _swap_halves(wq[..., NOPE_DIM:])], axis=-1)
    w_uq_tr = w_uq_ext.reshape(Q_LORA, C_HEADS * QK_PAD).T.astype(BF16)
    wkv = w_ukv[0].reshape(KV_LORA, C_HEADS, NOPE_DIM + V_DIM)
    w_kn = wkv[..., :NOPE_DIM].reshape(KV_LORA, -1).astype(BF16)
    w_v_tr = wkv[..., NOPE_DIM:].reshape(KV_LORA, -1).T.astype(BF16)
    score_scale = (NOPE_DIM + ROPE_DIM) ** -0.5 * 1.4426950408889634
    q_tr, kn, kr, v_tr = _mla_proj(x2, row(g_mix[1]), w_in_ext, row(g_cq[0]) * score_scale,
                                   row(g_ckv[0]), w_uq_tr, w_kn, w_v_tr, cos_t, sin_t, cos_tr, sin_tr,
                                   seq=seq, tm=512)
    o, w_gate1, w_up1, w_down1 = _attention(q_tr, kn, kr, v_tr, batch=batch, seq=seq, tk=512, tq=256, skew=3,
                                            cast_next=(w_gate, w_up, w_down), cast_layer=1)
    out = _ffn(x2, row(g_ffn[1]), w_gate1, w_up1, w_down1,
               tm=512, tf=512, attn=o, w_o=w_out_c[0].astype(BF16), g_final=row(g_final))
    return out.reshape(batch, seq, d)
```

```python
import functools

import jax
import jax.numpy as jnp
from jax import lax
from jax.experimental import pallas as pl
from jax.experimental.pallas import tpu as pltpu

F32 = jnp.float32
BF16 = jnp.bfloat16

CHUNK = 64
EPS = 1e-6
GMLP_BLOCK = 128
A_HEADS = 8
POOL_WINDOWS = (2, 4, 8, 16)
POOL_HALO = 16
C_HEADS = 16
Q_LORA = 512
KV_LORA = 512
NOPE_DIM = 128
ROPE_DIM = 64
V_DIM = 128
ROPE_THETA = 10000.0
QK_PAD = 256

V7X_LANES = 128
V7X_SCOPED_VMEM_BYTES = 60000 * 1024


def _rms(xf, g):
    ms = jnp.mean(xf * xf, axis=-1, keepdims=True)
    return xf * lax.rsqrt(ms + EPS) * g


def _gelu_tanh(x):
    c = 0.7978845608028654
    return 0.5 * x * (1.0 + jnp.tanh(c * (x + 0.044715 * (x * x * x))))


def _dot(a, b):
    return jnp.dot(a, b, preferred_element_type=F32)


def _const_spec(shape):
    nd = len(shape)
    return pl.BlockSpec(shape, lambda *_: (0,) * nd, pipeline_mode=pl.Buffered(1))


def _params(semantics, vmem_bytes):
    return pltpu.CompilerParams(dimension_semantics=semantics,
                                vmem_limit_bytes=min(int(vmem_bytes), V7X_SCOPED_VMEM_BYTES))


def _rope_table_kernel(invf_ref, cos_ref, sin_ref, cos_tr_ref, sin_tr_ref, *, ts):
    i = pl.program_id(0)
    pos = (lax.broadcasted_iota(jnp.int32, (ts, V7X_LANES), 0) + i * ts).astype(F32)
    lane = lax.broadcasted_iota(jnp.int32, (ts, V7X_LANES), 1)
    ang = pos * invf_ref[...]
    c = jnp.cos(ang)
    s = jnp.sin(ang)
    half = ROPE_DIM // 2
    cos_t = jnp.where(lane < ROPE_DIM, c, 0.0)
    sin_t = jnp.where(lane < half, -s, jnp.where(lane < ROPE_DIM, s, 0.0))
    cos_ref[...] = cos_t
    sin_ref[...] = sin_t
    cos_tr_ref[...] = cos_t.T[0:ROPE_DIM]
    sin_tr_ref[...] = sin_t.T[0:ROPE_DIM]


def _rope_tables(invf_row, seq, ts):
    return pl.pallas_call(
        functools.partial(_rope_table_kernel, ts=ts),
        grid=(seq // ts,),
        in_specs=[pl.BlockSpec((1, V7X_LANES), lambda i: (0, 0))],
        out_specs=[pl.BlockSpec((ts, V7X_LANES), lambda i: (i, 0)),
                   pl.BlockSpec((ts, V7X_LANES), lambda i: (i, 0)),
                   pl.BlockSpec((ROPE_DIM, ts), lambda i: (0, i)),
                   pl.BlockSpec((ROPE_DIM, ts), lambda i: (0, i))],
        out_shape=[jax.ShapeDtypeStruct((seq, V7X_LANES), F32)] * 2
                  + [jax.ShapeDtypeStruct((ROPE_DIM, seq), F32)] * 2,
        compiler_params=_params(("parallel",), 16 * ts * V7X_LANES * 4 + (4 << 20)),
        name="rope_tables",
    )(invf_row)


def _l0_mixer_kernel(x_ref, gmix_ref, win_ref, gv_ref, ws_ref, bs_ref, wpool_ref, psc_ref, wout_ref,
                     o_ref, h_ref, u_ref, v_ref, z_ref, cat_ref, *, tm, tiles_per_seq, d_a, d_g):
    i = pl.program_id(0)
    seq_tile = i % tiles_per_seq
    nb = tm // GMLP_BLOCK

    h_ref[...] = _rms(x_ref[...], gmix_ref[...]).astype(BF16)

    u_ref[...] = _gelu_tanh(_dot(h_ref[...], win_ref[:, 0:d_a]))
    v_ref[...] = _rms(_gelu_tanh(_dot(h_ref[...], win_ref[:, d_a:2 * d_a])), gv_ref[...]).astype(BF16)

    tt = lax.broadcasted_iota(jnp.int32, (GMLP_BLOCK, GMLP_BLOCK), 0)
    ss = lax.broadcasted_iota(jnp.int32, (GMLP_BLOCK, GMLP_BLOCK), 1)
    tri = (ss // CHUNK <= tt // CHUNK).astype(F32)
    hd_w = d_a // A_HEADS
    for hd in range(A_HEADS):
        c0 = hd * hd_w
        wm = (ws_ref[hd] * tri).astype(BF16)
        rhs = jnp.concatenate(
            [v_ref[n * GMLP_BLOCK:(n + 1) * GMLP_BLOCK, c0:c0 + hd_w] for n in range(nb)], axis=1)
        mix = _dot(wm, rhs)
        bias = bs_ref[hd]
        for n in range(nb):
            r0 = n * GMLP_BLOCK
            a = u_ref[r0:r0 + GMLP_BLOCK, c0:c0 + hd_w] * (mix[:, n * hd_w:(n + 1) * hd_w] + bias)
            cat_ref[r0:r0 + GMLP_BLOCK, c0:c0 + hd_w] = a.astype(BF16)

    @pl.when(seq_tile == 0)
    def _():
        z_ref[0:POOL_HALO, :] = jnp.zeros((POOL_HALO, z_ref.shape[1]), F32)

    z_ref[POOL_HALO:POOL_HALO + tm, :] = _dot(h_ref[...], win_ref[:, 2 * d_a:])
    pos1 = lax.broadcasted_iota(jnp.int32, (tm, d_g), 0) + (seq_tile * tm + 1)
    for g, win in enumerate(POOL_WINDOWS):
        zz = z_ref[:, g * d_g:(g + 1) * d_g]
        s = zz
        k = 1
        while k < win:
            s = s + pltpu.roll(s, k, axis=0)
            k *= 2
        cnt = jnp.minimum(pos1, win).astype(F32)
        d = (s[POOL_HALO:] / cnt - zz[POOL_HALO:]).astype(BF16)
        y = _dot(d, wpool_ref[g]) * psc_ref[:, g * d_g:(g + 1) * d_g]
        cat_ref[:, d_a + g * d_g:d_a + (g + 1) * d_g] = y.astype(BF16)
    z_ref[0:POOL_HALO, :] = z_ref[tm:tm + POOL_HALO, :]

    o_ref[...] = x_ref[...] + _dot(cat_ref[...], wout_ref[...])


def _l0_mixer(x2d, gmix, w_in, g_v, w_s, b_s_b, w_pool, pscale, w_out, *, seq, tm):
    t, d = x2d.shape
    n_in = w_in.shape[1]
    d_a = g_v.shape[1]
    d_b = pscale.shape[1]
    d_g = d_b // len(POOL_WINDOWS)
    assert n_in == 2 * d_a + d_b and seq % tm == 0 and tm % GMLP_BLOCK == 0
    vmem = (4 * tm * d * 4
            + w_in.size * 2 + w_out.size * 2 + w_pool.size * 2 + (w_s.size + b_s_b.size) * 4
            + tm * d * 2 + tm * d_a * 4 + tm * d_a * 2 + (tm + POOL_HALO) * d_b * 4 + tm * (d_a + d_b) * 2
            + 4 * tm * d_a * 4)
    return pl.pallas_call(
        functools.partial(_l0_mixer_kernel, tm=tm, tiles_per_seq=seq // tm, d_a=d_a, d_g=d_g),
        grid=(t // tm,),
        in_specs=[pl.BlockSpec((tm, d), lambda i: (i, 0)),
                  _const_spec(gmix.shape), _const_spec(w_in.shape), _const_spec(g_v.shape),
                  _const_spec(w_s.shape), _const_spec(b_s_b.shape), _const_spec(w_pool.shape),
                  _const_spec(pscale.shape), _const_spec(w_out.shape)],
        out_specs=pl.BlockSpec((tm, d), lambda i: (i, 0)),
        out_shape=jax.ShapeDtypeStruct((t, d), F32),
        scratch_shapes=[pltpu.VMEM((tm, d), BF16),
                        pltpu.VMEM((tm, d_a), F32),
                        pltpu.VMEM((tm, d_a), BF16),
                        pltpu.VMEM((tm + POOL_HALO, d_b), F32),
                        pltpu.VMEM((tm, d_a + d_b), BF16)],
        compiler_params=_params(("arbitrary",), vmem),
        name="l0_mixer",
    )(x2d, gmix, w_in, g_v, w_s, b_s_b, w_pool, pscale, w_out)


def _ffn_kernel(*refs, pre_proj, final_norm, n_cast):
    it = iter(refs)
    x_ref = next(it)
    attn_ref, wo_ref = (next(it), next(it)) if pre_proj else (None, None)
    g_ref, wg_ref, wu_ref, wd_ref = next(it), next(it), next(it), next(it)
    gfin_ref = next(it) if final_norm else None
    cast_in = [next(it) for _ in range(n_cast)]
    o_ref = next(it)
    cast_out = [next(it) for _ in range(n_cast)]
    h_ref = next(it)
    f = pl.program_id(1)

    @pl.when(f == 0)
    def _():
        x = x_ref[...]
        if pre_proj:
            x = x + _dot(attn_ref[...], wo_ref[...])
        h_ref[...] = _rms(x, g_ref[...]).astype(BF16)
        o_ref[...] = x

    h = h_ref[...]
    gate = _dot(h, wg_ref[...].astype(BF16))
    up = _dot(h, wu_ref[...].astype(BF16))
    act = (gate * jax.nn.sigmoid(gate) * up).astype(BF16)
    o_ref[...] += _dot(act, wd_ref[...].astype(BF16))

    for src_ref, dst_ref in zip(cast_in, cast_out):
        dst_ref[...] = src_ref[...].astype(BF16)

    if final_norm:
        @pl.when(f == pl.num_programs(1) - 1)
        def _():
            o_ref[...] = _rms(o_ref[...], gfin_ref[...])


def _ffn(x2d, g, w_gate, w_up, w_down, *, tm, tf, layer=None, attn=None, w_o=None, g_final=None,
         cast_next=(), cast_layer=None):
    t, d = x2d.shape
    dff = w_gate.shape[-1]
    assert t % tm == 0 and dff % tf == 0
    ni, nf = t // tm, dff // tf
    pre_proj = attn is not None
    final_norm = g_final is not None
    wbytes = w_gate.dtype.itemsize

    def wspec(w, lyr, blk, imap):
        if w.ndim == 2:
            return pl.BlockSpec(blk, imap)
        return pl.BlockSpec((None,) + blk, lambda i, f: (lyr,) + imap(i, f))

    args = [x2d]
    in_specs = [pl.BlockSpec((tm, d), lambda i, f: (i, 0))]
    vmem = 4 * tm * d * 4 + tm * d * 2 + 2 * 3 * d * tf * wbytes + 3 * d * tf * 2 + 4 * tm * tf * 4
    if pre_proj:
        args += [attn, w_o]
        in_specs += [pl.BlockSpec((tm, attn.shape[1]), lambda i, f: (i, 0)), _const_spec(w_o.shape)]
        vmem += 2 * tm * attn.shape[1] * 2 + w_o.size * 2 + tm * d * 4
    args += [g, w_gate, w_up, w_down]
    in_specs += [_const_spec(g.shape),
                 wspec(w_gate, layer, (d, tf), lambda i, f: (0, f)),
                 wspec(w_up, layer, (d, tf), lambda i, f: (0, f)),
                 wspec(w_down, layer, (tf, d), lambda i, f: (f, 0))]
    if final_norm:
        args.append(g_final)
        in_specs.append(_const_spec(g_final.shape))
    out_specs = [pl.BlockSpec((tm, d), lambda i, f: (i, 0))]
    out_shape = [jax.ShapeDtypeStruct((t, d), F32)]
    for w in cast_next:
        r, c = w.shape[-2:]
        if r % ni == 0 and c % nf == 0:
            blk, imap = (r // ni, c // nf), (lambda i, f: (i, f))
        else:
            assert r % nf == 0 and c % ni == 0
            blk, imap = (r // nf, c // ni), (lambda i, f: (f, i))
        args.append(w)
        in_specs.append(wspec(w, cast_layer, blk, imap))
        out_specs.append(pl.BlockSpec(blk, imap))
        out_shape.append(jax.ShapeDtypeStruct((r, c), BF16))
        vmem += 2 * blk[0] * blk[1] * 6
    res = pl.pallas_call(
        functools.partial(_ffn_kernel, pre_proj=pre_proj, final_norm=final_norm, n_cast=len(cast_next)),
        grid=(ni, nf),
        in_specs=in_specs,
        out_specs=out_specs,
        out_shape=out_shape,
        scratch_shapes=[pltpu.VMEM((tm, d), BF16)],
        compiler_params=_params(("parallel", "arbitrary"), vmem),
        name="ffn_final" if final_norm else "ffn",
    )(*args)
    return res if cast_next else res[0]


def _mla_proj_kernel(x_ref, gmix_ref, win_ref, gcq_ref, gckv_ref, wuq_tr_ref, wkn_ref, wv_tr_ref,
                     cos_ref, sin_ref, cos_tr_ref, sin_tr_ref,
                     q_tr_ref, kn_ref, kr_ref, v_tr_ref, h_ref, cq_tr_ref, ckv_ref, ckv_tr_ref, *, chunk):
    h_ref[...] = _rms(x_ref[...], gmix_ref[...]).astype(BF16)
    p = _dot(h_ref[...], win_ref[...])
    cq_tr_ref[...] = _rms(p[:, 0:Q_LORA], gcq_ref[...]).T.astype(BF16)
    ckv = _rms(p[:, Q_LORA:Q_LORA + KV_LORA], gckv_ref[...])
    ckv_ref[...] = ckv.astype(BF16)
    ckv_tr_ref[...] = ckv.T.astype(BF16)
    a = p[:, Q_LORA + KV_LORA:]
    kr_ref[...] = (a * cos_ref[...] + pltpu.roll(a, ROPE_DIM, axis=1) * sin_ref[...]).astype(BF16)

    rope_lo, rope_hi = NOPE_DIM, NOPE_DIM + ROPE_DIM
    for hd in range(C_HEADS):
        r0 = hd * QK_PAD
        qh = _dot(wuq_tr_ref[r0:r0 + QK_PAD, :], cq_tr_ref[...])
        q_tr_ref[r0:r0 + rope_lo, :] = qh[0:rope_lo].astype(BF16)
        roped = qh[rope_lo:rope_hi] * cos_tr_ref[...] + qh[rope_hi:] * sin_tr_ref[...]
        q_tr_ref[r0 + rope_lo:r0 + rope_hi, :] = roped.astype(BF16)
        q_tr_ref[r0 + rope_hi:r0 + QK_PAD, :] = jnp.zeros((QK_PAD - rope_hi, qh.shape[1]), BF16)

    for c in range(kn_ref.shape[1] // chunk):
        sl = slice(c * chunk, (c + 1) * chunk)
        kn_ref[:, sl] = _dot(ckv_ref[...], wkn_ref[:, sl]).astype(BF16)
        v_tr_ref[sl, :] = _dot(wv_tr_ref[sl, :], ckv_tr_ref[...]).astype(BF16)


def _mla_proj(x2d, gmix, w_in_ext, g_cq_s, g_ckv, w_uq_tr, w_kn, w_v_tr, cos_t, sin_t, cos_tr, sin_tr,
              *, seq, tm):
    t, d = x2d.shape
    n_q = w_uq_tr.shape[0]
    n_kv = w_kn.shape[1]
    tiles_per_seq = seq // tm
    assert seq % tm == 0 and w_in_ext.shape[1] == Q_LORA + KV_LORA + 2 * ROPE_DIM
    vmem = (2 * tm * d * 4 + (w_in_ext.size + w_uq_tr.size + w_kn.size + w_v_tr.size) * 2
            + 2 * 2 * tm * (V7X_LANES + ROPE_DIM) * 4
            + 2 * tm * (n_q + 2 * n_kv + V7X_LANES) * 2
            + tm * d * 2 + tm * (Q_LORA + 2 * KV_LORA) * 2
            + 4 * tm * w_in_ext.shape[1] * 4)
    row = lambda i: (i, 0)
    col = lambda i: (0, i)
    return pl.pallas_call(
        functools.partial(_mla_proj_kernel, chunk=512),
        grid=(t // tm,),
        in_specs=[pl.BlockSpec((tm, d), row),
                  _const_spec(gmix.shape), _const_spec(w_in_ext.shape), _const_spec(g_cq_s.shape),
                  _const_spec(g_ckv.shape), _const_spec(w_uq_tr.shape), _const_spec(w_kn.shape),
                  _const_spec(w_v_tr.shape),
                  pl.BlockSpec((tm, V7X_LANES), lambda i: (i % tiles_per_seq, 0)),
                  pl.BlockSpec((tm, V7X_LANES), lambda i: (i % tiles_per_seq, 0)),
                  pl.BlockSpec((ROPE_DIM, tm), lambda i: (0, i % tiles_per_seq)),
                  pl.BlockSpec((ROPE_DIM, tm), lambda i: (0, i % tiles_per_seq))],
        out_specs=[pl.BlockSpec((n_q, tm), col), pl.BlockSpec((tm, n_kv), row),
                   pl.BlockSpec((tm, V7X_LANES), row), pl.BlockSpec((n_kv, tm), col)],
        out_shape=[jax.ShapeDtypeStruct((n_q, t), BF16), jax.ShapeDtypeStruct((t, n_kv), BF16),
                   jax.ShapeDtypeStruct((t, V7X_LANES), BF16), jax.ShapeDtypeStruct((n_kv, t), BF16)],
        scratch_shapes=[pltpu.VMEM((tm, d), BF16), pltpu.VMEM((Q_LORA, tm), BF16),
                        pltpu.VMEM((tm, KV_LORA), BF16), pltpu.VMEM((KV_LORA, tm), BF16)],
        compiler_params=_params(("parallel",), vmem),
        name="mla_proj",
    )(x2d, gmix, w_in_ext, g_cq_s, g_ckv, w_uq_tr, w_kn, w_v_tr, cos_t, sin_t, cos_tr, sin_tr)


def _attn_kernel(*refs, tk, tq, skew, n_cast):
    q_tr_ref, kn_ref, kr_ref, v_tr_ref = refs[0:4]
    cast_in = refs[4:4 + n_cast]
    o_ref = refs[4 + n_cast]
    cast_out = refs[5 + n_cast:5 + 2 * n_cast]
    kcat_ref, m_ref, l_ref, acc_ref = refs[5 + 2 * n_cast:]
    for src_ref, dst_ref in zip(cast_in, cast_out):
        dst_ref[...] = src_ref[...].astype(BF16)
    seq = kcat_ref.shape[0]
    kcat_ref[:, 0:NOPE_DIM] = kn_ref[...]
    kcat_ref[:, NOPE_DIM:] = kr_ref[...]

    units = []
    for j in range(seq // tk):
        for i in range(seq // tq):
            k_lo, q_lo = j * tk, i * tq
            k_hi = min(k_lo + tk, q_lo + tq)
            if k_hi > k_lo:
                units.append((j == 0, k_lo, k_hi, q_lo))

    def scores(unit):
        _, k_lo, k_hi, q_lo = unit
        s = _dot(kcat_ref[k_lo:k_hi, :], q_tr_ref[:, q_lo:q_lo + tq])
        if k_hi > q_lo + CHUNK:
            kc = (lax.broadcasted_iota(jnp.int32, s.shape, 0) + k_lo) // CHUNK
            qc = (lax.broadcasted_iota(jnp.int32, s.shape, 1) + q_lo) // CHUNK
            s = jnp.where(kc <= qc, s, -jnp.inf)
        return s

    pending = [scores(u) for u in units[:skew]]
    for n, (first, k_lo, k_hi, q_lo) in enumerate(units):
        if n + skew < len(units):
            pending.append(scores(units[n + skew]))
        s = pending.pop(0)
        qs = slice(q_lo, q_lo + tq)
        v_j = v_tr_ref[:, k_lo:k_hi]
        m_blk = jnp.max(s, axis=0, keepdims=True)
        if first:
            m_new = m_blk
            p = jnp.exp2(s - m_new)
            l_ref[:, qs] = jnp.sum(p, axis=0, keepdims=True)
            acc_ref[:, qs] = _dot(v_j, p.astype(BF16))
        else:
            m_prev = m_ref[:, qs]
            m_new = jnp.maximum(m_prev, m_blk)
            alpha = jnp.exp2(m_prev - m_new)
            p = jnp.exp2(s - m_new)
            l_ref[:, qs] = alpha * l_ref[:, qs] + jnp.sum(p, axis=0, keepdims=True)
            acc_ref[:, qs] = alpha * acc_ref[:, qs] + _dot(v_j, p.astype(BF16))
        m_ref[:, qs] = m_new

    for i in range(seq // tq):
        qs = slice(i * tq, (i + 1) * tq)
        o_ref[qs, :] = (acc_ref[:, qs] / l_ref[:, qs]).T.astype(o_ref.dtype)


def _attention(q_tr, kn, kr, v_tr, *, batch, seq, tk, tq, skew, cast_next=(), cast_layer=None):
    t = kn.shape[0]
    steps = batch * C_HEADS
    assert seq % tk == 0 and seq % tq == 0 and tk % CHUNK == 0 and tq % CHUNK == 0
    assert V_DIM == V7X_LANES and NOPE_DIM == V7X_LANES
    vmem = (2 * (QK_PAD + V_DIM) * seq * 2 + 2 * 2 * seq * V7X_LANES * 2 + 2 * seq * V_DIM * 2
            + seq * QK_PAD * 2 + 2 * 8 * seq * 4 + V_DIM * seq * 4 + 24 * tk * tq * 4)
    in_specs = [pl.BlockSpec((QK_PAD, seq), lambda b, h: (h, b)),
                pl.BlockSpec((seq, NOPE_DIM), lambda b, h: (b, h)),
                pl.BlockSpec((seq, V7X_LANES), lambda b, h: (b, 0)),
                pl.BlockSpec((V_DIM, seq), lambda b, h: (h, b))]
    out_specs = [pl.BlockSpec((seq, V_DIM), lambda b, h: (b, h))]
    out_shape = [jax.ShapeDtypeStruct((t, C_HEADS * V_DIM), BF16)]
    for w in cast_next:
        _, r, c = w.shape
        assert r % (steps * 16) == 0
        in_specs.append(pl.BlockSpec((None, r // steps, c), lambda b, h: (cast_layer, b * C_HEADS + h, 0)))
        out_specs.append(pl.BlockSpec((r // steps, c), lambda b, h: (b * C_HEADS + h, 0)))
        out_shape.append(jax.ShapeDtypeStruct((r, c), BF16))
        vmem += 2 * (r // steps) * c * 6
    res = pl.pallas_call(
        functools.partial(_attn_kernel, tk=tk, tq=tq, skew=skew, n_cast=len(cast_next)),
        grid=(batch, C_HEADS),
        in_specs=in_specs,
        out_specs=out_specs,
        out_shape=out_shape,
        scratch_shapes=[pltpu.VMEM((seq, QK_PAD), BF16),
                        pltpu.VMEM((1, seq), F32), pltpu.VMEM((1, seq), F32),
                        pltpu.VMEM((V_DIM, seq), F32)],
        compiler_params=_params(("parallel", "parallel"), vmem),
        name="attention",
    )(q_tr, kn, kr, v_tr, *cast_next)
    return res if cast_next else res[0]


def _proj_residual_kernel(x_ref, a_ref, w_ref, o_ref):
    o_ref[...] = x_ref[...] + _dot(a_ref[...], w_ref[...])


def _proj_residual(x2d, a, w, *, tm):
    t, d = x2d.shape
    k = a.shape[1]
    vmem = 4 * tm * d * 4 + 2 * tm * k * 2 + w.size * 2 + 2 * tm * d * 4
    return pl.pallas_call(
        _proj_residual_kernel,
        grid=(t // tm,),
        in_specs=[pl.BlockSpec((tm, d), lambda i: (i, 0)), pl.BlockSpec((tm, k), lambda i: (i, 0)),
                  _const_spec(w.shape)],
        out_specs=pl.BlockSpec((tm, d), lambda i: (i, 0)),
        out_shape=jax.ShapeDtypeStruct((t, d), F32),
        compiler_params=_params(("parallel",), vmem),
        name="attn_out_proj",
    )(x2d, a, w)


def _swap_halves(w):
    half = w.shape[-1] // 2
    return jnp.concatenate([w[..., half:], w[..., :half]], axis=-1)


def kernel(x, g_mix, g_ffn, g_final, w_in_ab, g_v, w_s, b_s, w_pool, pool_scale, w_out_ab, w_in_c, g_cq, g_ckv, w_uq, w_ukv, w_out_c, w_gate, w_up, w_down):
    batch, seq, d = x.shape
    x2d = x.reshape(batch * seq, d)
    row = lambda a: a.reshape(1, -1)

    b_s_b = jnp.broadcast_to(b_s[0][:, :, None], (A_HEADS, GMLP_BLOCK, GMLP_BLOCK))
    x1 = _l0_mixer(x2d, row(g_mix[0]), w_in_ab[0].astype(BF16), row(g_v[0]), w_s[0], b_s_b,
                   w_pool[0].astype(BF16), row(pool_scale[0]), w_out_ab[0].astype(BF16),
                   seq=seq, tm=512)
    x2 = _ffn(x1, row(g_ffn[0]), w_gate, w_up, w_down, layer=0, tm=1024, tf=256)

    inv_freq = ROPE_THETA ** (-jnp.arange(0, ROPE_DIM, 2, dtype=F32) / ROPE_DIM)
    cos_t, sin_t, cos_tr, sin_tr = _rope_tables(
        jnp.tile(inv_freq, V7X_LANES // inv_freq.shape[0]).reshape(1, -1), seq, 512)
    wc = w_in_c[0]
    w_kr = wc[:, Q_LORA + KV_LORA:]
    w_in_ext = jnp.concatenate([wc, _swap_halves(w_kr)], axis=1).astype(BF16)
    wq = w_uq[0].reshape(Q_LORA, C_HEADS, NOPE_DIM + ROPE_DIM)
    w_uq_ext = jnp.concatenate([wq, _swap_halves(wq[..., NOPE_DIM:])], axis=-1)
    w_uq_tr = w_uq_ext.reshape(Q_LORA, C_HEADS * QK_PAD).T.astype(BF16)
    wkv = w_ukv[0].reshape(KV_LORA, C_HEADS, NOPE_DIM + V_DIM)
    w_kn = wkv[..., :NOPE_DIM].reshape(KV_LORA, -1).astype(BF16)
    w_v_tr = wkv[..., NOPE_DIM:].reshape(KV_LORA, -1).T.astype(BF16)
    score_scale = (NOPE_DIM + ROPE_DIM) ** -0.5 * 1.4426950408889634
    q_tr, kn, kr, v_tr = _mla_proj(x2, row(g_mix[1]), w_in_ext, row(g_cq[0]) * score_scale,
                                   row(g_ckv[0]), w_uq_tr, w_kn, w_v_tr, cos_t, sin_t, cos_tr, sin_tr,
                                   seq=seq, tm=512)
    o, w_gate1, w_up1, w_down1 = _attention(q_tr, kn, kr, v_tr, batch=batch, seq=seq, tk=512, tq=256, skew=3,
                                            cast_next=(w_gate, w_up, w_down), cast_layer=1)
    x3 = _proj_residual(x2, o, w_out_c[0].astype(BF16), tm=512)
    out = _ffn(x3, row(g_ffn[1]), w_gate1, w_up1, w_down1, tm=1024, tf=512, g_final=row(g_final))
    return out.reshape(batch, seq, d)
```

```python
import functools

import jax
import jax.numpy as jnp
from jax import lax
from jax.experimental import pallas as pl
from jax.experimental.pallas import tpu as pltpu

F32 = jnp.float32
BF16 = jnp.bfloat16

CHUNK = 64
EPS = 1e-6
GMLP_BLOCK = 128
A_HEADS = 8
POOL_WINDOWS = (2, 4, 8, 16)
POOL_HALO = 16
C_HEADS = 16
Q_LORA = 512
KV_LORA = 512
NOPE_DIM = 128
ROPE_DIM = 64
V_DIM = 128
ROPE_THETA = 10000.0
QK_PAD = 256
ONES_ROWS = 16

V7X_LANES = 128
V7X_SCOPED_VMEM_BYTES = 60000 * 1024


def _rms(xf, g):
    ms = jnp.mean(xf * xf, axis=-1, keepdims=True)
    return xf * lax.rsqrt(ms + EPS) * g


def _gelu_tanh(x):
    c = 0.7978845608028654
    return 0.5 * x * (1.0 + jnp.tanh(c * (x + 0.044715 * (x * x * x))))


def _dot(a, b):
    return jnp.dot(a, b, preferred_element_type=F32)


def _const_spec(shape):
    nd = len(shape)
    return pl.BlockSpec(shape, lambda *_: (0,) * nd, pipeline_mode=pl.Buffered(1))


def _params(semantics, vmem_bytes):
    return pltpu.CompilerParams(dimension_semantics=semantics,
                                vmem_limit_bytes=min(int(vmem_bytes), V7X_SCOPED_VMEM_BYTES))


def _rope_table_kernel(invf_ref, cos_ref, sin_ref, cos_tr_ref, sin_tr_ref, *, ts):
    i = pl.program_id(0)
    pos = (lax.broadcasted_iota(jnp.int32, (ts, V7X_LANES), 0) + i * ts).astype(F32)
    lane = lax.broadcasted_iota(jnp.int32, (ts, V7X_LANES), 1)
    ang = pos * invf_ref[...]
    c = jnp.cos(ang)
    s = jnp.sin(ang)
    half = ROPE_DIM // 2
    cos_t = jnp.where(lane < ROPE_DIM, c, 0.0)
    sin_t = jnp.where(lane < half, -s, jnp.where(lane < ROPE_DIM, s, 0.0))
    cos_ref[...] = cos_t
    sin_ref[...] = sin_t
    cos_tr_ref[...] = cos_t.T[0:ROPE_DIM]
    sin_tr_ref[...] = sin_t.T[0:ROPE_DIM]


def _rope_tables(invf_row, seq, ts):
    return pl.pallas_call(
        functools.partial(_rope_table_kernel, ts=ts),
        grid=(seq // ts,),
        in_specs=[pl.BlockSpec((1, V7X_LANES), lambda i: (0, 0))],
        out_specs=[pl.BlockSpec((ts, V7X_LANES), lambda i: (i, 0)),
                   pl.BlockSpec((ts, V7X_LANES), lambda i: (i, 0)),
                   pl.BlockSpec((ROPE_DIM, ts), lambda i: (0, i)),
                   pl.BlockSpec((ROPE_DIM, ts), lambda i: (0, i))],
        out_shape=[jax.ShapeDtypeStruct((seq, V7X_LANES), F32)] * 2
                  + [jax.ShapeDtypeStruct((ROPE_DIM, seq), F32)] * 2,
        compiler_params=_params(("parallel",), 16 * ts * V7X_LANES * 4 + (4 << 20)),
        name="rope_tables",
    )(invf_row)


def _l0_mixer_kernel(x_ref, gmix_ref, win_ref, gv_ref, ws_ref, bs_ref, wpool_ref, psc_ref, wout_ref,
                     o_ref, h_ref, u_ref, v_ref, z_ref, cat_ref, *, tm, tiles_per_seq, d_a, d_g):
    i = pl.program_id(0)
    seq_tile = i % tiles_per_seq
    nb = tm // GMLP_BLOCK

    h_ref[...] = _rms(x_ref[...], gmix_ref[...]).astype(BF16)

    u_ref[...] = _gelu_tanh(_dot(h_ref[...], win_ref[:, 0:d_a]))
    v_ref[...] = _rms(_gelu_tanh(_dot(h_ref[...], win_ref[:, d_a:2 * d_a])), gv_ref[...]).astype(BF16)

    tt = lax.broadcasted_iota(jnp.int32, (GMLP_BLOCK, GMLP_BLOCK), 0)
    ss = lax.broadcasted_iota(jnp.int32, (GMLP_BLOCK, GMLP_BLOCK), 1)
    tri = (ss // CHUNK <= tt // CHUNK).astype(F32)
    hd_w = d_a // A_HEADS
    for hd in range(A_HEADS):
        c0 = hd * hd_w
        wm = (ws_ref[hd] * tri).astype(BF16)
        rhs = jnp.concatenate(
            [v_ref[n * GMLP_BLOCK:(n + 1) * GMLP_BLOCK, c0:c0 + hd_w] for n in range(nb)], axis=1)
        mix = _dot(wm, rhs)
        bias = bs_ref[hd]
        for n in range(nb):
            r0 = n * GMLP_BLOCK
            a = u_ref[r0:r0 + GMLP_BLOCK, c0:c0 + hd_w] * (mix[:, n * hd_w:(n + 1) * hd_w] + bias)
            cat_ref[r0:r0 + GMLP_BLOCK, c0:c0 + hd_w] = a.astype(BF16)

    @pl.when(seq_tile == 0)
    def _():
        z_ref[0:POOL_HALO, :] = jnp.zeros((POOL_HALO, z_ref.shape[1]), F32)

    z_ref[POOL_HALO:POOL_HALO + tm, :] = _dot(h_ref[...], win_ref[:, 2 * d_a:])
    pos1 = lax.broadcasted_iota(jnp.int32, (tm, d_g), 0) + (seq_tile * tm + 1)
    for g, win in enumerate(POOL_WINDOWS):
        zz = z_ref[:, g * d_g:(g + 1) * d_g]
        s = zz
        k = 1
        while k < win:
            s = s + pltpu.roll(s, k, axis=0)
            k *= 2
        cnt = jnp.minimum(pos1, win).astype(F32)
        d = (s[POOL_HALO:] / cnt - zz[POOL_HALO:]).astype(BF16)
        y = _dot(d, wpool_ref[g]) * psc_ref[:, g * d_g:(g + 1) * d_g]
        cat_ref[:, d_a + g * d_g:d_a + (g + 1) * d_g] = y.astype(BF16)
    z_ref[0:POOL_HALO, :] = z_ref[tm:tm + POOL_HALO, :]

    o_ref[...] = x_ref[...] + _dot(cat_ref[...], wout_ref[...])


def _l0_mixer(x2d, gmix, w_in, g_v, w_s, b_s_b, w_pool, pscale, w_out, *, seq, tm):
    t, d = x2d.shape
    n_in = w_in.shape[1]
    d_a = g_v.shape[1]
    d_b = pscale.shape[1]
    d_g = d_b // len(POOL_WINDOWS)
    assert n_in == 2 * d_a + d_b and seq % tm == 0 and tm % GMLP_BLOCK == 0
    vmem = (4 * tm * d * 4
            + w_in.size * 2 + w_out.size * 2 + w_pool.size * 2 + (w_s.size + b_s_b.size) * 4
            + tm * d * 2 + tm * d_a * 4 + tm * d_a * 2 + (tm + POOL_HALO) * d_b * 4 + tm * (d_a + d_b) * 2
            + 4 * tm * d_a * 4)
    return pl.pallas_call(
        functools.partial(_l0_mixer_kernel, tm=tm, tiles_per_seq=seq // tm, d_a=d_a, d_g=d_g),
        grid=(t // tm,),
        in_specs=[pl.BlockSpec((tm, d), lambda i: (i, 0)),
                  _const_spec(gmix.shape), _const_spec(w_in.shape), _const_spec(g_v.shape),
                  _const_spec(w_s.shape), _const_spec(b_s_b.shape), _const_spec(w_pool.shape),
                  _const_spec(pscale.shape), _const_spec(w_out.shape)],
        out_specs=pl.BlockSpec((tm, d), lambda i: (i, 0)),
        out_shape=jax.ShapeDtypeStruct((t, d), F32),
        scratch_shapes=[pltpu.VMEM((tm, d), BF16),
                        pltpu.VMEM((tm, d_a), F32),
                        pltpu.VMEM((tm, d_a), BF16),
                        pltpu.VMEM((tm + POOL_HALO, d_b), F32),
                        pltpu.VMEM((tm, d_a + d_b), BF16)],
        compiler_params=_params(("arbitrary",), vmem),
        name="l0_mixer",
    )(x2d, gmix, w_in, g_v, w_s, b_s_b, w_pool, pscale, w_out)


def _ffn_kernel(*refs, pre_proj, final_norm, n_cast):
    it = iter(refs)
    x_ref = next(it)
    attn_ref, wo_ref = (next(it), next(it)) if pre_proj else (None, None)
    g_ref, wg_ref, wu_ref, wd_ref = next(it), next(it), next(it), next(it)
    gfin_ref = next(it) if final_norm else None
    cast_in = [next(it) for _ in range(n_cast)]
    o_ref = next(it)
    cast_out = [next(it) for _ in range(n_cast)]
    h_ref = next(it)
    f = pl.program_id(1)

    @pl.when(f == 0)
    def _():
        x = x_ref[...]
        if pre_proj:
            x = x + _dot(attn_ref[...], wo_ref[...])
        h_ref[...] = _rms(x, g_ref[...]).astype(BF16)
        o_ref[...] = x

    h = h_ref[...]
    gate = _dot(h, wg_ref[...].astype(BF16))
    up = _dot(h, wu_ref[...].astype(BF16))
    act = (gate * jax.nn.sigmoid(gate) * up).astype(BF16)
    o_ref[...] += _dot(act, wd_ref[...].astype(BF16))

    for src_ref, dst_ref in zip(cast_in, cast_out):
        dst_ref[...] = src_ref[...].astype(BF16)

    if final_norm:
        @pl.when(f == pl.num_programs(1) - 1)
        def _():
            o_ref[...] = _rms(o_ref[...], gfin_ref[...])


def _ffn(x2d, g, w_gate, w_up, w_down, *, tm, tf, layer=None, attn=None, w_o=None, g_final=None,
         cast_next=(), cast_layer=None):
    t, d = x2d.shape
    dff = w_gate.shape[-1]
    assert t % tm == 0 and dff % tf == 0
    ni, nf = t // tm, dff // tf
    pre_proj = attn is not None
    final_norm = g_final is not None
    wbytes = w_gate.dtype.itemsize

    def wspec(w, lyr, blk, imap):
        if w.ndim == 2:
            return pl.BlockSpec(blk, imap)
        return pl.BlockSpec((None,) + blk, lambda i, f: (lyr,) + imap(i, f))

    args = [x2d]
    in_specs = [pl.BlockSpec((tm, d), lambda i, f: (i, 0))]
    vmem = 4 * tm * d * 4 + tm * d * 2 + 2 * 3 * d * tf * wbytes + 3 * d * tf * 2 + 4 * tm * tf * 4
    if pre_proj:
        args += [attn, w_o]
        in_specs += [pl.BlockSpec((tm, attn.shape[1]), lambda i, f: (i, 0)), _const_spec(w_o.shape)]
        vmem += 2 * tm * attn.shape[1] * 2 + w_o.size * 2 + tm * d * 4
    args += [g, w_gate, w_up, w_down]
    in_specs += [_const_spec(g.shape),
                 wspec(w_gate, layer, (d, tf), lambda i, f: (0, f)),
                 wspec(w_up, layer, (d, tf), lambda i, f: (0, f)),
                 wspec(w_down, layer, (tf, d), lambda i, f: (f, 0))]
    if final_norm:
        args.append(g_final)
        in_specs.append(_const_spec(g_final.shape))
    out_specs = [pl.BlockSpec((tm, d), lambda i, f: (i, 0))]
    out_shape = [jax.ShapeDtypeStruct((t, d), F32)]
    for w in cast_next:
        r, c = w.shape[-2:]
        if r % ni == 0 and c % nf == 0:
            blk, imap = (r // ni, c // nf), (lambda i, f: (i, f))
        else:
            assert r % nf == 0 and c % ni == 0
            blk, imap = (r // nf, c // ni), (lambda i, f: (f, i))
        args.append(w)
        in_specs.append(wspec(w, cast_layer, blk, imap))
        out_specs.append(pl.BlockSpec(blk, imap))
        out_shape.append(jax.ShapeDtypeStruct((r, c), BF16))
        vmem += 2 * blk[0] * blk[1] * 6
    res = pl.pallas_call(
        functools.partial(_ffn_kernel, pre_proj=pre_proj, final_norm=final_norm, n_cast=len(cast_next)),
        grid=(ni, nf),
        in_specs=in_specs,
        out_specs=out_specs,
        out_shape=out_shape,
        scratch_shapes=[pltpu.VMEM((tm, d), BF16)],
        compiler_params=_params(("parallel", "arbitrary"), vmem),
        name="ffn_final" if final_norm else "ffn",
    )(*args)
    return res if cast_next else res[0]


def _mla_proj_kernel(x_ref, gmix_ref, win_ref, gcq_ref, gckv_ref, wuq_tr_ref, wkn_ref, wv_tr_ref,
                     cos_ref, sin_ref, cos_tr_ref, sin_tr_ref,
                     q_tr_ref, kn_ref, kr_ref, v_tr_ref, h_ref, cq_tr_ref, ckv_ref, ckv_tr_ref, *, chunk):
    h_ref[...] = _rms(x_ref[...], gmix_ref[...]).astype(BF16)
    p = _dot(h_ref[...], win_ref[...])
    cq_tr_ref[...] = _rms(p[:, 0:Q_LORA], gcq_ref[...]).T.astype(BF16)
    ckv = _rms(p[:, Q_LORA:Q_LORA + KV_LORA], gckv_ref[...])
    ckv_ref[...] = ckv.astype(BF16)
    ckv_tr_ref[...] = ckv.T.astype(BF16)
    a = p[:, Q_LORA + KV_LORA:]
    kr_ref[...] = (a * cos_ref[...] + pltpu.roll(a, ROPE_DIM, axis=1) * sin_ref[...]).astype(BF16)

    rope_lo, rope_hi = NOPE_DIM, NOPE_DIM + ROPE_DIM
    for hd in range(C_HEADS):
        r0 = hd * QK_PAD
        qh = _dot(wuq_tr_ref[r0:r0 + QK_PAD, :], cq_tr_ref[...])
        q_tr_ref[r0:r0 + rope_lo, :] = qh[0:rope_lo].astype(BF16)
        roped = qh[rope_lo:rope_hi] * cos_tr_ref[...] + qh[rope_hi:] * sin_tr_ref[...]
        q_tr_ref[r0 + rope_lo:r0 + rope_hi, :] = roped.astype(BF16)
        q_tr_ref[r0 + rope_hi:r0 + QK_PAD, :] = jnp.zeros((QK_PAD - rope_hi, qh.shape[1]), BF16)

    for c in range(kn_ref.shape[1] // chunk):
        sl = slice(c * chunk, (c + 1) * chunk)
        kn_ref[:, sl] = _dot(ckv_ref[...], wkn_ref[:, sl]).astype(BF16)
        v_tr_ref[sl, :] = _dot(wv_tr_ref[sl, :], ckv_tr_ref[...]).astype(BF16)


def _mla_proj(x2d, gmix, w_in_ext, g_cq_s, g_ckv, w_uq_tr, w_kn, w_v_tr, cos_t, sin_t, cos_tr, sin_tr,
              *, seq, tm):
    t, d = x2d.shape
    n_q = w_uq_tr.shape[0]
    n_kv = w_kn.shape[1]
    tiles_per_seq = seq // tm
    assert seq % tm == 0 and w_in_ext.shape[1] == Q_LORA + KV_LORA + 2 * ROPE_DIM
    vmem = (2 * tm * d * 4 + (w_in_ext.size + w_uq_tr.size + w_kn.size + w_v_tr.size) * 2
            + 2 * 2 * tm * (V7X_LANES + ROPE_DIM) * 4
            + 2 * tm * (n_q + 2 * n_kv + V7X_LANES) * 2
            + tm * d * 2 + tm * (Q_LORA + 2 * KV_LORA) * 2
            + 4 * tm * w_in_ext.shape[1] * 4)
    row = lambda i: (i, 0)
    col = lambda i: (0, i)
    return pl.pallas_call(
        functools.partial(_mla_proj_kernel, chunk=512),
        grid=(t // tm,),
        in_specs=[pl.BlockSpec((tm, d), row),
                  _const_spec(gmix.shape), _const_spec(w_in_ext.shape), _const_spec(g_cq_s.shape),
                  _const_spec(g_ckv.shape), _const_spec(w_uq_tr.shape), _const_spec(w_kn.shape),
                  _const_spec(w_v_tr.shape),
                  pl.BlockSpec((tm, V7X_LANES), lambda i: (i % tiles_per_seq, 0)),
                  pl.BlockSpec((tm, V7X_LANES), lambda i: (i % tiles_per_seq, 0)),
                  pl.BlockSpec((ROPE_DIM, tm), lambda i: (0, i % tiles_per_seq)),
                  pl.BlockSpec((ROPE_DIM, tm), lambda i: (0, i % tiles_per_seq))],
        out_specs=[pl.BlockSpec((n_q, tm), col), pl.BlockSpec((tm, n_kv), row),
                   pl.BlockSpec((tm, V7X_LANES), row), pl.BlockSpec((n_kv, tm), col)],
        out_shape=[jax.ShapeDtypeStruct((n_q, t), BF16), jax.ShapeDtypeStruct((t, n_kv), BF16),
                   jax.ShapeDtypeStruct((t, V7X_LANES), BF16), jax.ShapeDtypeStruct((n_kv, t), BF16)],
        scratch_shapes=[pltpu.VMEM((tm, d), BF16), pltpu.VMEM((Q_LORA, tm), BF16),
                        pltpu.VMEM((tm, KV_LORA), BF16), pltpu.VMEM((KV_LORA, tm), BF16)],
        compiler_params=_params(("parallel",), vmem),
        name="mla_proj",
    )(x2d, gmix, w_in_ext, g_cq_s, g_ckv, w_uq_tr, w_kn, w_v_tr, cos_t, sin_t, cos_tr, sin_tr)


def _attn_kernel(*refs, tk, tq, skew, n_cast):
    q_tr_ref, kn_ref, kr_ref, v_tr_ref = refs[0:4]
    cast_in = refs[4:4 + n_cast]
    o_ref = refs[4 + n_cast]
    cast_out = refs[5 + n_cast:5 + 2 * n_cast]
    kcat_ref, vaug_ref, m_ref, acc_ref = refs[5 + 2 * n_cast:]
    for src_ref, dst_ref in zip(cast_in, cast_out):
        dst_ref[...] = src_ref[...].astype(BF16)
    seq = kcat_ref.shape[0]
    kcat_ref[:, 0:NOPE_DIM] = kn_ref[...]
    kcat_ref[:, NOPE_DIM:] = kr_ref[...]
    vaug_ref[0:V_DIM, :] = v_tr_ref[...]
    vaug_ref[V_DIM:, :] = jnp.ones((vaug_ref.shape[0] - V_DIM, seq), BF16)

    units = []
    for j in range(seq // tk):
        for i in range(seq // tq):
            k_lo, q_lo = j * tk, i * tq
            k_hi = min(k_lo + tk, q_lo + tq)
            if k_hi > k_lo:
                units.append((j == 0, k_lo, k_hi, q_lo))

    def scores(unit):
        _, k_lo, k_hi, q_lo = unit
        s = _dot(kcat_ref[k_lo:k_hi, :], q_tr_ref[:, q_lo:q_lo + tq])
        if k_hi > q_lo + CHUNK:
            kc = (lax.broadcasted_iota(jnp.int32, s.shape, 0) + k_lo) // CHUNK
            qc = (lax.broadcasted_iota(jnp.int32, s.shape, 1) + q_lo) // CHUNK
            s = jnp.where(kc <= qc, s, -jnp.inf)
        return s

    pending = [scores(u) for u in units[:skew]]
    for n, (first, k_lo, k_hi, q_lo) in enumerate(units):
        if n + skew < len(units):
            pending.append(scores(units[n + skew]))
        s = pending.pop(0)
        qs = slice(q_lo, q_lo + tq)
        v_j = vaug_ref[:, k_lo:k_hi]
        m_blk = jnp.max(s, axis=0, keepdims=True)
        if first:
            m_new = m_blk
            acc_ref[:, qs] = _dot(v_j, jnp.exp2(s - m_new).astype(BF16))
        else:
            m_prev = m_ref[:, qs]
            m_new = jnp.maximum(m_prev, m_blk)
            alpha = jnp.exp2(m_prev - m_new)
            acc_ref[:, qs] = alpha * acc_ref[:, qs] + _dot(v_j, jnp.exp2(s - m_new).astype(BF16))
        m_ref[:, qs] = m_new

    for i in range(seq // tq):
        qs = slice(i * tq, (i + 1) * tq)
        o_ref[qs, :] = (acc_ref[0:V_DIM, qs] / acc_ref[V_DIM:V_DIM + 1, qs]).T.astype(o_ref.dtype)


def _attention(q_tr, kn, kr, v_tr, *, batch, seq, tk, tq, skew, cast_next=(), cast_layer=None):
    t = kn.shape[0]
    steps = batch * C_HEADS
    assert seq % tk == 0 and seq % tq == 0 and tk % CHUNK == 0 and tq % CHUNK == 0
    assert V_DIM == V7X_LANES and NOPE_DIM == V7X_LANES
    vmem = (2 * (QK_PAD + V_DIM) * seq * 2 + 2 * 2 * seq * V7X_LANES * 2 + 2 * seq * V_DIM * 2
            + seq * QK_PAD * 2 + 2 * 8 * seq * 4 + V_DIM * seq * 4 + 24 * tk * tq * 4)
    in_specs = [pl.BlockSpec((QK_PAD, seq), lambda b, h: (h, b)),
                pl.BlockSpec((seq, NOPE_DIM), lambda b, h: (b, h)),
                pl.BlockSpec((seq, V7X_LANES), lambda b, h: (b, 0)),
                pl.BlockSpec((V_DIM, seq), lambda b, h: (h, b))]
    out_specs = [pl.BlockSpec((seq, V_DIM), lambda b, h: (b, h))]
    out_shape = [jax.ShapeDtypeStruct((t, C_HEADS * V_DIM), BF16)]
    for w in cast_next:
        _, r, c = w.shape
        assert r % (steps * 16) == 0
        in_specs.append(pl.BlockSpec((None, r // steps, c), lambda b, h: (cast_layer, b * C_HEADS + h, 0)))
        out_specs.append(pl.BlockSpec((r // steps, c), lambda b, h: (b * C_HEADS + h, 0)))
        out_shape.append(jax.ShapeDtypeStruct((r, c), BF16))
        vmem += 2 * (r // steps) * c * 6
    res = pl.pallas_call(
        functools.partial(_attn_kernel, tk=tk, tq=tq, skew=skew, n_cast=len(cast_next)),
        grid=(batch, C_HEADS),
        in_specs=in_specs,
        out_specs=out_specs,
        out_shape=out_shape,
        scratch_shapes=[pltpu.VMEM((seq, QK_PAD), BF16),
                        pltpu.VMEM((V_DIM + ONES_ROWS, seq), BF16),
                        pltpu.VMEM((1, seq), F32),
                        pltpu.VMEM((V_DIM + ONES_ROWS, seq), F32)],
        compiler_params=_params(("parallel", "parallel"), vmem),
        name="attention",
    )(q_tr, kn, kr, v_tr, *cast_next)
    return res if cast_next else res[0]


def _proj_residual_kernel(x_ref, a_ref, w_ref, o_ref):
    o_ref[...] = x_ref[...] + _dot(a_ref[...], w_ref[...])


def _proj_residual(x2d, a, w, *, tm):
    t, d = x2d.shape
    k = a.shape[1]
    vmem = 4 * tm * d * 4 + 2 * tm * k * 2 + w.size * 2 + 2 * tm * d * 4
    return pl.pallas_call(
        _proj_residual_kernel,
        grid=(t // tm,),
        in_specs=[pl.BlockSpec((tm, d), lambda i: (i, 0)), pl.BlockSpec((tm, k), lambda i: (i, 0)),
                  _const_spec(w.shape)],
        out_specs=pl.BlockSpec((tm, d), lambda i: (i, 0)),
        out_shape=jax.ShapeDtypeStruct((t, d), F32),
        compiler_params=_params(("parallel",), vmem),
        name="attn_out_proj",
    )(x2d, a, w)


def _swap_halves(w):
    half = w.shape[-1] // 2
    return jnp.concatenate([w[..., half:], w[..., :half]], axis=-1)


def kernel(x, g_mix, g_ffn, g_final, w_in_ab, g_v, w_s, b_s, w_pool, pool_scale, w_out_ab, w_in_c, g_cq, g_ckv, w_uq, w_ukv, w_out_c, w_gate, w_up, w_down):
    batch, seq, d = x.shape
    x2d = x.reshape(batch * seq, d)
    row = lambda a: a.reshape(1, -1)

    b_s_b = jnp.broadcast_to(b_s[0][:, :, None], (A_HEADS, GMLP_BLOCK, GMLP_BLOCK))
    x1 = _l0_mixer(x2d, row(g_mix[0]), w_in_ab[0].astype(BF16), row(g_v[0]), w_s[0], b_s_b,
                   w_pool[0].astype(BF16), row(pool_scale[0]), w_out_ab[0].astype(BF16),
                   seq=seq, tm=512)
    x2 = _ffn(x1, row(g_ffn[0]), w_gate, w_up, w_down, layer=0, tm=1024, tf=256)

    inv_freq = ROPE_THETA ** (-jnp.arange(0, ROPE_DIM, 2, dtype=F32) / ROPE_DIM)
    cos_t, sin_t, cos_tr, sin_tr = _rope_tables(
        jnp.tile(inv_freq, V7X_LANES // inv_freq.shape[0]).reshape(1, -1), seq, 512)
    wc = w_in_c[0]
    w_kr = wc[:, Q_LORA + KV_LORA:]
    w_in_ext = jnp.concatenate([wc, _swap_halves(w_kr)], axis=1).astype(BF16)
    wq = w_uq[0].reshape(Q_LORA, C_HEADS, NOPE_DIM + ROPE_DIM)
    w_uq_ext = jnp.concatenate([wq, _swap_halves(wq[..., NOPE_DIM:])], axis=-1)
    w_uq_tr = w_uq_ext.reshape(Q_LORA, C_HEADS * QK_PAD).T.astype(BF16)
    wkv = w_ukv[0].reshape(KV_LORA, C_HEADS, NOPE_DIM + V_DIM)
    w_kn = wkv[..., :NOPE_DIM].reshape(KV_LORA, -1).astype(BF16)
    w_v_tr = wkv[..., NOPE_DIM:].reshape(KV_LORA, -1).T.astype(BF16)
    score_scale = (NOPE_DIM + ROPE_DIM) ** -0.5 * 1.4426950408889634
    q_tr, kn, kr, v_tr = _mla_proj(x2, row(g_mix[1]), w_in_ext, row(g_cq[0]) * score_scale,
                                   row(g_ckv[0]), w_uq_tr, w_kn, w_v_tr, cos_t, sin_t, cos_tr, sin_tr,
                                   seq=seq, tm=512)
    o, w_gate1, w_up1, w_down1 = _attention(q_tr, kn, kr, v_tr, batch=batch, seq=seq, tk=512, tq=256, skew=3,
                                            cast_next=(w_gate, w_up, w_down), cast_layer=1)
    x3 = _proj_residual(x2, o, w_out_c[0].astype(BF16), tm=512)
    out = _ffn(x3, row(g_ffn[1]), w_gate1, w_up1, w_down1, tm=1024, tf=512, g_final=row(g_final))
    return out.reshape(batch, seq, d)
```

```python
import functools

import jax
import jax.numpy as jnp
from jax import lax
from jax.experimental import pallas as pl
from jax.experimental.pallas import tpu as pltpu

F32 = jnp.float32
BF16 = jnp.bfloat16

CHUNK = 64
EPS = 1e-6
GMLP_BLOCK = 128
A_HEADS = 8
POOL_WINDOWS = (2, 4, 8, 16)
POOL_HALO = 16
C_HEADS = 16
Q_LORA = 512
KV_LORA = 512
NOPE_DIM = 128
ROPE_DIM = 64
V_DIM = 128
ROPE_THETA = 10000.0
QK_PAD = 256
ONES_ROWS = 16

V7X_LANES = 128
V7X_SCOPED_VMEM_BYTES = 60000 * 1024


def _rms(xf, g):
    ms = jnp.mean(xf * xf, axis=-1, keepdims=True)
    return xf * lax.rsqrt(ms + EPS) * g


def _gelu_tanh(x):
    c = 0.7978845608028654
    return 0.5 * x * (1.0 + jnp.tanh(c * (x + 0.044715 * (x * x * x))))


def _dot(a, b):
    return jnp.dot(a, b, preferred_element_type=F32)


def _const_spec(shape):
    nd = len(shape)
    return pl.BlockSpec(shape, lambda *_: (0,) * nd, pipeline_mode=pl.Buffered(1))


def _params(semantics, vmem_bytes):
    return pltpu.CompilerParams(dimension_semantics=semantics,
                                vmem_limit_bytes=min(int(vmem_bytes), V7X_SCOPED_VMEM_BYTES))


def _rope_table_kernel(invf_ref, cos_ref, sin_ref, cos_tr_ref, sin_tr_ref, *, ts):
    i = pl.program_id(0)
    pos = (lax.broadcasted_iota(jnp.int32, (ts, V7X_LANES), 0) + i * ts).astype(F32)
    lane = lax.broadcasted_iota(jnp.int32, (ts, V7X_LANES), 1)
    ang = pos * invf_ref[...]
    c = jnp.cos(ang)
    s = jnp.sin(ang)
    half = ROPE_DIM // 2
    cos_t = jnp.where(lane < ROPE_DIM, c, 0.0)
    sin_t = jnp.where(lane < half, -s, jnp.where(lane < ROPE_DIM, s, 0.0))
    cos_ref[...] = cos_t
    sin_ref[...] = sin_t
    cos_tr_ref[...] = cos_t.T[0:ROPE_DIM]
    sin_tr_ref[...] = sin_t.T[0:ROPE_DIM]


def _rope_tables(invf_row, seq, ts):
    return pl.pallas_call(
        functools.partial(_rope_table_kernel, ts=ts),
        grid=(seq // ts,),
        in_specs=[pl.BlockSpec((1, V7X_LANES), lambda i: (0, 0))],
        out_specs=[pl.BlockSpec((ts, V7X_LANES), lambda i: (i, 0)),
                   pl.BlockSpec((ts, V7X_LANES), lambda i: (i, 0)),
                   pl.BlockSpec((ROPE_DIM, ts), lambda i: (0, i)),
                   pl.BlockSpec((ROPE_DIM, ts), lambda i: (0, i))],
        out_shape=[jax.ShapeDtypeStruct((seq, V7X_LANES), F32)] * 2
                  + [jax.ShapeDtypeStruct((ROPE_DIM, seq), F32)] * 2,
        compiler_params=_params(("parallel",), 16 * ts * V7X_LANES * 4 + (4 << 20)),
        name="rope_tables",
    )(invf_row)


def _l0_mixer_kernel(x_ref, gmix_ref, win_ref, gv_ref, ws_ref, bs_ref, wpool_ref, psc_ref, wout_ref,
                     o_ref, h_ref, u_ref, v_ref, z_ref, cat_ref, *, tm, tiles_per_seq, d_a, d_g):
    i = pl.program_id(0)
    seq_tile = i % tiles_per_seq
    nb = tm // GMLP_BLOCK

    h_ref[...] = _rms(x_ref[...], gmix_ref[...]).astype(BF16)

    u_ref[...] = _gelu_tanh(_dot(h_ref[...], win_ref[:, 0:d_a]))
    v_ref[...] = _rms(_gelu_tanh(_dot(h_ref[...], win_ref[:, d_a:2 * d_a])), gv_ref[...]).astype(BF16)

    tt = lax.broadcasted_iota(jnp.int32, (GMLP_BLOCK, GMLP_BLOCK), 0)
    ss = lax.broadcasted_iota(jnp.int32, (GMLP_BLOCK, GMLP_BLOCK), 1)
    tri = (ss // CHUNK <= tt // CHUNK).astype(F32)
    hd_w = d_a // A_HEADS
    for hd in range(A_HEADS):
        c0 = hd * hd_w
        wm = (ws_ref[hd] * tri).astype(BF16)
        rhs = jnp.concatenate(
            [v_ref[n * GMLP_BLOCK:(n + 1) * GMLP_BLOCK, c0:c0 + hd_w] for n in range(nb)], axis=1)
        mix = _dot(wm, rhs)
        bias = bs_ref[hd]
        for n in range(nb):
            r0 = n * GMLP_BLOCK
            a = u_ref[r0:r0 + GMLP_BLOCK, c0:c0 + hd_w] * (mix[:, n * hd_w:(n + 1) * hd_w] + bias)
            cat_ref[r0:r0 + GMLP_BLOCK, c0:c0 + hd_w] = a.astype(BF16)

    @pl.when(seq_tile == 0)
    def _():
        z_ref[0:POOL_HALO, :] = jnp.zeros((POOL_HALO, z_ref.shape[1]), F32)

    z_ref[POOL_HALO:POOL_HALO + tm, :] = _dot(h_ref[...], win_ref[:, 2 * d_a:])
    pos1 = lax.broadcasted_iota(jnp.int32, (tm, d_g), 0) + (seq_tile * tm + 1)
    for g, win in enumerate(POOL_WINDOWS):
        zz = z_ref[:, g * d_g:(g + 1) * d_g]
        s = zz
        k = 1
        while k < win:
            s = s + pltpu.roll(s, k, axis=0)
            k *= 2
        cnt = jnp.minimum(pos1, win).astype(F32)
        d = (s[POOL_HALO:] / cnt - zz[POOL_HALO:]).astype(BF16)
        y = _dot(d, wpool_ref[g]) * psc_ref[:, g * d_g:(g + 1) * d_g]
        cat_ref[:, d_a + g * d_g:d_a + (g + 1) * d_g] = y.astype(BF16)
    z_ref[0:POOL_HALO, :] = z_ref[tm:tm + POOL_HALO, :]

    o_ref[...] = x_ref[...] + _dot(cat_ref[...], wout_ref[...])


def _l0_mixer(x2d, gmix, w_in, g_v, w_s, b_s_b, w_pool, pscale, w_out, *, seq, tm):
    t, d = x2d.shape
    n_in = w_in.shape[1]
    d_a = g_v.shape[1]
    d_b = pscale.shape[1]
    d_g = d_b // len(POOL_WINDOWS)
    assert n_in == 2 * d_a + d_b and seq % tm == 0 and tm % GMLP_BLOCK == 0
    vmem = (4 * tm * d * 4
            + w_in.size * 2 + w_out.size * 2 + w_pool.size * 2 + (w_s.size + b_s_b.size) * 4
            + tm * d * 2 + tm * d_a * 4 + tm * d_a * 2 + (tm + POOL_HALO) * d_b * 4 + tm * (d_a + d_b) * 2
            + 4 * tm * d_a * 4)
    return pl.pallas_call(
        functools.partial(_l0_mixer_kernel, tm=tm, tiles_per_seq=seq // tm, d_a=d_a, d_g=d_g),
        grid=(t // tm,),
        in_specs=[pl.BlockSpec((tm, d), lambda i: (i, 0)),
                  _const_spec(gmix.shape), _const_spec(w_in.shape), _const_spec(g_v.shape),
                  _const_spec(w_s.shape), _const_spec(b_s_b.shape), _const_spec(w_pool.shape),
                  _const_spec(pscale.shape), _const_spec(w_out.shape)],
        out_specs=pl.BlockSpec((tm, d), lambda i: (i, 0)),
        out_shape=jax.ShapeDtypeStruct((t, d), F32),
        scratch_shapes=[pltpu.VMEM((tm, d), BF16),
                        pltpu.VMEM((tm, d_a), F32),
                        pltpu.VMEM((tm, d_a), BF16),
                        pltpu.VMEM((tm + POOL_HALO, d_b), F32),
                        pltpu.VMEM((tm, d_a + d_b), BF16)],
        compiler_params=_params(("arbitrary",), vmem),
        name="l0_mixer",
    )(x2d, gmix, w_in, g_v, w_s, b_s_b, w_pool, pscale, w_out)


def _ffn_kernel(*refs, final_norm, emit_bf16, has_init):
    it = iter(refs)
    x_ref, g_ref, wg_ref, wu_ref, wd_ref = next(it), next(it), next(it), next(it), next(it)
    gfin_ref = next(it) if final_norm else None
    if has_init:
        next(it)
    o_ref = next(it)
    bf16_out = [next(it) for _ in range(3)] if emit_bf16 else None
    h_ref = next(it)
    f = pl.program_id(1)

    @pl.when(f == 0)
    def _():
        x = x_ref[...]
        h_ref[...] = _rms(x, g_ref[...]).astype(BF16)
        o_ref[...] = x

    wg = wg_ref[...].astype(BF16)
    wu = wu_ref[...].astype(BF16)
    wd = wd_ref[...].astype(BF16)
    if emit_bf16:
        for dst_ref, w in zip(bf16_out, (wg, wu, wd)):
            dst_ref[...] = w

    h = h_ref[...]
    gate = _dot(h, wg)
    up = _dot(h, wu)
    act = (gate * jax.nn.sigmoid(gate) * up).astype(BF16)
    o_ref[...] += _dot(act, wd)

    if final_norm:
        @pl.when(f == pl.num_programs(1) - 1)
        def _():
            o_ref[...] = _rms(o_ref[...], gfin_ref[...])


def _ffn(x2d, g, w_gate, w_up, w_down, *, tm, tf, layer=None, tile_lo=0, n_tiles=None, out_init=None,
         emit_bf16=False, g_final=None):
    t, d = x2d.shape
    dff = w_gate.shape[-1]
    n_tiles = t // tm - tile_lo if n_tiles is None else n_tiles
    nf = dff // tf
    assert t % tm == 0 and dff % tf == 0 and (not emit_bf16 or n_tiles == 1)
    final_norm = g_final is not None
    wbytes = w_gate.dtype.itemsize

    def wspec(w, blk, imap):
        if w.ndim == 2:
            return pl.BlockSpec(blk, imap)
        return pl.BlockSpec((None,) + blk, lambda i, f: (layer,) + imap(i, f))

    rows = lambda i, f: (i + tile_lo, 0)
    gate_cols = lambda i, f: (0, f)
    down_rows = lambda i, f: (f, 0)
    args = [x2d, g, w_gate, w_up, w_down]
    in_specs = [pl.BlockSpec((tm, d), rows), _const_spec(g.shape),
                wspec(w_gate, (d, tf), gate_cols), wspec(w_up, (d, tf), gate_cols),
                wspec(w_down, (tf, d), down_rows)]
    vmem = 4 * tm * d * 4 + tm * d * 2 + 2 * 3 * d * tf * wbytes + 3 * d * tf * 2 + 4 * tm * tf * 4
    if final_norm:
        args.append(g_final)
        in_specs.append(_const_spec(g_final.shape))
    aliases = {}
    if out_init is not None:
        aliases[len(args)] = 0
        args.append(out_init)
        in_specs.append(pl.BlockSpec(memory_space=pl.ANY))
    out_specs = [pl.BlockSpec((tm, d), rows)]
    out_shape = [jax.ShapeDtypeStruct((t, d), F32)]
    if emit_bf16:
        out_specs += [pl.BlockSpec((d, tf), gate_cols), pl.BlockSpec((d, tf), gate_cols),
                      pl.BlockSpec((tf, d), down_rows)]
        out_shape += [jax.ShapeDtypeStruct((d, dff), BF16)] * 2 + [jax.ShapeDtypeStruct((dff, d), BF16)]
        vmem += 2 * 3 * d * tf * 2
    res = pl.pallas_call(
        functools.partial(_ffn_kernel, final_norm=final_norm, emit_bf16=emit_bf16,
                          has_init=out_init is not None),
        grid=(n_tiles, nf),
        in_specs=in_specs,
        out_specs=out_specs,
        out_shape=out_shape,
        input_output_aliases=aliases,
        scratch_shapes=[pltpu.VMEM((tm, d), BF16)],
        compiler_params=_params(("parallel", "arbitrary"), vmem),
        name="ffn_final" if final_norm else ("ffn_head" if emit_bf16 else "ffn"),
    )(*args)
    return res if emit_bf16 else res[0]


def _mla_proj_kernel(x_ref, gmix_ref, win_ref, gcq_ref, gckv_ref, wuq_tr_ref, wkn_ref, wv_tr_ref,
                     cos_ref, sin_ref, cos_tr_ref, sin_tr_ref,
                     q_tr_ref, kn_ref, kr_ref, v_tr_ref, h_ref, cq_tr_ref, ckv_ref, ckv_tr_ref, *, chunk):
    h_ref[...] = _rms(x_ref[...], gmix_ref[...]).astype(BF16)
    p = _dot(h_ref[...], win_ref[...])
    cq_tr_ref[...] = _rms(p[:, 0:Q_LORA], gcq_ref[...]).T.astype(BF16)
    ckv = _rms(p[:, Q_LORA:Q_LORA + KV_LORA], gckv_ref[...])
    ckv_ref[...] = ckv.astype(BF16)
    ckv_tr_ref[...] = ckv.T.astype(BF16)
    a = p[:, Q_LORA + KV_LORA:]
    kr_ref[...] = (a * cos_ref[...] + pltpu.roll(a, ROPE_DIM, axis=1) * sin_ref[...]).astype(BF16)

    rope_lo, rope_hi = NOPE_DIM, NOPE_DIM + ROPE_DIM
    for hd in range(C_HEADS):
        r0 = hd * QK_PAD
        qh = _dot(wuq_tr_ref[r0:r0 + QK_PAD, :], cq_tr_ref[...])
        q_tr_ref[r0:r0 + rope_lo, :] = qh[0:rope_lo].astype(BF16)
        roped = qh[rope_lo:rope_hi] * cos_tr_ref[...] + qh[rope_hi:] * sin_tr_ref[...]
        q_tr_ref[r0 + rope_lo:r0 + rope_hi, :] = roped.astype(BF16)
        q_tr_ref[r0 + rope_hi:r0 + QK_PAD, :] = jnp.zeros((QK_PAD - rope_hi, qh.shape[1]), BF16)

    for c in range(kn_ref.shape[1] // chunk):
        sl = slice(c * chunk, (c + 1) * chunk)
        kn_ref[:, sl] = _dot(ckv_ref[...], wkn_ref[:, sl]).astype(BF16)
        v_tr_ref[sl, :] = _dot(wv_tr_ref[sl, :], ckv_tr_ref[...]).astype(BF16)


def _mla_proj(x2d, gmix, w_in_ext, g_cq_s, g_ckv, w_uq_tr, w_kn, w_v_tr, cos_t, sin_t, cos_tr, sin_tr,
              *, seq, tm):
    t, d = x2d.shape
    n_q = w_uq_tr.shape[0]
    n_kv = w_kn.shape[1]
    tiles_per_seq = seq // tm
    assert seq % tm == 0 and w_in_ext.shape[1] == Q_LORA + KV_LORA + 2 * ROPE_DIM
    vmem = (2 * tm * d * 4 + (w_in_ext.size + w_uq_tr.size + w_kn.size + w_v_tr.size) * 2
            + 2 * 2 * tm * (V7X_LANES + ROPE_DIM) * 4
            + 2 * tm * (n_q + 2 * n_kv + V7X_LANES) * 2
            + tm * d * 2 + tm * (Q_LORA + 2 * KV_LORA) * 2
            + 4 * tm * w_in_ext.shape[1] * 4)
    row = lambda i: (i, 0)
    col = lambda i: (0, i)
    return pl.pallas_call(
        functools.partial(_mla_proj_kernel, chunk=512),
        grid=(t // tm,),
        in_specs=[pl.BlockSpec((tm, d), row),
                  _const_spec(gmix.shape), _const_spec(w_in_ext.shape), _const_spec(g_cq_s.shape),
                  _const_spec(g_ckv.shape), _const_spec(w_uq_tr.shape), _const_spec(w_kn.shape),
                  _const_spec(w_v_tr.shape),
                  pl.BlockSpec((tm, V7X_LANES), lambda i: (i % tiles_per_seq, 0)),
                  pl.BlockSpec((tm, V7X_LANES), lambda i: (i % tiles_per_seq, 0)),
                  pl.BlockSpec((ROPE_DIM, tm), lambda i: (0, i % tiles_per_seq)),
                  pl.BlockSpec((ROPE_DIM, tm), lambda i: (0, i % tiles_per_seq))],
        out_specs=[pl.BlockSpec((n_q, tm), col), pl.BlockSpec((tm, n_kv), row),
                   pl.BlockSpec((tm, V7X_LANES), row), pl.BlockSpec((n_kv, tm), col)],
        out_shape=[jax.ShapeDtypeStruct((n_q, t), BF16), jax.ShapeDtypeStruct((t, n_kv), BF16),
                   jax.ShapeDtypeStruct((t, V7X_LANES), BF16), jax.ShapeDtypeStruct((n_kv, t), BF16)],
        scratch_shapes=[pltpu.VMEM((tm, d), BF16), pltpu.VMEM((Q_LORA, tm), BF16),
                        pltpu.VMEM((tm, KV_LORA), BF16), pltpu.VMEM((KV_LORA, tm), BF16)],
        compiler_params=_params(("parallel",), vmem),
        name="mla_proj",
    )(x2d, gmix, w_in_ext, g_cq_s, g_ckv, w_uq_tr, w_kn, w_v_tr, cos_t, sin_t, cos_tr, sin_tr)


def _attn_kernel(*refs, tk, tq, skew, n_cast):
    q_tr_ref, kn_ref, kr_ref, v_tr_ref = refs[0:4]
    cast_in = refs[4:4 + n_cast]
    o_ref = refs[4 + n_cast]
    cast_out = refs[5 + n_cast:5 + 2 * n_cast]
    kcat_ref, vaug_ref, m_ref, acc_ref = refs[5 + 2 * n_cast:]
    for src_ref, dst_ref in zip(cast_in, cast_out):
        dst_ref[...] = src_ref[...].astype(BF16)
    seq = kcat_ref.shape[0]
    kcat_ref[:, 0:NOPE_DIM] = kn_ref[...]
    kcat_ref[:, NOPE_DIM:] = kr_ref[...]
    vaug_ref[0:V_DIM, :] = v_tr_ref[...]
    vaug_ref[V_DIM:, :] = jnp.ones((vaug_ref.shape[0] - V_DIM, seq), BF16)

    units = []
    for j in range(seq // tk):
        for i in range(seq // tq):
            k_lo, q_lo = j * tk, i * tq
            k_hi = min(k_lo + tk, q_lo + tq)
            if k_hi > k_lo:
                units.append((j == 0, k_lo, k_hi, q_lo))

    def scores(unit):
        _, k_lo, k_hi, q_lo = unit
        s = _dot(kcat_ref[k_lo:k_hi, :], q_tr_ref[:, q_lo:q_lo + tq])
        if k_hi > q_lo + CHUNK:
            kc = (lax.broadcasted_iota(jnp.int32, s.shape, 0) + k_lo) // CHUNK
            qc = (lax.broadcasted_iota(jnp.int32, s.shape, 1) + q_lo) // CHUNK
            s = jnp.where(kc <= qc, s, -jnp.inf)
        return s

    pending = [scores(u) for u in units[:skew]]
    for n, (first, k_lo, k_hi, q_lo) in enumerate(units):
        if n + skew < len(units):
            pending.append(scores(units[n + skew]))
        s = pending.pop(0)
        qs = slice(q_lo, q_lo + tq)
        v_j = vaug_ref[:, k_lo:k_hi]
        m_blk = jnp.max(s, axis=0, keepdims=True)
        if first:
            m_new = m_blk
            acc_ref[:, qs] = _dot(v_j, jnp.exp2(s - m_new).astype(BF16))
        else:
            m_prev = m_ref[:, qs]
            m_new = jnp.maximum(m_prev, m_blk)
            alpha = jnp.exp2(m_prev - m_new)
            acc_ref[:, qs] = alpha * acc_ref[:, qs] + _dot(v_j, jnp.exp2(s - m_new).astype(BF16))
        m_ref[:, qs] = m_new

    for i in range(seq // tq):
        qs = slice(i * tq, (i + 1) * tq)
        o_ref[qs, :] = (acc_ref[0:V_DIM, qs] / acc_ref[V_DIM:V_DIM + 1, qs]).T.astype(o_ref.dtype)


def _attention(q_tr, kn, kr, v_tr, *, batch, seq, tk, tq, skew, cast_next=(), cast_layer=None):
    t = kn.shape[0]
    steps = batch * C_HEADS
    assert seq % tk == 0 and seq % tq == 0 and tk % CHUNK == 0 and tq % CHUNK == 0
    assert V_DIM == V7X_LANES and NOPE_DIM == V7X_LANES
    vmem = (2 * (QK_PAD + V_DIM) * seq * 2 + 2 * 2 * seq * V7X_LANES * 2 + 2 * seq * V_DIM * 2
            + seq * QK_PAD * 2 + 2 * 8 * seq * 4 + V_DIM * seq * 4 + 24 * tk * tq * 4)
    in_specs = [pl.BlockSpec((QK_PAD, seq), lambda b, h: (h, b)),
                pl.BlockSpec((seq, NOPE_DIM), lambda b, h: (b, h)),
                pl.BlockSpec((seq, V7X_LANES), lambda b, h: (b, 0)),
                pl.BlockSpec((V_DIM, seq), lambda b, h: (h, b))]
    out_specs = [pl.BlockSpec((seq, V_DIM), lambda b, h: (b, h))]
    out_shape = [jax.ShapeDtypeStruct((t, C_HEADS * V_DIM), BF16)]
    for w in cast_next:
        _, r, c = w.shape
        assert r % (steps * 16) == 0
        in_specs.append(pl.BlockSpec((None, r // steps, c), lambda b, h: (cast_layer, b * C_HEADS + h, 0)))
        out_specs.append(pl.BlockSpec((r // steps, c), lambda b, h: (b * C_HEADS + h, 0)))
        out_shape.append(jax.ShapeDtypeStruct((r, c), BF16))
        vmem += 2 * (r // steps) * c * 6
    res = pl.pallas_call(
        functools.partial(_attn_kernel, tk=tk, tq=tq, skew=skew, n_cast=len(cast_next)),
        grid=(batch, C_HEADS),
        in_specs=in_specs,
        out_specs=out_specs,
        out_shape=out_shape,
        scratch_shapes=[pltpu.VMEM((seq, QK_PAD), BF16),
                        pltpu.VMEM((V_DIM + ONES_ROWS, seq), BF16),
                        pltpu.VMEM((1, seq), F32),
                        pltpu.VMEM((V_DIM + ONES_ROWS, seq), F32)],
        compiler_params=_params(("parallel", "parallel"), vmem),
        name="attention",
    )(q_tr, kn, kr, v_tr, *cast_next)
    return res if cast_next else res[0]


def _proj_residual_kernel(x_ref, a_ref, w_ref, o_ref):
    o_ref[...] = x_ref[...] + _dot(a_ref[...], w_ref[...])


def _proj_residual(x2d, a, w, *, tm):
    t, d = x2d.shape
    k = a.shape[1]
    vmem = 4 * tm * d * 4 + 2 * tm * k * 2 + w.size * 2 + 2 * tm * d * 4
    return pl.pallas_call(
        _proj_residual_kernel,
        grid=(t // tm,),
        in_specs=[pl.BlockSpec((tm, d), lambda i: (i, 0)), pl.BlockSpec((tm, k), lambda i: (i, 0)),
                  _const_spec(w.shape)],
        out_specs=pl.BlockSpec((tm, d), lambda i: (i, 0)),
        out_shape=jax.ShapeDtypeStruct((t, d), F32),
        compiler_params=_params(("parallel",), vmem),
        name="attn_out_proj",
    )(x2d, a, w)


def _swap_halves(w):
    half = w.shape[-1] // 2
    return jnp.concatenate([w[..., half:], w[..., :half]], axis=-1)


def kernel(x, g_mix, g_ffn, g_final, w_in_ab, g_v, w_s, b_s, w_pool, pool_scale, w_out_ab, w_in_c, g_cq, g_ckv, w_uq, w_ukv, w_out_c, w_gate, w_up, w_down):
    batch, seq, d = x.shape
    x2d = x.reshape(batch * seq, d)
    row = lambda a: a.reshape(1, -1)

    b_s_b = jnp.broadcast_to(b_s[0][:, :, None], (A_HEADS, GMLP_BLOCK, GMLP_BLOCK))
    x1 = _l0_mixer(x2d, row(g_mix[0]), w_in_ab[0].astype(BF16), row(g_v[0]), w_s[0], b_s_b,
                   w_pool[0].astype(BF16), row(pool_scale[0]), w_out_ab[0].astype(BF16),
                   seq=seq, tm=512)
    x2_head, wg0, wu0, wd0 = _ffn(x1, row(g_ffn[0]), w_gate, w_up, w_down, layer=0, tm=1024, tf=256,
                                  n_tiles=1, emit_bf16=True)
    x2 = _ffn(x1, row(g_ffn[0]), wg0, wu0, wd0, tm=1024, tf=512, tile_lo=1, out_init=x2_head)

    inv_freq = ROPE_THETA ** (-jnp.arange(0, ROPE_DIM, 2, dtype=F32) / ROPE_DIM)
    cos_t, sin_t, cos_tr, sin_tr = _rope_tables(
        jnp.tile(inv_freq, V7X_LANES // inv_freq.shape[0]).reshape(1, -1), seq, 512)
    wc = w_in_c[0]
    w_kr = wc[:, Q_LORA + KV_LORA:]
    w_in_ext = jnp.concatenate([wc, _swap_halves(w_kr)], axis=1).astype(BF16)
    wq = w_uq[0].reshape(Q_LORA, C_HEADS, NOPE_DIM + ROPE_DIM)
    w_uq_ext = jnp.concatenate([wq, _swap_halves(wq[..., NOPE_DIM:])], axis=-1)
    w_uq_tr = w_uq_ext.reshape(Q_LORA, C_HEADS * QK_PAD).T.astype(BF16)
    wkv = w_ukv[0].reshape(KV_LORA, C_HEADS, NOPE_DIM + V_DIM)
    w_kn = wkv[..., :NOPE_DIM].reshape(KV_LORA, -1).astype(BF16)
    w_v_tr = wkv[..., NOPE_DIM:].reshape(KV_LORA, -1).T.astype(BF16)
    score_scale = (NOPE_DIM + ROPE_DIM) ** -0.5 * 1.4426950408889634
    q_tr, kn, kr, v_tr = _mla_proj(x2, row(g_mix[1]), w_in_ext, row(g_cq[0]) * score_scale,
                                   row(g_ckv[0]), w_uq_tr, w_kn, w_v_tr, cos_t, sin_t, cos_tr, sin_tr,
                                   seq=seq, tm=512)
    o, w_gate1, w_up1, w_down1 = _attention(q_tr, kn, kr, v_tr, batch=batch, seq=seq, tk=512, tq=256, skew=3,
                                            cast_next=(w_gate, w_up, w_down), cast_layer=1)
    x3 = _proj_residual(x2, o, w_out_c[0].astype(BF16), tm=512)
    out = _ffn(x3, row(g_ffn[1]), w_gate1, w_up1, w_down1, tm=1024, tf=512, g_final=row(g_final))
    return out.reshape(batch, seq, d)
```

```python
import functools

import jax
import jax.numpy as jnp
from jax import lax
from jax.experimental import pallas as pl
from jax.experimental.pallas import tpu as pltpu

F32 = jnp.float32
BF16 = jnp.bfloat16

CHUNK = 64
EPS = 1e-6
GMLP_BLOCK = 128
A_HEADS = 8
POOL_WINDOWS = (2, 4, 8, 16)
POOL_HALO = 16
C_HEADS = 16
Q_LORA = 512
KV_LORA = 512
NOPE_DIM = 128
ROPE_DIM = 64
V_DIM = 128
ROPE_THETA = 10000.0
QK_PAD = 256
ONES_ROWS = 16

V7X_LANES = 128
V7X_SCOPED_VMEM_BYTES = 60000 * 1024


def _rms(xf, g):
    ms = jnp.mean(xf * xf, axis=-1, keepdims=True)
    return xf * lax.rsqrt(ms + EPS) * g


def _gelu_tanh(x):
    c = 0.7978845608028654
    return 0.5 * x * (1.0 + jnp.tanh(c * (x + 0.044715 * (x * x * x))))


def _dot(a, b):
    return jnp.dot(a, b, preferred_element_type=F32)


def _const_spec(shape):
    nd = len(shape)
    return pl.BlockSpec(shape, lambda *_: (0,) * nd, pipeline_mode=pl.Buffered(1))


def _params(semantics, vmem_bytes):
    return pltpu.CompilerParams(dimension_semantics=semantics,
                                vmem_limit_bytes=min(int(vmem_bytes), V7X_SCOPED_VMEM_BYTES))


def _rope_table_kernel(invf_ref, cos_ref, sin_ref, cos_tr_ref, sin_tr_ref, *, ts):
    i = pl.program_id(0)
    pos = (lax.broadcasted_iota(jnp.int32, (ts, V7X_LANES), 0) + i * ts).astype(F32)
    lane = lax.broadcasted_iota(jnp.int32, (ts, V7X_LANES), 1)
    ang = pos * invf_ref[...]
    c = jnp.cos(ang)
    s = jnp.sin(ang)
    half = ROPE_DIM // 2
    cos_t = jnp.where(lane < ROPE_DIM, c, 0.0)
    sin_t = jnp.where(lane < half, -s, jnp.where(lane < ROPE_DIM, s, 0.0))
    cos_ref[...] = cos_t
    sin_ref[...] = sin_t
    cos_tr_ref[...] = cos_t.T[0:ROPE_DIM]
    sin_tr_ref[...] = sin_t.T[0:ROPE_DIM]


def _rope_tables(invf_row, seq, ts):
    return pl.pallas_call(
        functools.partial(_rope_table_kernel, ts=ts),
        grid=(seq // ts,),
        in_specs=[pl.BlockSpec((1, V7X_LANES), lambda i: (0, 0))],
        out_specs=[pl.BlockSpec((ts, V7X_LANES), lambda i: (i, 0)),
                   pl.BlockSpec((ts, V7X_LANES), lambda i: (i, 0)),
                   pl.BlockSpec((ROPE_DIM, ts), lambda i: (0, i)),
                   pl.BlockSpec((ROPE_DIM, ts), lambda i: (0, i))],
        out_shape=[jax.ShapeDtypeStruct((seq, V7X_LANES), F32)] * 2
                  + [jax.ShapeDtypeStruct((ROPE_DIM, seq), F32)] * 2,
        compiler_params=_params(("parallel",), 16 * ts * V7X_LANES * 4 + (4 << 20)),
        name="rope_tables",
    )(invf_row)


def _l0_mixer_kernel(x_ref, gmix_ref, win_ref, gv_ref, ws_ref, bs_ref, wpool_ref, psc_ref, wout_ref, gffn_ref,
                     o_ref, hn_ref, h_ref, u_ref, v_ref, z_ref, cat_ref, *, tm, tiles_per_seq, d_a, d_g):
    i = pl.program_id(0)
    seq_tile = i % tiles_per_seq
    nb = tm // GMLP_BLOCK

    h_ref[...] = _rms(x_ref[...], gmix_ref[...]).astype(BF16)

    u_ref[...] = _gelu_tanh(_dot(h_ref[...], win_ref[:, 0:d_a]))
    v_ref[...] = _rms(_gelu_tanh(_dot(h_ref[...], win_ref[:, d_a:2 * d_a])), gv_ref[...]).astype(BF16)

    tt = lax.broadcasted_iota(jnp.int32, (GMLP_BLOCK, GMLP_BLOCK), 0)
    ss = lax.broadcasted_iota(jnp.int32, (GMLP_BLOCK, GMLP_BLOCK), 1)
    tri = (ss // CHUNK <= tt // CHUNK).astype(F32)
    hd_w = d_a // A_HEADS
    for hd in range(A_HEADS):
        c0 = hd * hd_w
        wm = (ws_ref[hd] * tri).astype(BF16)
        rhs = jnp.concatenate(
            [v_ref[n * GMLP_BLOCK:(n + 1) * GMLP_BLOCK, c0:c0 + hd_w] for n in range(nb)], axis=1)
        mix = _dot(wm, rhs)
        bias = bs_ref[hd]
        for n in range(nb):
            r0 = n * GMLP_BLOCK
            a = u_ref[r0:r0 + GMLP_BLOCK, c0:c0 + hd_w] * (mix[:, n * hd_w:(n + 1) * hd_w] + bias)
            cat_ref[r0:r0 + GMLP_BLOCK, c0:c0 + hd_w] = a.astype(BF16)

    @pl.when(seq_tile == 0)
    def _():
        z_ref[0:POOL_HALO, :] = jnp.zeros((POOL_HALO, z_ref.shape[1]), F32)

    z_ref[POOL_HALO:POOL_HALO + tm, :] = _dot(h_ref[...], win_ref[:, 2 * d_a:])
    pos1 = lax.broadcasted_iota(jnp.int32, (tm, d_g), 0) + (seq_tile * tm + 1)
    for g, win in enumerate(POOL_WINDOWS):
        zz = z_ref[:, g * d_g:(g + 1) * d_g]
        s = zz
        k = 1
        while k < win:
            s = s + pltpu.roll(s, k, axis=0)
            k *= 2
        cnt = jnp.minimum(pos1, win).astype(F32)
        d = (s[POOL_HALO:] / cnt - zz[POOL_HALO:]).astype(BF16)
        y = _dot(d, wpool_ref[g]) * psc_ref[:, g * d_g:(g + 1) * d_g]
        cat_ref[:, d_a + g * d_g:d_a + (g + 1) * d_g] = y.astype(BF16)
    z_ref[0:POOL_HALO, :] = z_ref[tm:tm + POOL_HALO, :]

    x_new = x_ref[...] + _dot(cat_ref[...], wout_ref[...])
    o_ref[...] = x_new
    hn_ref[...] = _rms(x_new, gffn_ref[...]).astype(BF16)


def _l0_mixer(x2d, gmix, w_in, g_v, w_s, b_s_b, w_pool, pscale, w_out, g_ffn, *, seq, tm):
    t, d = x2d.shape
    n_in = w_in.shape[1]
    d_a = g_v.shape[1]
    d_b = pscale.shape[1]
    d_g = d_b // len(POOL_WINDOWS)
    assert n_in == 2 * d_a + d_b and seq % tm == 0 and tm % GMLP_BLOCK == 0
    vmem = (4 * tm * d * 4 + 2 * tm * d * 2
            + w_in.size * 2 + w_out.size * 2 + w_pool.size * 2 + (w_s.size + b_s_b.size) * 4
            + tm * d * 2 + tm * d_a * 4 + tm * d_a * 2 + (tm + POOL_HALO) * d_b * 4 + tm * (d_a + d_b) * 2
            + 4 * tm * d_a * 4)
    return pl.pallas_call(
        functools.partial(_l0_mixer_kernel, tm=tm, tiles_per_seq=seq // tm, d_a=d_a, d_g=d_g),
        grid=(t // tm,),
        in_specs=[pl.BlockSpec((tm, d), lambda i: (i, 0)),
                  _const_spec(gmix.shape), _const_spec(w_in.shape), _const_spec(g_v.shape),
                  _const_spec(w_s.shape), _const_spec(b_s_b.shape), _const_spec(w_pool.shape),
                  _const_spec(pscale.shape), _const_spec(w_out.shape), _const_spec(g_ffn.shape)],
        out_specs=[pl.BlockSpec((tm, d), lambda i: (i, 0)), pl.BlockSpec((tm, d), lambda i: (i, 0))],
        out_shape=[jax.ShapeDtypeStruct((t, d), F32), jax.ShapeDtypeStruct((t, d), BF16)],
        scratch_shapes=[pltpu.VMEM((tm, d), BF16),
                        pltpu.VMEM((tm, d_a), F32),
                        pltpu.VMEM((tm, d_a), BF16),
                        pltpu.VMEM((tm + POOL_HALO, d_b), F32),
                        pltpu.VMEM((tm, d_a + d_b), BF16)],
        compiler_params=_params(("arbitrary",), vmem),
        name="l0_mixer",
    )(x2d, gmix, w_in, g_v, w_s, b_s_b, w_pool, pscale, w_out, g_ffn)


def _ffn_kernel(*refs, final_norm, emit_bf16, has_init):
    it = iter(refs)
    x_ref, h_ref, wg_ref, wu_ref, wd_ref = next(it), next(it), next(it), next(it), next(it)
    gfin_ref = next(it) if final_norm else None
    if has_init:
        next(it)
    o_ref = next(it)
    bf16_out = [next(it) for _ in range(3)] if emit_bf16 else None
    f = pl.program_id(1)

    def step(first):
        wg = wg_ref[...].astype(BF16)
        wu = wu_ref[...].astype(BF16)
        wd = wd_ref[...].astype(BF16)
        if emit_bf16:
            for dst_ref, w in zip(bf16_out, (wg, wu, wd)):
                dst_ref[...] = w
        h = h_ref[...]
        gate = _dot(h, wg)
        up = _dot(h, wu)
        act = (gate * jax.nn.sigmoid(gate) * up).astype(BF16)
        down = _dot(act, wd)
        if first:
            o_ref[...] = x_ref[...] + down
        else:
            o_ref[...] += down

    pl.when(f == 0)(functools.partial(step, True))
    pl.when(f != 0)(functools.partial(step, False))

    if final_norm:
        @pl.when(f == pl.num_programs(1) - 1)
        def _():
            o_ref[...] = _rms(o_ref[...], gfin_ref[...])


def _ffn(x2d, h2d, w_gate, w_up, w_down, *, tm, tf, layer=None, tile_lo=0, n_tiles=None, out_init=None,
         emit_bf16=False, g_final=None):
    t, d = x2d.shape
    dff = w_gate.shape[-1]
    n_tiles = t // tm - tile_lo if n_tiles is None else n_tiles
    nf = dff // tf
    assert t % tm == 0 and dff % tf == 0 and (not emit_bf16 or n_tiles == 1)
    final_norm = g_final is not None
    wbytes = w_gate.dtype.itemsize

    def wspec(w, blk, imap):
        if w.ndim == 2:
            return pl.BlockSpec(blk, imap)
        return pl.BlockSpec((None,) + blk, lambda i, f: (layer,) + imap(i, f))

    rows = lambda i, f: (i + tile_lo, 0)
    gate_cols = lambda i, f: (0, f)
    down_rows = lambda i, f: (f, 0)
    args = [x2d, h2d, w_gate, w_up, w_down]
    in_specs = [pl.BlockSpec((tm, d), rows), pl.BlockSpec((tm, d), rows),
                wspec(w_gate, (d, tf), gate_cols), wspec(w_up, (d, tf), gate_cols),
                wspec(w_down, (tf, d), down_rows)]
    vmem = 4 * tm * d * 4 + 2 * tm * d * 2 + 2 * 3 * d * tf * wbytes + 3 * d * tf * 2 + 4 * tm * tf * 4
    if final_norm:
        args.append(g_final)
        in_specs.append(_const_spec(g_final.shape))
    aliases = {}
    if out_init is not None:
        aliases[len(args)] = 0
        args.append(out_init)
        in_specs.append(pl.BlockSpec(memory_space=pl.ANY))
    out_specs = [pl.BlockSpec((tm, d), rows)]
    out_shape = [jax.ShapeDtypeStruct((t, d), F32)]
    if emit_bf16:
        out_specs += [pl.BlockSpec((d, tf), gate_cols), pl.BlockSpec((d, tf), gate_cols),
                      pl.BlockSpec((tf, d), down_rows)]
        out_shape += [jax.ShapeDtypeStruct((d, dff), BF16)] * 2 + [jax.ShapeDtypeStruct((dff, d), BF16)]
        vmem += 2 * 3 * d * tf * 2
    res = pl.pallas_call(
        functools.partial(_ffn_kernel, final_norm=final_norm, emit_bf16=emit_bf16,
                          has_init=out_init is not None),
        grid=(n_tiles, nf),
        in_specs=in_specs,
        out_specs=out_specs,
        out_shape=out_shape,
        input_output_aliases=aliases,
        compiler_params=_params(("parallel", "arbitrary"), vmem),
        name="ffn_final" if final_norm else ("ffn_head" if emit_bf16 else "ffn"),
    )(*args)
    return res if emit_bf16 else res[0]


def _mla_proj_kernel(x_ref, gmix_ref, win_ref, gcq_ref, gckv_ref, wuq_tr_ref, wkn_ref, wv_tr_ref,
                     cos_ref, sin_ref, cos_tr_ref, sin_tr_ref,
                     q_tr_ref, kn_ref, kr_ref, v_tr_ref, h_ref, cq_tr_ref, ckv_ref, ckv_tr_ref, *, chunk):
    h_ref[...] = _rms(x_ref[...], gmix_ref[...]).astype(BF16)
    p = _dot(h_ref[...], win_ref[...])
    cq_tr_ref[...] = _rms(p[:, 0:Q_LORA], gcq_ref[...]).T.astype(BF16)
    ckv = _rms(p[:, Q_LORA:Q_LORA + KV_LORA], gckv_ref[...])
    ckv_ref[...] = ckv.astype(BF16)
    ckv_tr_ref[...] = ckv.T.astype(BF16)
    a = p[:, Q_LORA + KV_LORA:]
    kr_ref[...] = (a * cos_ref[...] + pltpu.roll(a, ROPE_DIM, axis=1) * sin_ref[...]).astype(BF16)

    rope_lo, rope_hi = NOPE_DIM, NOPE_DIM + ROPE_DIM
    for hd in range(C_HEADS):
        r0 = hd * QK_PAD
        qh = _dot(wuq_tr_ref[r0:r0 + QK_PAD, :], cq_tr_ref[...])
        q_tr_ref[r0:r0 + rope_lo, :] = qh[0:rope_lo].astype(BF16)
        roped = qh[rope_lo:rope_hi] * cos_tr_ref[...] + qh[rope_hi:] * sin_tr_ref[...]
        q_tr_ref[r0 + rope_lo:r0 + rope_hi, :] = roped.astype(BF16)
        q_tr_ref[r0 + rope_hi:r0 + QK_PAD, :] = jnp.zeros((QK_PAD - rope_hi, qh.shape[1]), BF16)

    for c in range(kn_ref.shape[1] // chunk):
        sl = slice(c * chunk, (c + 1) * chunk)
        kn_ref[:, sl] = _dot(ckv_ref[...], wkn_ref[:, sl]).astype(BF16)
        v_tr_ref[sl, :] = _dot(wv_tr_ref[sl, :], ckv_tr_ref[...]).astype(BF16)


def _mla_proj(x2d, gmix, w_in_ext, g_cq_s, g_ckv, w_uq_tr, w_kn, w_v_tr, cos_t, sin_t, cos_tr, sin_tr,
              *, seq, tm):
    t, d = x2d.shape
    n_q = w_uq_tr.shape[0]
    n_kv = w_kn.shape[1]
    tiles_per_seq = seq // tm
    assert seq % tm == 0 and w_in_ext.shape[1] == Q_LORA + KV_LORA + 2 * ROPE_DIM
    vmem = (2 * tm * d * 4 + (w_in_ext.size + w_uq_tr.size + w_kn.size + w_v_tr.size) * 2
            + 2 * 2 * tm * (V7X_LANES + ROPE_DIM) * 4
            + 2 * tm * (n_q + 2 * n_kv + V7X_LANES) * 2
            + tm * d * 2 + tm * (Q_LORA + 2 * KV_LORA) * 2
            + 4 * tm * w_in_ext.shape[1] * 4)
    row = lambda i: (i, 0)
    col = lambda i: (0, i)
    return pl.pallas_call(
        functools.partial(_mla_proj_kernel, chunk=512),
        grid=(t // tm,),
        in_specs=[pl.BlockSpec((tm, d), row),
                  _const_spec(gmix.shape), _const_spec(w_in_ext.shape), _const_spec(g_cq_s.shape),
                  _const_spec(g_ckv.shape), _const_spec(w_uq_tr.shape), _const_spec(w_kn.shape),
                  _const_spec(w_v_tr.shape),
                  pl.BlockSpec((tm, V7X_LANES), lambda i: (i % tiles_per_seq, 0)),
                  pl.BlockSpec((tm, V7X_LANES), lambda i: (i % tiles_per_seq, 0)),
                  pl.BlockSpec((ROPE_DIM, tm), lambda i: (0, i % tiles_per_seq)),
                  pl.BlockSpec((ROPE_DIM, tm), lambda i: (0, i % tiles_per_seq))],
        out_specs=[pl.BlockSpec((n_q, tm), col), pl.BlockSpec((tm, n_kv), row),
                   pl.BlockSpec((tm, V7X_LANES), row), pl.BlockSpec((n_kv, tm), col)],
        out_shape=[jax.ShapeDtypeStruct((n_q, t), BF16), jax.ShapeDtypeStruct((t, n_kv), BF16),
                   jax.ShapeDtypeStruct((t, V7X_LANES), BF16), jax.ShapeDtypeStruct((n_kv, t), BF16)],
        scratch_shapes=[pltpu.VMEM((tm, d), BF16), pltpu.VMEM((Q_LORA, tm), BF16),
                        pltpu.VMEM((tm, KV_LORA), BF16), pltpu.VMEM((KV_LORA, tm), BF16)],
        compiler_params=_params(("parallel",), vmem),
        name="mla_proj",
    )(x2d, gmix, w_in_ext, g_cq_s, g_ckv, w_uq_tr, w_kn, w_v_tr, cos_t, sin_t, cos_tr, sin_tr)


def _attn_kernel(*refs, tk, tq, skew, n_cast):
    q_tr_ref, kn_ref, kr_ref, v_tr_ref = refs[0:4]
    cast_in = refs[4:4 + n_cast]
    o_ref = refs[4 + n_cast]
    cast_out = refs[5 + n_cast:5 + 2 * n_cast]
    kcat_ref, vaug_ref, m_ref, acc_ref = refs[5 + 2 * n_cast:]
    for src_ref, dst_ref in zip(cast_in, cast_out):
        dst_ref[...] = src_ref[...].astype(BF16)
    seq = kcat_ref.shape[0]
    kcat_ref[:, 0:NOPE_DIM] = kn_ref[...]
    kcat_ref[:, NOPE_DIM:] = kr_ref[...]
    vaug_ref[0:V_DIM, :] = v_tr_ref[...]
    vaug_ref[V_DIM:, :] = jnp.ones((vaug_ref.shape[0] - V_DIM, seq), BF16)

    units = []
    for j in range(seq // tk):
        for i in range(seq // tq):
            k_lo, q_lo = j * tk, i * tq
            k_hi = min(k_lo + tk, q_lo + tq)
            if k_hi > k_lo:
                units.append((j == 0, k_lo, k_hi, q_lo))

    def scores(unit):
        _, k_lo, k_hi, q_lo = unit
        s = _dot(kcat_ref[k_lo:k_hi, :], q_tr_ref[:, q_lo:q_lo + tq])
        if k_hi > q_lo + CHUNK:
            kc = (lax.broadcasted_iota(jnp.int32, s.shape, 0) + k_lo) // CHUNK
            qc = (lax.broadcasted_iota(jnp.int32, s.shape, 1) + q_lo) // CHUNK
            s = jnp.where(kc <= qc, s, -jnp.inf)
        return s

    pending = [scores(u) for u in units[:skew]]
    for n, (first, k_lo, k_hi, q_lo) in enumerate(units):
        if n + skew < len(units):
            pending.append(scores(units[n + skew]))
        s = pending.pop(0)
        qs = slice(q_lo, q_lo + tq)
        v_j = vaug_ref[:, k_lo:k_hi]
        m_blk = jnp.max(s, axis=0, keepdims=True)
        if first:
            m_new = m_blk
            acc_ref[:, qs] = _dot(v_j, jnp.exp2(s - m_new).astype(BF16))
        else:
            m_prev = m_ref[:, qs]
            m_new = jnp.maximum(m_prev, m_blk)
            alpha = jnp.exp2(m_prev - m_new)
            acc_ref[:, qs] = alpha * acc_ref[:, qs] + _dot(v_j, jnp.exp2(s - m_new).astype(BF16))
        m_ref[:, qs] = m_new

    for i in range(seq // tq):
        qs = slice(i * tq, (i + 1) * tq)
        o_ref[qs, :] = (acc_ref[0:V_DIM, qs] / acc_ref[V_DIM:V_DIM + 1, qs]).T.astype(o_ref.dtype)


def _attention(q_tr, kn, kr, v_tr, *, batch, seq, tk, tq, skew, cast_next=(), cast_layer=None):
    t = kn.shape[0]
    steps = batch * C_HEADS
    assert seq % tk == 0 and seq % tq == 0 and tk % CHUNK == 0 and tq % CHUNK == 0
    assert V_DIM == V7X_LANES and NOPE_DIM == V7X_LANES
    vmem = (2 * (QK_PAD + V_DIM) * seq * 2 + 2 * 2 * seq * V7X_LANES * 2 + 2 * seq * V_DIM * 2
            + seq * QK_PAD * 2 + 2 * 8 * seq * 4 + V_DIM * seq * 4 + 24 * tk * tq * 4)
    in_specs = [pl.BlockSpec((QK_PAD, seq), lambda b, h: (h, b)),
                pl.BlockSpec((seq, NOPE_DIM), lambda b, h: (b, h)),
                pl.BlockSpec((seq, V7X_LANES), lambda b, h: (b, 0)),
                pl.BlockSpec((V_DIM, seq), lambda b, h: (h, b))]
    out_specs = [pl.BlockSpec((seq, V_DIM), lambda b, h: (b, h))]
    out_shape = [jax.ShapeDtypeStruct((t, C_HEADS * V_DIM), BF16)]
    for w in cast_next:
        _, r, c = w.shape
        assert r % (steps * 16) == 0
        in_specs.append(pl.BlockSpec((None, r // steps, c), lambda b, h: (cast_layer, b * C_HEADS + h, 0)))
        out_specs.append(pl.BlockSpec((r // steps, c), lambda b, h: (b * C_HEADS + h, 0)))
        out_shape.append(jax.ShapeDtypeStruct((r, c), BF16))
        vmem += 2 * (r // steps) * c * 6
    res = pl.pallas_call(
        functools.partial(_attn_kernel, tk=tk, tq=tq, skew=skew, n_cast=len(cast_next)),
        grid=(batch, C_HEADS),
        in_specs=in_specs,
        out_specs=out_specs,
        out_shape=out_shape,
        scratch_shapes=[pltpu.VMEM((seq, QK_PAD), BF16),
                        pltpu.VMEM((V_DIM + ONES_ROWS, seq), BF16),
                        pltpu.VMEM((1, seq), F32),
                        pltpu.VMEM((V_DIM + ONES_ROWS, seq), F32)],
        compiler_params=_params(("parallel", "parallel"), vmem),
        name="attention",
    )(q_tr, kn, kr, v_tr, *cast_next)
    return res if cast_next else res[0]


def _proj_residual_kernel(x_ref, a_ref, w_ref, gffn_ref, o_ref, hn_ref):
    x_new = x_ref[...] + _dot(a_ref[...], w_ref[...])
    o_ref[...] = x_new
    hn_ref[...] = _rms(x_new, gffn_ref[...]).astype(BF16)


def _proj_residual(x2d, a, w, g_ffn, *, tm):
    t, d = x2d.shape
    k = a.shape[1]
    vmem = 4 * tm * d * 4 + 2 * tm * (k + d) * 2 + w.size * 2 + 2 * tm * d * 4
    return pl.pallas_call(
        _proj_residual_kernel,
        grid=(t // tm,),
        in_specs=[pl.BlockSpec((tm, d), lambda i: (i, 0)), pl.BlockSpec((tm, k), lambda i: (i, 0)),
                  _const_spec(w.shape), _const_spec(g_ffn.shape)],
        out_specs=[pl.BlockSpec((tm, d), lambda i: (i, 0)), pl.BlockSpec((tm, d), lambda i: (i, 0))],
        out_shape=[jax.ShapeDtypeStruct((t, d), F32), jax.ShapeDtypeStruct((t, d), BF16)],
        compiler_params=_params(("parallel",), vmem),
        name="attn_out_proj",
    )(x2d, a, w, g_ffn)


def _swap_halves(w):
    half = w.shape[-1] // 2
    return jnp.concatenate([w[..., half:], w[..., :half]], axis=-1)


def kernel(x, g_mix, g_ffn, g_final, w_in_ab, g_v, w_s, b_s, w_pool, pool_scale, w_out_ab, w_in_c, g_cq, g_ckv, w_uq, w_ukv, w_out_c, w_gate, w_up, w_down):
    batch, seq, d = x.shape
    x2d = x.reshape(batch * seq, d)
    row = lambda a: a.reshape(1, -1)

    b_s_b = jnp.broadcast_to(b_s[0][:, :, None], (A_HEADS, GMLP_BLOCK, GMLP_BLOCK))
    x1, h1 = _l0_mixer(x2d, row(g_mix[0]), w_in_ab[0].astype(BF16), row(g_v[0]), w_s[0], b_s_b,
                       w_pool[0].astype(BF16), row(pool_scale[0]), w_out_ab[0].astype(BF16),
                       row(g_ffn[0]), seq=seq, tm=512)
    x2_head, wg0, wu0, wd0 = _ffn(x1, h1, w_gate, w_up, w_down, layer=0, tm=1024, tf=256,
                                  n_tiles=1, emit_bf16=True)
    x2 = _ffn(x1, h1, wg0, wu0, wd0, tm=1024, tf=512, tile_lo=1, out_init=x2_head)

    inv_freq = ROPE_THETA ** (-jnp.arange(0, ROPE_DIM, 2, dtype=F32) / ROPE_DIM)
    cos_t, sin_t, cos_tr, sin_tr = _rope_tables(
        jnp.tile(inv_freq, V7X_LANES // inv_freq.shape[0]).reshape(1, -1), seq, 512)
    wc = w_in_c[0]
    w_kr = wc[:, Q_LORA + KV_LORA:]
    w_in_ext = jnp.concatenate([wc, _swap_halves(w_kr)], axis=1).astype(BF16)
    wq = w_uq[0].reshape(Q_LORA, C_HEADS, NOPE_DIM + ROPE_DIM)
    w_uq_ext = jnp.concatenate([wq, _swap_halves(wq[..., NOPE_DIM:])], axis=-1)
    w_uq_tr = w_uq_ext.reshape(Q_LORA, C_HEADS * QK_PAD).T.astype(BF16)
    wkv = w_ukv[0].reshape(KV_LORA, C_HEADS, NOPE_DIM + V_DIM)
    w_kn = wkv[..., :NOPE_DIM].reshape(KV_LORA, -1).astype(BF16)
    w_v_tr = wkv[..., NOPE_DIM:].reshape(KV_LORA, -1).T.astype(BF16)
    score_scale = (NOPE_DIM + ROPE_DIM) ** -0.5 * 1.4426950408889634
    q_tr, kn, kr, v_tr = _mla_proj(x2, row(g_mix[1]), w_in_ext, row(g_cq[0]) * score_scale,
                                   row(g_ckv[0]), w_uq_tr, w_kn, w_v_tr, cos_t, sin_t, cos_tr, sin_tr,
                                   seq=seq, tm=512)
    o, w_gate1, w_up1, w_down1 = _attention(q_tr, kn, kr, v_tr, batch=batch, seq=seq, tk=512, tq=256, skew=3,
                                            cast_next=(w_gate, w_up, w_down), cast_layer=1)
    x3, h3 = _proj_residual(x2, o, w_out_c[0].astype(BF16), row(g_ffn[1]), tm=512)
    out = _ffn(x3, h3, w_gate1, w_up1, w_down1, tm=1024, tf=512, g_final=row(g_final))
    return out.reshape(batch, seq, d)
```

```python
import functools

import jax
import jax.numpy as jnp
from jax import lax
from jax.experimental import pallas as pl
from jax.experimental.pallas import tpu as pltpu

F32 = jnp.float32
BF16 = jnp.bfloat16

CHUNK = 64
EPS = 1e-6
GMLP_BLOCK = 128
A_HEADS = 8
POOL_WINDOWS = (2, 4, 8, 16)
POOL_HALO = 16
C_HEADS = 16
Q_LORA = 512
KV_LORA = 512
NOPE_DIM = 128
ROPE_DIM = 64
V_DIM = 128
ROPE_THETA = 10000.0
QK_PAD = 256
ONES_ROWS = 16

V7X_LANES = 128
V7X_SCOPED_VMEM_BYTES = 60000 * 1024


def _rms(xf, g):
    ms = jnp.mean(xf * xf, axis=-1, keepdims=True)
    return xf * lax.rsqrt(ms + EPS) * g


def _gelu_tanh(x):
    c = 0.7978845608028654
    return 0.5 * x * (1.0 + jnp.tanh(c * (x + 0.044715 * (x * x * x))))


def _dot(a, b):
    return jnp.dot(a, b, preferred_element_type=F32)


def _const_spec(shape):
    nd = len(shape)
    return pl.BlockSpec(shape, lambda *_: (0,) * nd, pipeline_mode=pl.Buffered(1))


def _params(semantics, vmem_bytes):
    return pltpu.CompilerParams(dimension_semantics=semantics,
                                vmem_limit_bytes=min(int(vmem_bytes), V7X_SCOPED_VMEM_BYTES))


def _rope_table_kernel(invf_ref, cos_ref, sin_ref, cos_tr_ref, sin_tr_ref, *, ts):
    i = pl.program_id(0)
    pos = (lax.broadcasted_iota(jnp.int32, (ts, V7X_LANES), 0) + i * ts).astype(F32)
    lane = lax.broadcasted_iota(jnp.int32, (ts, V7X_LANES), 1)
    ang = pos * invf_ref[...]
    c = jnp.cos(ang)
    s = jnp.sin(ang)
    half = ROPE_DIM // 2
    cos_t = jnp.where(lane < ROPE_DIM, c, 0.0)
    sin_t = jnp.where(lane < half, -s, jnp.where(lane < ROPE_DIM, s, 0.0))
    cos_ref[...] = cos_t
    sin_ref[...] = sin_t
    cos_tr_ref[...] = cos_t.T[0:ROPE_DIM]
    sin_tr_ref[...] = sin_t.T[0:ROPE_DIM]


def _rope_tables(invf_row, seq, ts):
    return pl.pallas_call(
        functools.partial(_rope_table_kernel, ts=ts),
        grid=(seq // ts,),
        in_specs=[pl.BlockSpec((1, V7X_LANES), lambda i: (0, 0))],
        out_specs=[pl.BlockSpec((ts, V7X_LANES), lambda i: (i, 0)),
                   pl.BlockSpec((ts, V7X_LANES), lambda i: (i, 0)),
                   pl.BlockSpec((ROPE_DIM, ts), lambda i: (0, i)),
                   pl.BlockSpec((ROPE_DIM, ts), lambda i: (0, i))],
        out_shape=[jax.ShapeDtypeStruct((seq, V7X_LANES), F32)] * 2
                  + [jax.ShapeDtypeStruct((ROPE_DIM, seq), F32)] * 2,
        compiler_params=_params(("parallel",), 16 * ts * V7X_LANES * 4 + (4 << 20)),
        name="rope_tables",
    )(invf_row)


def _l0_mixer_kernel(x_ref, gmix_ref, win_ref, gv_ref, ws_ref, bs_ref, wpool_ref, psc_ref, wout_ref,
                     o_ref, h_ref, u_ref, v_ref, z_ref, cat_ref, *, tm, tiles_per_seq, d_a, d_g):
    i = pl.program_id(0)
    seq_tile = i % tiles_per_seq
    nb = tm // GMLP_BLOCK

    h_ref[...] = _rms(x_ref[...], gmix_ref[...]).astype(BF16)

    u_ref[...] = _gelu_tanh(_dot(h_ref[...], win_ref[:, 0:d_a]))
    v_ref[...] = _rms(_gelu_tanh(_dot(h_ref[...], win_ref[:, d_a:2 * d_a])), gv_ref[...]).astype(BF16)

    tt = lax.broadcasted_iota(jnp.int32, (GMLP_BLOCK, GMLP_BLOCK), 0)
    ss = lax.broadcasted_iota(jnp.int32, (GMLP_BLOCK, GMLP_BLOCK), 1)
    tri = (ss // CHUNK <= tt // CHUNK).astype(F32)
    hd_w = d_a // A_HEADS
    for hd in range(A_HEADS):
        c0 = hd * hd_w
        wm = (ws_ref[hd] * tri).astype(BF16)
        rhs = jnp.concatenate(
            [v_ref[n * GMLP_BLOCK:(n + 1) * GMLP_BLOCK, c0:c0 + hd_w] for n in range(nb)], axis=1)
        mix = _dot(wm, rhs)
        bias = bs_ref[hd]
        for n in range(nb):
            r0 = n * GMLP_BLOCK
            a = u_ref[r0:r0 + GMLP_BLOCK, c0:c0 + hd_w] * (mix[:, n * hd_w:(n + 1) * hd_w] + bias)
            cat_ref[r0:r0 + GMLP_BLOCK, c0:c0 + hd_w] = a.astype(BF16)

    @pl.when(seq_tile == 0)
    def _():
        z_ref[0:POOL_HALO, :] = jnp.zeros((POOL_HALO, z_ref.shape[1]), F32)

    z_ref[POOL_HALO:POOL_HALO + tm, :] = _dot(h_ref[...], win_ref[:, 2 * d_a:])
    pos1 = lax.broadcasted_iota(jnp.int32, (tm, d_g), 0) + (seq_tile * tm + 1)
    for g, win in enumerate(POOL_WINDOWS):
        zz = z_ref[:, g * d_g:(g + 1) * d_g]
        s = zz
        k = 1
        while k < win:
            s = s + pltpu.roll(s, k, axis=0)
            k *= 2
        cnt = jnp.minimum(pos1, win).astype(F32)
        d = (s[POOL_HALO:] / cnt - zz[POOL_HALO:]).astype(BF16)
        y = _dot(d, wpool_ref[g]) * psc_ref[:, g * d_g:(g + 1) * d_g]
        cat_ref[:, d_a + g * d_g:d_a + (g + 1) * d_g] = y.astype(BF16)
    z_ref[0:POOL_HALO, :] = z_ref[tm:tm + POOL_HALO, :]

    o_ref[...] = x_ref[...] + _dot(cat_ref[...], wout_ref[...])


def _l0_mixer(x2d, gmix, w_in, g_v, w_s, b_s_b, w_pool, pscale, w_out, *, seq, tm):
    t, d = x2d.shape
    n_in = w_in.shape[1]
    d_a = g_v.shape[1]
    d_b = pscale.shape[1]
    d_g = d_b // len(POOL_WINDOWS)
    assert n_in == 2 * d_a + d_b and seq % tm == 0 and tm % GMLP_BLOCK == 0
    vmem = (4 * tm * d * 4
            + w_in.size * 2 + w_out.size * 2 + w_pool.size * 2 + (w_s.size + b_s_b.size) * 4
            + tm * d * 2 + tm * d_a * 4 + tm * d_a * 2 + (tm + POOL_HALO) * d_b * 4 + tm * (d_a + d_b) * 2
            + 4 * tm * d_a * 4)
    return pl.pallas_call(
        functools.partial(_l0_mixer_kernel, tm=tm, tiles_per_seq=seq // tm, d_a=d_a, d_g=d_g),
        grid=(t // tm,),
        in_specs=[pl.BlockSpec((tm, d), lambda i: (i, 0)),
                  _const_spec(gmix.shape), _const_spec(w_in.shape), _const_spec(g_v.shape),
                  _const_spec(w_s.shape), _const_spec(b_s_b.shape), _const_spec(w_pool.shape),
                  _const_spec(pscale.shape), _const_spec(w_out.shape)],
        out_specs=pl.BlockSpec((tm, d), lambda i: (i, 0)),
        out_shape=jax.ShapeDtypeStruct((t, d), F32),
        scratch_shapes=[pltpu.VMEM((tm, d), BF16),
                        pltpu.VMEM((tm, d_a), F32),
                        pltpu.VMEM((tm, d_a), BF16),
                        pltpu.VMEM((tm + POOL_HALO, d_b), F32),
                        pltpu.VMEM((tm, d_a + d_b), BF16)],
        compiler_params=_params(("arbitrary",), vmem),
        name="l0_mixer",
    )(x2d, gmix, w_in, g_v, w_s, b_s_b, w_pool, pscale, w_out)


def _ffn_kernel(*refs, final_norm, emit_bf16, has_init):
    it = iter(refs)
    x_ref, g_ref, wg_ref, wu_ref, wd_ref = next(it), next(it), next(it), next(it), next(it)
    gfin_ref = next(it) if final_norm else None
    if has_init:
        next(it)
    o_ref = next(it)
    bf16_out = [next(it) for _ in range(3)] if emit_bf16 else None
    h_ref = next(it)
    f = pl.program_id(1)

    def step(first):
        if first:
            h_ref[...] = _rms(x_ref[...], g_ref[...]).astype(BF16)
        wg = wg_ref[...].astype(BF16)
        wu = wu_ref[...].astype(BF16)
        wd = wd_ref[...].astype(BF16)
        if emit_bf16:
            for dst_ref, w in zip(bf16_out, (wg, wu, wd)):
                dst_ref[...] = w
        h = h_ref[...]
        gate = _dot(h, wg)
        up = _dot(h, wu)
        act = (gate * jax.nn.sigmoid(gate) * up).astype(BF16)
        down = _dot(act, wd)
        if first:
            o_ref[...] = x_ref[...] + down
        else:
            o_ref[...] += down

    pl.when(f == 0)(functools.partial(step, True))
    pl.when(f != 0)(functools.partial(step, False))

    if final_norm:
        @pl.when(f == pl.num_programs(1) - 1)
        def _():
            o_ref[...] = _rms(o_ref[...], gfin_ref[...])


def _ffn(x2d, g, w_gate, w_up, w_down, *, tm, tf, layer=None, tile_lo=0, n_tiles=None, out_init=None,
         emit_bf16=False, g_final=None):
    t, d = x2d.shape
    dff = w_gate.shape[-1]
    n_tiles = t // tm - tile_lo if n_tiles is None else n_tiles
    nf = dff // tf
    assert t % tm == 0 and dff % tf == 0 and (not emit_bf16 or n_tiles == 1)
    final_norm = g_final is not None
    wbytes = w_gate.dtype.itemsize

    def wspec(w, blk, imap):
        if w.ndim == 2:
            return pl.BlockSpec(blk, imap)
        return pl.BlockSpec((None,) + blk, lambda i, f: (layer,) + imap(i, f))

    rows = lambda i, f: (i + tile_lo, 0)
    gate_cols = lambda i, f: (0, f)
    down_rows = lambda i, f: (f, 0)
    args = [x2d, g, w_gate, w_up, w_down]
    in_specs = [pl.BlockSpec((tm, d), rows), _const_spec(g.shape),
                wspec(w_gate, (d, tf), gate_cols), wspec(w_up, (d, tf), gate_cols),
                wspec(w_down, (tf, d), down_rows)]
    vmem = 4 * tm * d * 4 + tm * d * 2 + 2 * 3 * d * tf * wbytes + 3 * d * tf * 2 + 4 * tm * tf * 4
    if final_norm:
        args.append(g_final)
        in_specs.append(_const_spec(g_final.shape))
    aliases = {}
    if out_init is not None:
        aliases[len(args)] = 0
        args.append(out_init)
        in_specs.append(pl.BlockSpec(memory_space=pl.ANY))
    out_specs = [pl.BlockSpec((tm, d), rows)]
    out_shape = [jax.ShapeDtypeStruct((t, d), F32)]
    if emit_bf16:
        out_specs += [pl.BlockSpec((d, tf), gate_cols), pl.BlockSpec((d, tf), gate_cols),
                      pl.BlockSpec((tf, d), down_rows)]
        out_shape += [jax.ShapeDtypeStruct((d, dff), BF16)] * 2 + [jax.ShapeDtypeStruct((dff, d), BF16)]
        vmem += 2 * 3 * d * tf * 2
    res = pl.pallas_call(
        functools.partial(_ffn_kernel, final_norm=final_norm, emit_bf16=emit_bf16,
                          has_init=out_init is not None),
        grid=(n_tiles, nf),
        in_specs=in_specs,
        out_specs=out_specs,
        out_shape=out_shape,
        input_output_aliases=aliases,
        scratch_shapes=[pltpu.VMEM((tm, d), BF16)],
        compiler_params=_params(("parallel", "arbitrary"), vmem),
        name="ffn_final" if final_norm else ("ffn_head" if emit_bf16 else "ffn"),
    )(*args)
    return res if emit_bf16 else res[0]


def _mla_proj_kernel(x_ref, gmix_ref, win_ref, gcq_ref, gckv_ref, wuq_tr_ref, wkn_ref, wv_tr_ref,
                     cos_ref, sin_ref, cos_tr_ref, sin_tr_ref,
                     q_tr_ref, kn_ref, kr_ref, v_tr_ref, h_ref, cq_tr_ref, ckv_ref, ckv_tr_ref, *, chunk):
    h_ref[...] = _rms(x_ref[...], gmix_ref[...]).astype(BF16)
    p = _dot(h_ref[...], win_ref[...])
    cq_tr_ref[...] = _rms(p[:, 0:Q_LORA], gcq_ref[...]).T.astype(BF16)
    ckv = _rms(p[:, Q_LORA:Q_LORA + KV_LORA], gckv_ref[...])
    ckv_ref[...] = ckv.astype(BF16)
    ckv_tr_ref[...] = ckv.T.astype(BF16)
    a = p[:, Q_LORA + KV_LORA:]
    kr_ref[...] = (a * cos_ref[...] + pltpu.roll(a, ROPE_DIM, axis=1) * sin_ref[...]).astype(BF16)

    rope_lo, rope_hi = NOPE_DIM, NOPE_DIM + ROPE_DIM
    for hd in range(C_HEADS):
        r0 = hd * QK_PAD
        qh = _dot(wuq_tr_ref[r0:r0 + QK_PAD, :], cq_tr_ref[...])
        q_tr_ref[r0:r0 + rope_lo, :] = qh[0:rope_lo].astype(BF16)
        roped = qh[rope_lo:rope_hi] * cos_tr_ref[...] + qh[rope_hi:] * sin_tr_ref[...]
        q_tr_ref[r0 + rope_lo:r0 + rope_hi, :] = roped.astype(BF16)
        q_tr_ref[r0 + rope_hi:r0 + QK_PAD, :] = jnp.zeros((QK_PAD - rope_hi, qh.shape[1]), BF16)

    for c in range(kn_ref.shape[1] // chunk):
        sl = slice(c * chunk, (c + 1) * chunk)
        kn_ref[:, sl] = _dot(ckv_ref[...], wkn_ref[:, sl]).astype(BF16)
        v_tr_ref[sl, :] = _dot(wv_tr_ref[sl, :], ckv_tr_ref[...]).astype(BF16)


def _mla_proj(x2d, gmix, w_in_ext, g_cq_s, g_ckv, w_uq_tr, w_kn, w_v_tr, cos_t, sin_t, cos_tr, sin_tr,
              *, seq, tm):
    t, d = x2d.shape
    n_q = w_uq_tr.shape[0]
    n_kv = w_kn.shape[1]
    tiles_per_seq = seq // tm
    assert seq % tm == 0 and w_in_ext.shape[1] == Q_LORA + KV_LORA + 2 * ROPE_DIM
    vmem = (2 * tm * d * 4 + (w_in_ext.size + w_uq_tr.size + w_kn.size + w_v_tr.size) * 2
            + 2 * 2 * tm * (V7X_LANES + ROPE_DIM) * 4
            + 2 * tm * (n_q + 2 * n_kv + V7X_LANES) * 2
            + tm * d * 2 + tm * (Q_LORA + 2 * KV_LORA) * 2
            + 4 * tm * w_in_ext.shape[1] * 4)
    row = lambda i: (i, 0)
    col = lambda i: (0, i)
    return pl.pallas_call(
        functools.partial(_mla_proj_kernel, chunk=512),
        grid=(t // tm,),
        in_specs=[pl.BlockSpec((tm, d), row),
                  _const_spec(gmix.shape), _const_spec(w_in_ext.shape), _const_spec(g_cq_s.shape),
                  _const_spec(g_ckv.shape), _const_spec(w_uq_tr.shape), _const_spec(w_kn.shape),
                  _const_spec(w_v_tr.shape),
                  pl.BlockSpec((tm, V7X_LANES), lambda i: (i % tiles_per_seq, 0)),
                  pl.BlockSpec((tm, V7X_LANES), lambda i: (i % tiles_per_seq, 0)),
                  pl.BlockSpec((ROPE_DIM, tm), lambda i: (0, i % tiles_per_seq)),
                  pl.BlockSpec((ROPE_DIM, tm), lambda i: (0, i % tiles_per_seq))],
        out_specs=[pl.BlockSpec((n_q, tm), col), pl.BlockSpec((tm, n_kv), row),
                   pl.BlockSpec((tm, V7X_LANES), row), pl.BlockSpec((n_kv, tm), col)],
        out_shape=[jax.ShapeDtypeStruct((n_q, t), BF16), jax.ShapeDtypeStruct((t, n_kv), BF16),
                   jax.ShapeDtypeStruct((t, V7X_LANES), BF16), jax.ShapeDtypeStruct((n_kv, t), BF16)],
        scratch_shapes=[pltpu.VMEM((tm, d), BF16), pltpu.VMEM((Q_LORA, tm), BF16),
                        pltpu.VMEM((tm, KV_LORA), BF16), pltpu.VMEM((KV_LORA, tm), BF16)],
        compiler_params=_params(("parallel",), vmem),
        name="mla_proj",
    )(x2d, gmix, w_in_ext, g_cq_s, g_ckv, w_uq_tr, w_kn, w_v_tr, cos_t, sin_t, cos_tr, sin_tr)


def _attn_kernel(*refs, tk, tq, skew, n_cast):
    q_tr_ref, kn_ref, kr_ref, v_tr_ref = refs[0:4]
    cast_in = refs[4:4 + n_cast]
    o_ref = refs[4 + n_cast]
    cast_out = refs[5 + n_cast:5 + 2 * n_cast]
    kcat_ref, vaug_ref, m_ref, acc_ref = refs[5 + 2 * n_cast:]
    for src_ref, dst_ref in zip(cast_in, cast_out):
        dst_ref[...] = src_ref[...].astype(BF16)
    seq = kcat_ref.shape[0]
    kcat_ref[:, 0:NOPE_DIM] = kn_ref[...]
    kcat_ref[:, NOPE_DIM:] = kr_ref[...]
    vaug_ref[0:V_DIM, :] = v_tr_ref[...]
    vaug_ref[V_DIM:, :] = jnp.ones((vaug_ref.shape[0] - V_DIM, seq), BF16)

    units = []
    for j in range(seq // tk):
        for i in range(seq // tq):
            k_lo, q_lo = j * tk, i * tq
            k_hi = min(k_lo + tk, q_lo + tq)
            if k_hi > k_lo:
                units.append((j == 0, k_lo, k_hi, q_lo))

    def scores(unit):
        _, k_lo, k_hi, q_lo = unit
        s = _dot(kcat_ref[k_lo:k_hi, :], q_tr_ref[:, q_lo:q_lo + tq])
        if k_hi > q_lo + CHUNK:
            kc = (lax.broadcasted_iota(jnp.int32, s.shape, 0) + k_lo) // CHUNK
            qc = (lax.broadcasted_iota(jnp.int32, s.shape, 1) + q_lo) // CHUNK
            s = jnp.where(kc <= qc, s, -jnp.inf)
        return s

    pending = [scores(u) for u in units[:skew]]
    for n, (first, k_lo, k_hi, q_lo) in enumerate(units):
        if n + skew < len(units):
            pending.append(scores(units[n + skew]))
        s = pending.pop(0)
        qs = slice(q_lo, q_lo + tq)
        v_j = vaug_ref[:, k_lo:k_hi]
        m_blk = jnp.max(s, axis=0, keepdims=True)
        if first:
            m_new = m_blk
            acc_ref[:, qs] = _dot(v_j, jnp.exp2(s - m_new).astype(BF16))
        else:
            m_prev = m_ref[:, qs]
            m_new = jnp.maximum(m_prev, m_blk)
            alpha = jnp.exp2(m_prev - m_new)
            acc_ref[:, qs] = alpha * acc_ref[:, qs] + _dot(v_j, jnp.exp2(s - m_new).astype(BF16))
        m_ref[:, qs] = m_new

    for i in range(seq // tq):
        qs = slice(i * tq, (i + 1) * tq)
        o_ref[qs, :] = (acc_ref[0:V_DIM, qs] / acc_ref[V_DIM:V_DIM + 1, qs]).T.astype(o_ref.dtype)


def _attention(q_tr, kn, kr, v_tr, *, batch, seq, tk, tq, skew, cast_next=(), cast_layer=None):
    t = kn.shape[0]
    steps = batch * C_HEADS
    assert seq % tk == 0 and seq % tq == 0 and tk % CHUNK == 0 and tq % CHUNK == 0
    assert V_DIM == V7X_LANES and NOPE_DIM == V7X_LANES
    vmem = (2 * (QK_PAD + V_DIM) * seq * 2 + 2 * 2 * seq * V7X_LANES * 2 + 2 * seq * V_DIM * 2
            + seq * QK_PAD * 2 + 2 * 8 * seq * 4 + V_DIM * seq * 4 + 24 * tk * tq * 4)
    in_specs = [pl.BlockSpec((QK_PAD, seq), lambda b, h: (h, b)),
                pl.BlockSpec((seq, NOPE_DIM), lambda b, h: (b, h)),
                pl.BlockSpec((seq, V7X_LANES), lambda b, h: (b, 0)),
                pl.BlockSpec((V_DIM, seq), lambda b, h: (h, b))]
    out_specs = [pl.BlockSpec((seq, V_DIM), lambda b, h: (b, h))]
    out_shape = [jax.ShapeDtypeStruct((t, C_HEADS * V_DIM), BF16)]
    for w in cast_next:
        _, r, c = w.shape
        assert r % (steps * 16) == 0
        in_specs.append(pl.BlockSpec((None, r // steps, c), lambda b, h: (cast_layer, b * C_HEADS + h, 0)))
        out_specs.append(pl.BlockSpec((r // steps, c), lambda b, h: (b * C_HEADS + h, 0)))
        out_shape.append(jax.ShapeDtypeStruct((r, c), BF16))
        vmem += 2 * (r // steps) * c * 6
    res = pl.pallas_call(
        functools.partial(_attn_kernel, tk=tk, tq=tq, skew=skew, n_cast=len(cast_next)),
        grid=(batch, C_HEADS),
        in_specs=in_specs,
        out_specs=out_specs,
        out_shape=out_shape,
        scratch_shapes=[pltpu.VMEM((seq, QK_PAD), BF16),
                        pltpu.VMEM((V_DIM + ONES_ROWS, seq), BF16),
                        pltpu.VMEM((1, seq), F32),
                        pltpu.VMEM((V_DIM + ONES_ROWS, seq), F32)],
        compiler_params=_params(("parallel", "parallel"), vmem),
        name="attention",
    )(q_tr, kn, kr, v_tr, *cast_next)
    return res if cast_next else res[0]


def _proj_residual_kernel(x_ref, a_ref, w_ref, o_ref):
    o_ref[...] = x_ref[...] + _dot(a_ref[...], w_ref[...])


def _proj_residual(x2d, a, w, *, tm):
    t, d = x2d.shape
    k = a.shape[1]
    vmem = 4 * tm * d * 4 + 2 * tm * k * 2 + w.size * 2 + 2 * tm * d * 4
    return pl.pallas_call(
        _proj_residual_kernel,
        grid=(t // tm,),
        in_specs=[pl.BlockSpec((tm, d), lambda i: (i, 0)), pl.BlockSpec((tm, k), lambda i: (i, 0)),
                  _const_spec(w.shape)],
        out_specs=pl.BlockSpec((tm, d), lambda i: (i, 0)),
        out_shape=jax.ShapeDtypeStruct((t, d), F32),
        compiler_params=_params(("parallel",), vmem),
        name="attn_out_proj",
    )(x2d, a, w)


def _swap_halves(w):
    half = w.shape[-1] // 2
    return jnp.concatenate([w[..., half:], w[..., :half]], axis=-1)


def kernel(x, g_mix, g_ffn, g_final, w_in_ab, g_v, w_s, b_s, w_pool, pool_scale, w_out_ab, w_in_c, g_cq, g_ckv, w_uq, w_ukv, w_out_c, w_gate, w_up, w_down):
    batch, seq, d = x.shape
    x2d = x.reshape(batch * seq, d)
    row = lambda a: a.reshape(1, -1)

    b_s_b = jnp.broadcast_to(b_s[0][:, :, None], (A_HEADS, GMLP_BLOCK, GMLP_BLOCK))
    x1 = _l0_mixer(x2d, row(g_mix[0]), w_in_ab[0].astype(BF16), row(g_v[0]), w_s[0], b_s_b,
                   w_pool[0].astype(BF16), row(pool_scale[0]), w_out_ab[0].astype(BF16),
                   seq=seq, tm=512)
    x2_head, wg0, wu0, wd0 = _ffn(x1, row(g_ffn[0]), w_gate, w_up, w_down, layer=0, tm=1024, tf=256,
                                  n_tiles=1, emit_bf16=True)
    x2 = _ffn(x1, row(g_ffn[0]), wg0, wu0, wd0, tm=1024, tf=512, tile_lo=1, out_init=x2_head)

    inv_freq = ROPE_THETA ** (-jnp.arange(0, ROPE_DIM, 2, dtype=F32) / ROPE_DIM)
    cos_t, sin_t, cos_tr, sin_tr = _rope_tables(
        jnp.tile(inv_freq, V7X_LANES // inv_freq.shape[0]).reshape(1, -1), seq, 512)
    wc = w_in_c[0]
    w_kr = wc[:, Q_LORA + KV_LORA:]
    w_in_ext = jnp.concatenate([wc, _swap_halves(w_kr)], axis=1).astype(BF16)
    wq = w_uq[0].reshape(Q_LORA, C_HEADS, NOPE_DIM + ROPE_DIM)
    w_uq_ext = jnp.concatenate([wq, _swap_halves(wq[..., NOPE_DIM:])], axis=-1)
    w_uq_tr = w_uq_ext.reshape(Q_LORA, C_HEADS * QK_PAD).T.astype(BF16)
    wkv = w_ukv[0].reshape(KV_LORA, C_HEADS, NOPE_DIM + V_DIM)
    w_kn = wkv[..., :NOPE_DIM].reshape(KV_LORA, -1).astype(BF16)
    w_v_tr = wkv[..., NOPE_DIM:].reshape(KV_LORA, -1).T.astype(BF16)
    score_scale = (NOPE_DIM + ROPE_DIM) ** -0.5 * 1.4426950408889634
    q_tr, kn, kr, v_tr = _mla_proj(x2, row(g_mix[1]), w_in_ext, row(g_cq[0]) * score_scale,
                                   row(g_ckv[0]), w_uq_tr, w_kn, w_v_tr, cos_t, sin_t, cos_tr, sin_tr,
                                   seq=seq, tm=512)
    o, w_gate1, w_up1, w_down1 = _attention(q_tr, kn, kr, v_tr, batch=batch, seq=seq, tk=512, tq=256, skew=3,
                                            cast_next=(w_gate, w_up, w_down), cast_layer=1)
    x3 = _proj_residual(x2, o, w_out_c[0].astype(BF16), tm=512)
    out = _ffn(x3, row(g_ffn[1]), w_gate1, w_up1, w_down1, tm=1024, tf=512, g_final=row(g_final))
    return out.reshape(batch, seq, d)
```

```python
import functools

import jax
import jax.numpy as jnp
from jax import lax
from jax.experimental import pallas as pl
from jax.experimental.pallas import tpu as pltpu

F32 = jnp.float32
BF16 = jnp.bfloat16

CHUNK = 64
EPS = 1e-6
GMLP_BLOCK = 128
A_HEADS = 8
POOL_WINDOWS = (2, 4, 8, 16)
POOL_HALO = 16
C_HEADS = 16
Q_LORA = 512
KV_LORA = 512
NOPE_DIM = 128
ROPE_DIM = 64
V_DIM = 128
ROPE_THETA = 10000.0
QK_PAD = 256
ONES_ROWS = 16

V7X_LANES = 128
V7X_SCOPED_VMEM_BYTES = 60000 * 1024


def _rms(xf, g):
    ms = jnp.mean(xf * xf, axis=-1, keepdims=True)
    return xf * lax.rsqrt(ms + EPS) * g


def _gelu_tanh(x):
    c = 0.7978845608028654
    return 0.5 * x * (1.0 + jnp.tanh(c * (x + 0.044715 * (x * x * x))))


def _dot(a, b):
    return jnp.dot(a, b, preferred_element_type=F32)


def _const_spec(shape):
    nd = len(shape)
    return pl.BlockSpec(shape, lambda *_: (0,) * nd, pipeline_mode=pl.Buffered(1))


def _params(semantics, vmem_bytes):
    return pltpu.CompilerParams(dimension_semantics=semantics,
                                vmem_limit_bytes=min(int(vmem_bytes), V7X_SCOPED_VMEM_BYTES))


def _rope_table_kernel(invf_ref, cos_ref, sin_ref, cos_tr_ref, sin_tr_ref, *, ts):
    i = pl.program_id(0)
    pos = (lax.broadcasted_iota(jnp.int32, (ts, V7X_LANES), 0) + i * ts).astype(F32)
    lane = lax.broadcasted_iota(jnp.int32, (ts, V7X_LANES), 1)
    ang = pos * invf_ref[...]
    c = jnp.cos(ang)
    s = jnp.sin(ang)
    half = ROPE_DIM // 2
    cos_t = jnp.where(lane < ROPE_DIM, c, 0.0)
    sin_t = jnp.where(lane < half, -s, jnp.where(lane < ROPE_DIM, s, 0.0))
    cos_ref[...] = cos_t
    sin_ref[...] = sin_t
    cos_tr_ref[...] = cos_t.T[0:ROPE_DIM]
    sin_tr_ref[...] = sin_t.T[0:ROPE_DIM]


def _rope_tables(invf_row, seq, ts):
    return pl.pallas_call(
        functools.partial(_rope_table_kernel, ts=ts),
        grid=(seq // ts,),
        in_specs=[pl.BlockSpec((1, V7X_LANES), lambda i: (0, 0))],
        out_specs=[pl.BlockSpec((ts, V7X_LANES), lambda i: (i, 0)),
                   pl.BlockSpec((ts, V7X_LANES), lambda i: (i, 0)),
                   pl.BlockSpec((ROPE_DIM, ts), lambda i: (0, i)),
                   pl.BlockSpec((ROPE_DIM, ts), lambda i: (0, i))],
        out_shape=[jax.ShapeDtypeStruct((seq, V7X_LANES), F32)] * 2
                  + [jax.ShapeDtypeStruct((ROPE_DIM, seq), F32)] * 2,
        compiler_params=_params(("parallel",), 16 * ts * V7X_LANES * 4 + (4 << 20)),
        name="rope_tables",
    )(invf_row)


def _l0_mixer_kernel(x_ref, gmix_ref, win_ref, gv_ref, ws_ref, bs_ref, wpool_ref, psc_ref, wout_ref,
                     o_ref, h_ref, u_ref, v_ref, z_ref, cat_ref, *, tm, tiles_per_seq, d_a, d_g):
    i = pl.program_id(0)
    seq_tile = i % tiles_per_seq
    nb = tm // GMLP_BLOCK

    h_ref[...] = _rms(x_ref[...], gmix_ref[...]).astype(BF16)

    u_ref[...] = _gelu_tanh(_dot(h_ref[...], win_ref[:, 0:d_a]))
    v_ref[...] = _rms(_gelu_tanh(_dot(h_ref[...], win_ref[:, d_a:2 * d_a])), gv_ref[...]).astype(BF16)

    tt = lax.broadcasted_iota(jnp.int32, (GMLP_BLOCK, GMLP_BLOCK), 0)
    ss = lax.broadcasted_iota(jnp.int32, (GMLP_BLOCK, GMLP_BLOCK), 1)
    tri = (ss // CHUNK <= tt // CHUNK).astype(F32)
    hd_w = d_a // A_HEADS
    for hd in range(A_HEADS):
        c0 = hd * hd_w
        wm = (ws_ref[hd] * tri).astype(BF16)
        rhs = jnp.concatenate(
            [v_ref[n * GMLP_BLOCK:(n + 1) * GMLP_BLOCK, c0:c0 + hd_w] for n in range(nb)], axis=1)
        mix = _dot(wm, rhs)
        bias = bs_ref[hd]
        for n in range(nb):
            r0 = n * GMLP_BLOCK
            a = u_ref[r0:r0 + GMLP_BLOCK, c0:c0 + hd_w] * (mix[:, n * hd_w:(n + 1) * hd_w] + bias)
            cat_ref[r0:r0 + GMLP_BLOCK, c0:c0 + hd_w] = a.astype(BF16)

    @pl.when(seq_tile == 0)
    def _():
        z_ref[0:POOL_HALO, :] = jnp.zeros((POOL_HALO, z_ref.shape[1]), F32)

    z_ref[POOL_HALO:POOL_HALO + tm, :] = _dot(h_ref[...], win_ref[:, 2 * d_a:])
    pos1 = lax.broadcasted_iota(jnp.int32, (tm, d_g), 0) + (seq_tile * tm + 1)
    for g, win in enumerate(POOL_WINDOWS):
        zz = z_ref[:, g * d_g:(g + 1) * d_g]
        s = zz
        k = 1
        while k < win:
            s = s + pltpu.roll(s, k, axis=0)
            k *= 2
        cnt = jnp.minimum(pos1, win).astype(F32)
        d = (s[POOL_HALO:] / cnt - zz[POOL_HALO:]).astype(BF16)
        y = _dot(d, wpool_ref[g]) * psc_ref[:, g * d_g:(g + 1) * d_g]
        cat_ref[:, d_a + g * d_g:d_a + (g + 1) * d_g] = y.astype(BF16)
    z_ref[0:POOL_HALO, :] = z_ref[tm:tm + POOL_HALO, :]

    o_ref[...] = x_ref[...] + _dot(cat_ref[...], wout_ref[...])


def _l0_mixer(x2d, gmix, w_in, g_v, w_s, b_s_b, w_pool, pscale, w_out, *, seq, tm):
    t, d = x2d.shape
    n_in = w_in.shape[1]
    d_a = g_v.shape[1]
    d_b = pscale.shape[1]
    d_g = d_b // len(POOL_WINDOWS)
    assert n_in == 2 * d_a + d_b and seq % tm == 0 and tm % GMLP_BLOCK == 0
    vmem = (4 * tm * d * 4
            + w_in.size * 2 + w_out.size * 2 + w_pool.size * 2 + (w_s.size + b_s_b.size) * 4
            + tm * d * 2 + tm * d_a * 4 + tm * d_a * 2 + (tm + POOL_HALO) * d_b * 4 + tm * (d_a + d_b) * 2
            + 4 * tm * d_a * 4)
    return pl.pallas_call(
        functools.partial(_l0_mixer_kernel, tm=tm, tiles_per_seq=seq // tm, d_a=d_a, d_g=d_g),
        grid=(t // tm,),
        in_specs=[pl.BlockSpec((tm, d), lambda i: (i, 0)),
                  _const_spec(gmix.shape), _const_spec(w_in.shape), _const_spec(g_v.shape),
                  _const_spec(w_s.shape), _const_spec(b_s_b.shape), _const_spec(w_pool.shape),
                  _const_spec(pscale.shape), _const_spec(w_out.shape)],
        out_specs=pl.BlockSpec((tm, d), lambda i: (i, 0)),
        out_shape=jax.ShapeDtypeStruct((t, d), F32),
        scratch_shapes=[pltpu.VMEM((tm, d), BF16),
                        pltpu.VMEM((tm, d_a), F32),
                        pltpu.VMEM((tm, d_a), BF16),
                        pltpu.VMEM((tm + POOL_HALO, d_b), F32),
                        pltpu.VMEM((tm, d_a + d_b), BF16)],
        compiler_params=_params(("arbitrary",), vmem),
        name="l0_mixer",
    )(x2d, gmix, w_in, g_v, w_s, b_s_b, w_pool, pscale, w_out)


def _ffn_kernel(*refs, final_norm, emit_bf16, has_init):
    it = iter(refs)
    x_ref, g_ref, wg_ref, wu_ref, wd_ref = next(it), next(it), next(it), next(it), next(it)
    gfin_ref = next(it) if final_norm else None
    if has_init:
        next(it)
    o_ref = next(it)
    bf16_out = [next(it) for _ in range(3)] if emit_bf16 else None
    h_ref = next(it)
    f = pl.program_id(1)

    @pl.when(f == 0)
    def _():
        x = x_ref[...]
        h_ref[...] = _rms(x, g_ref[...]).astype(BF16)
        o_ref[...] = x

    wg = wg_ref[...].astype(BF16)
    wu = wu_ref[...].astype(BF16)
    wd = wd_ref[...].astype(BF16)
    if emit_bf16:
        for dst_ref, w in zip(bf16_out, (wg, wu, wd)):
            dst_ref[...] = w

    h = h_ref[...]
    gate = _dot(h, wg)
    up = _dot(h, wu)
    act = (gate * jax.nn.sigmoid(gate) * up).astype(BF16)
    o_ref[...] += _dot(act, wd)

    if final_norm:
        @pl.when(f == pl.num_programs(1) - 1)
        def _():
            o_ref[...] = _rms(o_ref[...], gfin_ref[...])


def _ffn(x2d, g, w_gate, w_up, w_down, *, tm, tf, layer=None, tile_lo=0, n_tiles=None, out_init=None,
         emit_bf16=False, g_final=None):
    t, d = x2d.shape
    dff = w_gate.shape[-1]
    n_tiles = t // tm - tile_lo if n_tiles is None else n_tiles
    nf = dff // tf
    assert t % tm == 0 and dff % tf == 0 and (not emit_bf16 or n_tiles == 1)
    final_norm = g_final is not None
    wbytes = w_gate.dtype.itemsize

    def wspec(w, blk, imap):
        if w.ndim == 2:
            return pl.BlockSpec(blk, imap)
        return pl.BlockSpec((None,) + blk, lambda i, f: (layer,) + imap(i, f))

    rows = lambda i, f: (i + tile_lo, 0)
    gate_cols = lambda i, f: (0, f)
    down_rows = lambda i, f: (f, 0)
    args = [x2d, g, w_gate, w_up, w_down]
    in_specs = [pl.BlockSpec((tm, d), rows), _const_spec(g.shape),
                wspec(w_gate, (d, tf), gate_cols), wspec(w_up, (d, tf), gate_cols),
                wspec(w_down, (tf, d), down_rows)]
    vmem = 4 * tm * d * 4 + tm * d * 2 + 2 * 3 * d * tf * wbytes + 3 * d * tf * 2 + 4 * tm * tf * 4
    if final_norm:
        args.append(g_final)
        in_specs.append(_const_spec(g_final.shape))
    aliases = {}
    if out_init is not None:
        aliases[len(args)] = 0
        args.append(out_init)
        in_specs.append(pl.BlockSpec(memory_space=pl.ANY))
    out_specs = [pl.BlockSpec((tm, d), rows)]
    out_shape = [jax.ShapeDtypeStruct((t, d), F32)]
    if emit_bf16:
        out_specs += [pl.BlockSpec((d, tf), gate_cols), pl.BlockSpec((d, tf), gate_cols),
                      pl.BlockSpec((tf, d), down_rows)]
        out_shape += [jax.ShapeDtypeStruct((d, dff), BF16)] * 2 + [jax.ShapeDtypeStruct((dff, d), BF16)]
        vmem += 2 * 3 * d * tf * 2
    res = pl.pallas_call(
        functools.partial(_ffn_kernel, final_norm=final_norm, emit_bf16=emit_bf16,
                          has_init=out_init is not None),
        grid=(n_tiles, nf),
        in_specs=in_specs,
        out_specs=out_specs,
        out_shape=out_shape,
        input_output_aliases=aliases,
        scratch_shapes=[pltpu.VMEM((tm, d), BF16)],
        compiler_params=_params(("parallel", "arbitrary"), vmem),
        name="ffn_final" if final_norm else ("ffn_head" if emit_bf16 else "ffn"),
    )(*args)
    return res if emit_bf16 else res[0]


def _mla_proj_kernel(x_ref, gmix_ref, win_ref, gcq_ref, gckv_ref, wuq_tr_ref, wkn_ref, wv_tr_ref,
                     cos_ref, sin_ref, cos_tr_ref, sin_tr_ref,
                     q_tr_ref, kn_ref, kr_ref, v_tr_ref, h_ref, cq_tr_ref, ckv_ref, ckv_tr_ref, *, chunk):
    h_ref[...] = _rms(x_ref[...], gmix_ref[...]).astype(BF16)
    p = _dot(h_ref[...], win_ref[...])
    cq_tr_ref[...] = _rms(p[:, 0:Q_LORA], gcq_ref[...]).T.astype(BF16)
    ckv = _rms(p[:, Q_LORA:Q_LORA + KV_LORA], gckv_ref[...])
    ckv_ref[...] = ckv.astype(BF16)
    ckv_tr_ref[...] = ckv.T.astype(BF16)
    a = p[:, Q_LORA + KV_LORA:]
    kr_ref[...] = (a * cos_ref[...] + pltpu.roll(a, ROPE_DIM, axis=1) * sin_ref[...]).astype(BF16)

    rope_lo, rope_hi = NOPE_DIM, NOPE_DIM + ROPE_DIM
    for hd in range(C_HEADS):
        r0 = hd * QK_PAD
        qh = _dot(wuq_tr_ref[r0:r0 + QK_PAD, :], cq_tr_ref[...])
        q_tr_ref[r0:r0 + rope_lo, :] = qh[0:rope_lo].astype(BF16)
        roped = qh[rope_lo:rope_hi] * cos_tr_ref[...] + qh[rope_hi:] * sin_tr_ref[...]
        q_tr_ref[r0 + rope_lo:r0 + rope_hi, :] = roped.astype(BF16)
        q_tr_ref[r0 + rope_hi:r0 + QK_PAD, :] = jnp.zeros((QK_PAD - rope_hi, qh.shape[1]), BF16)

    heads_per_chunk = chunk // NOPE_DIM
    for c in range(C_HEADS // heads_per_chunk):
        sl = slice(c * chunk, (c + 1) * chunk)
        kn = _dot(ckv_ref[...], wkn_ref[:, sl]).astype(BF16)
        for hh in range(heads_per_chunk):
            kn_ref[c * heads_per_chunk + hh] = kn[:, hh * NOPE_DIM:(hh + 1) * NOPE_DIM]
        v_tr_ref[sl, :] = _dot(wv_tr_ref[sl, :], ckv_tr_ref[...]).astype(BF16)


def _mla_proj(x2d, gmix, w_in_ext, g_cq_s, g_ckv, w_uq_tr, w_kn, w_v_tr, cos_t, sin_t, cos_tr, sin_tr,
              *, seq, tm):
    t, d = x2d.shape
    n_q = w_uq_tr.shape[0]
    n_kv = w_kn.shape[1]
    tiles_per_seq = seq // tm
    assert seq % tm == 0 and w_in_ext.shape[1] == Q_LORA + KV_LORA + 2 * ROPE_DIM
    vmem = (2 * tm * d * 4 + (w_in_ext.size + w_uq_tr.size + w_kn.size + w_v_tr.size) * 2
            + 2 * 2 * tm * (V7X_LANES + ROPE_DIM) * 4
            + 2 * tm * (n_q + 2 * n_kv + V7X_LANES) * 2
            + tm * d * 2 + tm * (Q_LORA + 2 * KV_LORA) * 2
            + 4 * tm * w_in_ext.shape[1] * 4)
    row = lambda i: (i, 0)
    col = lambda i: (0, i)
    return pl.pallas_call(
        functools.partial(_mla_proj_kernel, chunk=512),
        grid=(t // tm,),
        in_specs=[pl.BlockSpec((tm, d), row),
                  _const_spec(gmix.shape), _const_spec(w_in_ext.shape), _const_spec(g_cq_s.shape),
                  _const_spec(g_ckv.shape), _const_spec(w_uq_tr.shape), _const_spec(w_kn.shape),
                  _const_spec(w_v_tr.shape),
                  pl.BlockSpec((tm, V7X_LANES), lambda i: (i % tiles_per_seq, 0)),
                  pl.BlockSpec((tm, V7X_LANES), lambda i: (i % tiles_per_seq, 0)),
                  pl.BlockSpec((ROPE_DIM, tm), lambda i: (0, i % tiles_per_seq)),
                  pl.BlockSpec((ROPE_DIM, tm), lambda i: (0, i % tiles_per_seq))],
        out_specs=[pl.BlockSpec((n_q, tm), col), pl.BlockSpec((C_HEADS, tm, NOPE_DIM), lambda i: (0, i, 0)),
                   pl.BlockSpec((tm, V7X_LANES), row), pl.BlockSpec((n_kv, tm), col)],
        out_shape=[jax.ShapeDtypeStruct((n_q, t), BF16), jax.ShapeDtypeStruct((C_HEADS, t, NOPE_DIM), BF16),
                   jax.ShapeDtypeStruct((t, V7X_LANES), BF16), jax.ShapeDtypeStruct((n_kv, t), BF16)],
        scratch_shapes=[pltpu.VMEM((tm, d), BF16), pltpu.VMEM((Q_LORA, tm), BF16),
                        pltpu.VMEM((tm, KV_LORA), BF16), pltpu.VMEM((KV_LORA, tm), BF16)],
        compiler_params=_params(("parallel",), vmem),
        name="mla_proj",
    )(x2d, gmix, w_in_ext, g_cq_s, g_ckv, w_uq_tr, w_kn, w_v_tr, cos_t, sin_t, cos_tr, sin_tr)


def _attn_kernel(*refs, tk, tq, skew, n_cast):
    q_tr_ref, kn_ref, kr_ref, v_tr_ref = refs[0:4]
    cast_in = refs[4:4 + n_cast]
    o_ref = refs[4 + n_cast]
    cast_out = refs[5 + n_cast:5 + 2 * n_cast]
    kcat_ref, vaug_ref, m_ref, acc_ref = refs[5 + 2 * n_cast:]
    for src_ref, dst_ref in zip(cast_in, cast_out):
        dst_ref[...] = src_ref[...].astype(BF16)
    seq = kcat_ref.shape[0]
    kcat_ref[:, 0:NOPE_DIM] = kn_ref[...]
    kcat_ref[:, NOPE_DIM:] = kr_ref[...]
    vaug_ref[0:V_DIM, :] = v_tr_ref[...]
    vaug_ref[V_DIM:, :] = jnp.ones((vaug_ref.shape[0] - V_DIM, seq), BF16)

    units = []
    for j in range(seq // tk):
        for i in range(seq // tq):
            k_lo, q_lo = j * tk, i * tq
            k_hi = min(k_lo + tk, q_lo + tq)
            if k_hi > k_lo:
                units.append((j == 0, k_lo, k_hi, q_lo))

    def scores(unit):
        _, k_lo, k_hi, q_lo = unit
        s = _dot(kcat_ref[k_lo:k_hi, :], q_tr_ref[:, q_lo:q_lo + tq])
        if k_hi > q_lo + CHUNK:
            kc = (lax.broadcasted_iota(jnp.int32, s.shape, 0) + k_lo) // CHUNK
            qc = (lax.broadcasted_iota(jnp.int32, s.shape, 1) + q_lo) // CHUNK
            s = jnp.where(kc <= qc, s, -jnp.inf)
        return s

    pending = [scores(u) for u in units[:skew]]
    for n, (first, k_lo, k_hi, q_lo) in enumerate(units):
        if n + skew < len(units):
            pending.append(scores(units[n + skew]))
        s = pending.pop(0)
        qs = slice(q_lo, q_lo + tq)
        v_j = vaug_ref[:, k_lo:k_hi]
        m_blk = jnp.max(s, axis=0, keepdims=True)
        if first:
            m_new = m_blk
            acc_ref[:, qs] = _dot(v_j, jnp.exp2(s - m_new).astype(BF16))
        else:
            m_prev = m_ref[:, qs]
            m_new = jnp.maximum(m_prev, m_blk)
            alpha = jnp.exp2(m_prev - m_new)
            acc_ref[:, qs] = alpha * acc_ref[:, qs] + _dot(v_j, jnp.exp2(s - m_new).astype(BF16))
        m_ref[:, qs] = m_new

    for i in range(seq // tq):
        qs = slice(i * tq, (i + 1) * tq)
        o_ref[qs, :] = (acc_ref[0:V_DIM, qs] / acc_ref[V_DIM:V_DIM + 1, qs]).T.astype(o_ref.dtype)


def _attention(q_tr, kn, kr, v_tr, *, batch, seq, tk, tq, skew, cast_next=(), cast_layer=None):
    t = kn.shape[1]
    steps = batch * C_HEADS
    assert seq % tk == 0 and seq % tq == 0 and tk % CHUNK == 0 and tq % CHUNK == 0
    assert V_DIM == V7X_LANES and NOPE_DIM == V7X_LANES
    vmem = (2 * (QK_PAD + V_DIM) * seq * 2 + 2 * 2 * seq * V7X_LANES * 2 + 2 * seq * V_DIM * 2
            + seq * QK_PAD * 2 + 2 * 8 * seq * 4 + V_DIM * seq * 4 + 24 * tk * tq * 4)
    in_specs = [pl.BlockSpec((QK_PAD, seq), lambda b, h: (h, b)),
                pl.BlockSpec((None, seq, NOPE_DIM), lambda b, h: (h, b, 0)),
                pl.BlockSpec((seq, V7X_LANES), lambda b, h: (b, 0)),
                pl.BlockSpec((V_DIM, seq), lambda b, h: (h, b))]
    out_specs = [pl.BlockSpec((None, seq, V_DIM), lambda b, h: (h, b, 0))]
    out_shape = [jax.ShapeDtypeStruct((C_HEADS, t, V_DIM), BF16)]
    for w in cast_next:
        _, r, c = w.shape
        assert r % (steps * 16) == 0
        in_specs.append(pl.BlockSpec((None, r // steps, c), lambda b, h: (cast_layer, b * C_HEADS + h, 0)))
        out_specs.append(pl.BlockSpec((r // steps, c), lambda b, h: (b * C_HEADS + h, 0)))
        out_shape.append(jax.ShapeDtypeStruct((r, c), BF16))
        vmem += 2 * (r // steps) * c * 6
    res = pl.pallas_call(
        functools.partial(_attn_kernel, tk=tk, tq=tq, skew=skew, n_cast=len(cast_next)),
        grid=(batch, C_HEADS),
        in_specs=in_specs,
        out_specs=out_specs,
        out_shape=out_shape,
        scratch_shapes=[pltpu.VMEM((seq, QK_PAD), BF16),
                        pltpu.VMEM((V_DIM + ONES_ROWS, seq), BF16),
                        pltpu.VMEM((1, seq), F32),
                        pltpu.VMEM((V_DIM + ONES_ROWS, seq), F32)],
        compiler_params=_params(("parallel", "parallel"), vmem),
        name="attention",
    )(q_tr, kn, kr, v_tr, *cast_next)
    return res if cast_next else res[0]


def _proj_residual_kernel(x_ref, a_ref, w_ref, o_ref):
    a = jnp.concatenate([a_ref[hd] for hd in range(a_ref.shape[0])], axis=1)
    o_ref[...] = x_ref[...] + _dot(a, w_ref[...])


def _proj_residual(x2d, a, w, *, tm):
    t, d = x2d.shape
    heads, _, hd_dim = a.shape
    k = heads * hd_dim
    vmem = 4 * tm * d * 4 + 2 * tm * k * 2 + w.size * 2 + 2 * tm * d * 4
    return pl.pallas_call(
        _proj_residual_kernel,
        grid=(t // tm,),
        in_specs=[pl.BlockSpec((tm, d), lambda i: (i, 0)), pl.BlockSpec((heads, tm, hd_dim), lambda i: (0, i, 0)),
                  _const_spec(w.shape)],
        out_specs=pl.BlockSpec((tm, d), lambda i: (i, 0)),
        out_shape=jax.ShapeDtypeStruct((t, d), F32),
        compiler_params=_params(("parallel",), vmem),
        name="attn_out_proj",
    )(x2d, a, w)


def _swap_halves(w):
    half = w.shape[-1] // 2
    return jnp.concatenate([w[..., half:], w[..., :half]], axis=-1)


def kernel(x, g_mix, g_ffn, g_final, w_in_ab, g_v, w_s, b_s, w_pool, pool_scale, w_out_ab, w_in_c, g_cq, g_ckv, w_uq, w_ukv, w_out_c, w_gate, w_up, w_down):
    batch, seq, d = x.shape
    x2d = x.reshape(batch * seq, d)
    row = lambda a: a.reshape(1, -1)

    b_s_b = jnp.broadcast_to(b_s[0][:, :, None], (A_HEADS, GMLP_BLOCK, GMLP_BLOCK))
    x1 = _l0_mixer(x2d, row(g_mix[0]), w_in_ab[0].astype(BF16), row(g_v[0]), w_s[0], b_s_b,
                   w_pool[0].astype(BF16), row(pool_scale[0]), w_out_ab[0].astype(BF16),
                   seq=seq, tm=512)
    x2_head, wg0, wu0, wd0 = _ffn(x1, row(g_ffn[0]), w_gate, w_up, w_down, layer=0, tm=1024, tf=256,
                                  n_tiles=1, emit_bf16=True)
    x2 = _ffn(x1, row(g_ffn[0]), wg0, wu0, wd0, tm=1024, tf=512, tile_lo=1, out_init=x2_head)

    inv_freq = ROPE_THETA ** (-jnp.arange(0, ROPE_DIM, 2, dtype=F32) / ROPE_DIM)
    cos_t, sin_t, cos_tr, sin_tr = _rope_tables(
        jnp.tile(inv_freq, V7X_LANES // inv_freq.shape[0]).reshape(1, -1), seq, 512)
    wc = w_in_c[0]
    w_kr = wc[:, Q_LORA + KV_LORA:]
    w_in_ext = jnp.concatenate([wc, _swap_halves(w_kr)], axis=1).astype(BF16)
    wq = w_uq[0].reshape(Q_LORA, C_HEADS, NOPE_DIM + ROPE_DIM)
    w_uq_ext = jnp.concatenate([wq, _swap_halves(wq[..., NOPE_DIM:])], axis=-1)
    w_uq_tr = w_uq_ext.reshape(Q_LORA, C_HEADS * QK_PAD).T.astype(BF16)
    wkv = w_ukv[0].reshape(KV_LORA, C_HEADS, NOPE_DIM + V_DIM)
    w_kn = wkv[..., :NOPE_DIM].reshape(KV_LORA, -1).astype(BF16)
    w_v_tr = wkv[..., NOPE_DIM:].reshape(KV_LORA, -1).T.astype(BF16)
    score_scale = (NOPE_DIM + ROPE_DIM) ** -0.5 * 1.4426950408889634
    q_tr, kn, kr, v_tr = _mla_proj(x2, row(g_mix[1]), w_in_ext, row(g_cq[0]) * score_scale,
                                   row(g_ckv[0]), w_uq_tr, w_kn, w_v_tr, cos_t, sin_t, cos_tr, sin_tr,
                                   seq=seq, tm=512)
    o, w_gate1, w_up1, w_down1 = _attention(q_tr, kn, kr, v_tr, batch=batch, seq=seq, tk=512, tq=256, skew=3,
                                            cast_next=(w_gate, w_up, w_down), cast_layer=1)
    x3 = _proj_residual(x2, o, w_out_c[0].astype(BF16), tm=512)
    out = _ffn(x3, row(g_ffn[1]), w_gate1, w_up1, w_down1, tm=1024, tf=512, g_final=row(g_final))
    return out.reshape(batch, seq, d)
```

```python
import functools

import jax
import jax.numpy as jnp
from jax import lax
from jax.experimental import pallas as pl
from jax.experimental.pallas import tpu as pltpu

F32 = jnp.float32
BF16 = jnp.bfloat16

CHUNK = 64
EPS = 1e-6
GMLP_BLOCK = 128
A_HEADS = 8
POOL_WINDOWS = (2, 4, 8, 16)
POOL_HALO = 16
C_HEADS = 16
Q_LORA = 512
KV_LORA = 512
NOPE_DIM = 128
ROPE_DIM = 64
V_DIM = 128
ROPE_THETA = 10000.0
QK_PAD = 256
ONES_ROWS = 16

V7X_LANES = 128
V7X_SCOPED_VMEM_BYTES = 60000 * 1024


def _rms(xf, g):
    ms = jnp.mean(xf * xf, axis=-1, keepdims=True)
    return xf * lax.rsqrt(ms + EPS) * g


def _gelu_tanh(x):
    c = 0.7978845608028654
    return 0.5 * x * (1.0 + jnp.tanh(c * (x + 0.044715 * (x * x * x))))


def _dot(a, b):
    return jnp.dot(a, b, preferred_element_type=F32)


def _const_spec(shape):
    nd = len(shape)
    return pl.BlockSpec(shape, lambda *_: (0,) * nd, pipeline_mode=pl.Buffered(1))


def _params(semantics, vmem_bytes):
    return pltpu.CompilerParams(dimension_semantics=semantics,
                                vmem_limit_bytes=min(int(vmem_bytes), V7X_SCOPED_VMEM_BYTES))


def _rope_table_kernel(invf_ref, cos_ref, sin_ref, cos_tr_ref, sin_tr_ref, *, ts):
    i = pl.program_id(0)
    pos = (lax.broadcasted_iota(jnp.int32, (ts, V7X_LANES), 0) + i * ts).astype(F32)
    lane = lax.broadcasted_iota(jnp.int32, (ts, V7X_LANES), 1)
    ang = pos * invf_ref[...]
    c = jnp.cos(ang)
    s = jnp.sin(ang)
    half = ROPE_DIM // 2
    cos_t = jnp.where(lane < ROPE_DIM, c, 0.0)
    sin_t = jnp.where(lane < half, -s, jnp.where(lane < ROPE_DIM, s, 0.0))
    cos_ref[...] = cos_t
    sin_ref[...] = sin_t
    cos_tr_ref[...] = cos_t.T[0:ROPE_DIM]
    sin_tr_ref[...] = sin_t.T[0:ROPE_DIM]


def _rope_tables(invf_row, seq, ts):
    return pl.pallas_call(
        functools.partial(_rope_table_kernel, ts=ts),
        grid=(seq // ts,),
        in_specs=[pl.BlockSpec((1, V7X_LANES), lambda i: (0, 0))],
        out_specs=[pl.BlockSpec((ts, V7X_LANES), lambda i: (i, 0)),
                   pl.BlockSpec((ts, V7X_LANES), lambda i: (i, 0)),
                   pl.BlockSpec((ROPE_DIM, ts), lambda i: (0, i)),
                   pl.BlockSpec((ROPE_DIM, ts), lambda i: (0, i))],
        out_shape=[jax.ShapeDtypeStruct((seq, V7X_LANES), F32)] * 2
                  + [jax.ShapeDtypeStruct((ROPE_DIM, seq), F32)] * 2,
        compiler_params=_params(("parallel",), 16 * ts * V7X_LANES * 4 + (4 << 20)),
        name="rope_tables",
    )(invf_row)


def _l0_mixer_kernel(x_ref, gmix_ref, win_ref, gv_ref, ws_ref, bs_ref, wpool_ref, psc_ref, wout_ref,
                     o_ref, h_ref, u_ref, v_ref, z_ref, cat_ref, *, tm, tiles_per_seq, d_a, d_g):
    i = pl.program_id(0)
    seq_tile = i % tiles_per_seq
    nb = tm // GMLP_BLOCK

    h_ref[...] = _rms(x_ref[...], gmix_ref[...]).astype(BF16)

    u_ref[...] = _gelu_tanh(_dot(h_ref[...], win_ref[:, 0:d_a]))
    v_ref[...] = _rms(_gelu_tanh(_dot(h_ref[...], win_ref[:, d_a:2 * d_a])), gv_ref[...]).astype(BF16)

    tt = lax.broadcasted_iota(jnp.int32, (GMLP_BLOCK, GMLP_BLOCK), 0)
    ss = lax.broadcasted_iota(jnp.int32, (GMLP_BLOCK, GMLP_BLOCK), 1)
    tri = (ss // CHUNK <= tt // CHUNK).astype(F32)
    hd_w = d_a // A_HEADS
    for hd in range(A_HEADS):
        c0 = hd * hd_w
        wm = (ws_ref[hd] * tri).astype(BF16)
        rhs = jnp.concatenate(
            [v_ref[n * GMLP_BLOCK:(n + 1) * GMLP_BLOCK, c0:c0 + hd_w] for n in range(nb)], axis=1)
        mix = _dot(wm, rhs)
        bias = bs_ref[hd]
        for n in range(nb):
            r0 = n * GMLP_BLOCK
            a = u_ref[r0:r0 + GMLP_BLOCK, c0:c0 + hd_w] * (mix[:, n * hd_w:(n + 1) * hd_w] + bias)
            cat_ref[r0:r0 + GMLP_BLOCK, c0:c0 + hd_w] = a.astype(BF16)

    @pl.when(seq_tile == 0)
    def _():
        z_ref[0:POOL_HALO, :] = jnp.zeros((POOL_HALO, z_ref.shape[1]), F32)

    z_ref[POOL_HALO:POOL_HALO + tm, :] = _dot(h_ref[...], win_ref[:, 2 * d_a:])
    pos1 = lax.broadcasted_iota(jnp.int32, (tm, d_g), 0) + (seq_tile * tm + 1)
    for g, win in enumerate(POOL_WINDOWS):
        zz = z_ref[:, g * d_g:(g + 1) * d_g]
        s = zz
        k = 1
        while k < win:
            s = s + pltpu.roll(s, k, axis=0)
            k *= 2
        cnt = jnp.minimum(pos1, win).astype(F32)
        d = (s[POOL_HALO:] / cnt - zz[POOL_HALO:]).astype(BF16)
        y = _dot(d, wpool_ref[g]) * psc_ref[:, g * d_g:(g + 1) * d_g]
        cat_ref[:, d_a + g * d_g:d_a + (g + 1) * d_g] = y.astype(BF16)
    z_ref[0:POOL_HALO, :] = z_ref[tm:tm + POOL_HALO, :]

    o_ref[...] = x_ref[...] + _dot(cat_ref[...], wout_ref[...])


def _l0_mixer(x2d, gmix, w_in, g_v, w_s, b_s_b, w_pool, pscale, w_out, *, seq, tm):
    t, d = x2d.shape
    n_in = w_in.shape[1]
    d_a = g_v.shape[1]
    d_b = pscale.shape[1]
    d_g = d_b // len(POOL_WINDOWS)
    assert n_in == 2 * d_a + d_b and seq % tm == 0 and tm % GMLP_BLOCK == 0
    vmem = (4 * tm * d * 4
            + w_in.size * 2 + w_out.size * 2 + w_pool.size * 2 + (w_s.size + b_s_b.size) * 4
            + tm * d * 2 + tm * d_a * 4 + tm * d_a * 2 + (tm + POOL_HALO) * d_b * 4 + tm * (d_a + d_b) * 2
            + 4 * tm * d_a * 4)
    return pl.pallas_call(
        functools.partial(_l0_mixer_kernel, tm=tm, tiles_per_seq=seq // tm, d_a=d_a, d_g=d_g),
        grid=(t // tm,),
        in_specs=[pl.BlockSpec((tm, d), lambda i: (i, 0)),
                  _const_spec(gmix.shape), _const_spec(w_in.shape), _const_spec(g_v.shape),
                  _const_spec(w_s.shape), _const_spec(b_s_b.shape), _const_spec(w_pool.shape),
                  _const_spec(pscale.shape), _const_spec(w_out.shape)],
        out_specs=pl.BlockSpec((tm, d), lambda i: (i, 0)),
        out_shape=jax.ShapeDtypeStruct((t, d), F32),
        scratch_shapes=[pltpu.VMEM((tm, d), BF16),
                        pltpu.VMEM((tm, d_a), F32),
                        pltpu.VMEM((tm, d_a), BF16),
                        pltpu.VMEM((tm + POOL_HALO, d_b), F32),
                        pltpu.VMEM((tm, d_a + d_b), BF16)],
        compiler_params=_params(("arbitrary",), vmem),
        name="l0_mixer",
    )(x2d, gmix, w_in, g_v, w_s, b_s_b, w_pool, pscale, w_out)


def _ffn_kernel(*refs, final_norm, emit_bf16, has_init):
    it = iter(refs)
    x_ref, g_ref, wg_ref, wu_ref, wd_ref = next(it), next(it), next(it), next(it), next(it)
    gfin_ref = next(it) if final_norm else None
    if has_init:
        next(it)
    o_ref = next(it)
    bf16_out = [next(it) for _ in range(3)] if emit_bf16 else None
    h_ref = next(it)
    f = pl.program_id(1)

    @pl.when(f == 0)
    def _():
        x = x_ref[...]
        h_ref[...] = _rms(x, g_ref[...]).astype(BF16)
        o_ref[...] = x

    wg = wg_ref[...].astype(BF16)
    wu = wu_ref[...].astype(BF16)
    wd = wd_ref[...].astype(BF16)
    if emit_bf16:
        for dst_ref, w in zip(bf16_out, (wg, wu, wd)):
            dst_ref[...] = w

    h = h_ref[...]
    gate = _dot(h, wg)
    up = _dot(h, wu)
    act = (gate * jax.nn.sigmoid(gate) * up).astype(BF16)
    o_ref[...] += _dot(act, wd)

    if final_norm:
        @pl.when(f == pl.num_programs(1) - 1)
        def _():
            o_ref[...] = _rms(o_ref[...], gfin_ref[...])


def _ffn(x2d, g, w_gate, w_up, w_down, *, tm, tf, layer=None, tile_lo=0, n_tiles=None, out_init=None,
         emit_bf16=False, g_final=None):
    t, d = x2d.shape
    dff = w_gate.shape[-1]
    n_tiles = t // tm - tile_lo if n_tiles is None else n_tiles
    nf = dff // tf
    assert t % tm == 0 and dff % tf == 0 and (not emit_bf16 or n_tiles == 1)
    final_norm = g_final is not None
    wbytes = w_gate.dtype.itemsize

    def wspec(w, blk, imap):
        if w.ndim == 2:
            return pl.BlockSpec(blk, imap)
        return pl.BlockSpec((None,) + blk, lambda i, f: (layer,) + imap(i, f))

    rows = lambda i, f: (i + tile_lo, 0)
    gate_cols = lambda i, f: (0, f)
    down_rows = lambda i, f: (f, 0)
    args = [x2d, g, w_gate, w_up, w_down]
    in_specs = [pl.BlockSpec((tm, d), rows), _const_spec(g.shape),
                wspec(w_gate, (d, tf), gate_cols), wspec(w_up, (d, tf), gate_cols),
                wspec(w_down, (tf, d), down_rows)]
    vmem = 4 * tm * d * 4 + tm * d * 2 + 2 * 3 * d * tf * wbytes + 3 * d * tf * 2 + 4 * tm * tf * 4
    if final_norm:
        args.append(g_final)
        in_specs.append(_const_spec(g_final.shape))
    aliases = {}
    if out_init is not None:
        aliases[len(args)] = 0
        args.append(out_init)
        in_specs.append(pl.BlockSpec(memory_space=pl.ANY))
    out_specs = [pl.BlockSpec((tm, d), rows)]
    out_shape = [jax.ShapeDtypeStruct((t, d), F32)]
    if emit_bf16:
        out_specs += [pl.BlockSpec((d, tf), gate_cols), pl.BlockSpec((d, tf), gate_cols),
                      pl.BlockSpec((tf, d), down_rows)]
        out_shape += [jax.ShapeDtypeStruct((d, dff), BF16)] * 2 + [jax.ShapeDtypeStruct((dff, d), BF16)]
        vmem += 2 * 3 * d * tf * 2
    res = pl.pallas_call(
        functools.partial(_ffn_kernel, final_norm=final_norm, emit_bf16=emit_bf16,
                          has_init=out_init is not None),
        grid=(n_tiles, nf),
        in_specs=in_specs,
        out_specs=out_specs,
        out_shape=out_shape,
        input_output_aliases=aliases,
        scratch_shapes=[pltpu.VMEM((tm, d), BF16)],
        compiler_params=_params(("parallel", "arbitrary"), vmem),
        name="ffn_final" if final_norm else ("ffn_head" if emit_bf16 else "ffn"),
    )(*args)
    return res if emit_bf16 else res[0]


def _mla_proj_kernel(x_ref, gmix_ref, win_ref, gcq_ref, gckv_ref, wuq_tr_ref, wkn_ref, wv_tr_ref,
                     cos_ref, sin_ref, cos_tr_ref, sin_tr_ref,
                     q_tr_ref, k_ref, v_tr_ref, h_ref, cq_tr_ref, ckv_ref, ckv_tr_ref, *, chunk):
    h_ref[...] = _rms(x_ref[...], gmix_ref[...]).astype(BF16)
    p = _dot(h_ref[...], win_ref[...])
    cq_tr_ref[...] = _rms(p[:, 0:Q_LORA], gcq_ref[...]).T.astype(BF16)
    ckv = _rms(p[:, Q_LORA:Q_LORA + KV_LORA], gckv_ref[...])
    ckv_ref[...] = ckv.astype(BF16)
    ckv_tr_ref[...] = ckv.T.astype(BF16)
    a = p[:, Q_LORA + KV_LORA:]
    k_rope = (a * cos_ref[...] + pltpu.roll(a, ROPE_DIM, axis=1) * sin_ref[...]).astype(BF16)

    rope_lo, rope_mid, rope_hi = NOPE_DIM, NOPE_DIM + ROPE_DIM // 2, NOPE_DIM + ROPE_DIM
    for hd in range(C_HEADS):
        qh = _dot(wuq_tr_ref[hd * rope_hi:(hd + 1) * rope_hi, :], cq_tr_ref[...])
        r0 = hd * QK_PAD
        q_tr_ref[r0:r0 + rope_lo, :] = qh[0:rope_lo].astype(BF16)
        swapped = jnp.concatenate([qh[rope_mid:rope_hi], qh[rope_lo:rope_mid]], axis=0)
        roped = qh[rope_lo:rope_hi] * cos_tr_ref[...] + swapped * sin_tr_ref[...]
        q_tr_ref[r0 + rope_lo:r0 + rope_hi, :] = roped.astype(BF16)
        q_tr_ref[r0 + rope_hi:r0 + QK_PAD, :] = jnp.zeros((QK_PAD - rope_hi, qh.shape[1]), BF16)

    heads_per_chunk = chunk // NOPE_DIM
    for c in range(C_HEADS // heads_per_chunk):
        sl = slice(c * chunk, (c + 1) * chunk)
        kn = _dot(ckv_ref[...], wkn_ref[:, sl]).astype(BF16)
        for hh in range(heads_per_chunk):
            hd = c * heads_per_chunk + hh
            k_ref[hd, :, 0:NOPE_DIM] = kn[:, hh * NOPE_DIM:(hh + 1) * NOPE_DIM]
            k_ref[hd, :, NOPE_DIM:] = k_rope
        v_tr_ref[sl, :] = _dot(wv_tr_ref[sl, :], ckv_tr_ref[...]).astype(BF16)


def _mla_proj(x2d, gmix, w_in_ext, g_cq_s, g_ckv, w_uq_tr, w_kn, w_v_tr, cos_t, sin_t, cos_tr, sin_tr,
              *, seq, tm):
    t, d = x2d.shape
    n_q = C_HEADS * QK_PAD
    assert w_uq_tr.shape[0] == C_HEADS * (NOPE_DIM + ROPE_DIM)
    n_kv = w_kn.shape[1]
    tiles_per_seq = seq // tm
    assert seq % tm == 0 and w_in_ext.shape[1] == Q_LORA + KV_LORA + 2 * ROPE_DIM
    vmem = (2 * tm * d * 4 + (w_in_ext.size + w_uq_tr.size + w_kn.size + w_v_tr.size) * 2
            + 2 * 2 * tm * (V7X_LANES + ROPE_DIM) * 4
            + 2 * tm * (n_q + C_HEADS * QK_PAD + n_kv) * 2
            + tm * d * 2 + tm * (Q_LORA + 2 * KV_LORA) * 2
            + 4 * tm * w_in_ext.shape[1] * 4)
    row = lambda i: (i, 0)
    col = lambda i: (0, i)
    return pl.pallas_call(
        functools.partial(_mla_proj_kernel, chunk=512),
        grid=(t // tm,),
        in_specs=[pl.BlockSpec((tm, d), row),
                  _const_spec(gmix.shape), _const_spec(w_in_ext.shape), _const_spec(g_cq_s.shape),
                  _const_spec(g_ckv.shape), _const_spec(w_uq_tr.shape), _const_spec(w_kn.shape),
                  _const_spec(w_v_tr.shape),
                  pl.BlockSpec((tm, V7X_LANES), lambda i: (i % tiles_per_seq, 0)),
                  pl.BlockSpec((tm, V7X_LANES), lambda i: (i % tiles_per_seq, 0)),
                  pl.BlockSpec((ROPE_DIM, tm), lambda i: (0, i % tiles_per_seq)),
                  pl.BlockSpec((ROPE_DIM, tm), lambda i: (0, i % tiles_per_seq))],
        out_specs=[pl.BlockSpec((n_q, tm), col), pl.BlockSpec((C_HEADS, tm, QK_PAD), lambda i: (0, i, 0)),
                   pl.BlockSpec((n_kv, tm), col)],
        out_shape=[jax.ShapeDtypeStruct((n_q, t), BF16), jax.ShapeDtypeStruct((C_HEADS, t, QK_PAD), BF16),
                   jax.ShapeDtypeStruct((n_kv, t), BF16)],
        scratch_shapes=[pltpu.VMEM((tm, d), BF16), pltpu.VMEM((Q_LORA, tm), BF16),
                        pltpu.VMEM((tm, KV_LORA), BF16), pltpu.VMEM((KV_LORA, tm), BF16)],
        compiler_params=_params(("parallel",), vmem),
        name="mla_proj",
    )(x2d, gmix, w_in_ext, g_cq_s, g_ckv, w_uq_tr, w_kn, w_v_tr, cos_t, sin_t, cos_tr, sin_tr)


def _attn_kernel(*refs, tk, tq, skew, n_cast):
    q_tr_ref, k_ref, v_tr_ref = refs[0:3]
    cast_in = refs[3:3 + n_cast]
    o_ref = refs[3 + n_cast]
    cast_out = refs[4 + n_cast:4 + 2 * n_cast]
    vaug_ref, m_ref, acc_ref = refs[4 + 2 * n_cast:]
    for src_ref, dst_ref in zip(cast_in, cast_out):
        dst_ref[...] = src_ref[...].astype(BF16)
    seq = k_ref.shape[0]
    vaug_ref[0:V_DIM, :] = v_tr_ref[...]
    vaug_ref[V_DIM:, :] = jnp.ones((vaug_ref.shape[0] - V_DIM, seq), BF16)

    units = []
    for j in range(seq // tk):
        for i in range(seq // tq):
            k_lo, q_lo = j * tk, i * tq
            k_hi = min(k_lo + tk, q_lo + tq)
            if k_hi > k_lo:
                units.append((j == 0, k_lo, k_hi, q_lo))

    def scores(unit):
        _, k_lo, k_hi, q_lo = unit
        s = _dot(k_ref[k_lo:k_hi, :], q_tr_ref[:, q_lo:q_lo + tq])
        if k_hi > q_lo + CHUNK:
            kc = (lax.broadcasted_iota(jnp.int32, s.shape, 0) + k_lo) // CHUNK
            qc = (lax.broadcasted_iota(jnp.int32, s.shape, 1) + q_lo) // CHUNK
            s = jnp.where(kc <= qc, s, -jnp.inf)
        return s

    pending = [scores(u) for u in units[:skew]]
    for n, (first, k_lo, k_hi, q_lo) in enumerate(units):
        if n + skew < len(units):
            pending.append(scores(units[n + skew]))
        s = pending.pop(0)
        qs = slice(q_lo, q_lo + tq)
        v_j = vaug_ref[:, k_lo:k_hi]
        m_blk = jnp.max(s, axis=0, keepdims=True)
        if first:
            m_new = m_blk
            acc_ref[:, qs] = _dot(v_j, jnp.exp2(s - m_new).astype(BF16))
        else:
            m_prev = m_ref[:, qs]
            m_new = jnp.maximum(m_prev, m_blk)
            alpha = jnp.exp2(m_prev - m_new)
            acc_ref[:, qs] = alpha * acc_ref[:, qs] + _dot(v_j, jnp.exp2(s - m_new).astype(BF16))
        m_ref[:, qs] = m_new

    for i in range(seq // tq):
        qs = slice(i * tq, (i + 1) * tq)
        o_ref[qs, :] = (acc_ref[0:V_DIM, qs] / acc_ref[V_DIM:V_DIM + 1, qs]).T.astype(o_ref.dtype)


def _attention(q_tr, k, v_tr, *, batch, seq, tk, tq, skew, cast_next=(), cast_layer=None):
    t = k.shape[1]
    steps = batch * C_HEADS
    assert seq % tk == 0 and seq % tq == 0 and tk % CHUNK == 0 and tq % CHUNK == 0
    assert V_DIM == V7X_LANES and NOPE_DIM == V7X_LANES
    vmem = (2 * (2 * QK_PAD + V_DIM) * seq * 2 + 2 * seq * V_DIM * 2
            + (V_DIM + ONES_ROWS) * seq * 6 + 8 * seq * 4 + 24 * tk * tq * 4)
    in_specs = [pl.BlockSpec((QK_PAD, seq), lambda b, h: (h, b)),
                pl.BlockSpec((None, seq, QK_PAD), lambda b, h: (h, b, 0)),
                pl.BlockSpec((V_DIM, seq), lambda b, h: (h, b))]
    out_specs = [pl.BlockSpec((None, seq, V_DIM), lambda b, h: (h, b, 0))]
    out_shape = [jax.ShapeDtypeStruct((C_HEADS, t, V_DIM), BF16)]
    for w in cast_next:
        _, r, c = w.shape
        assert r % (steps * 16) == 0
        in_specs.append(pl.BlockSpec((None, r // steps, c), lambda b, h: (cast_layer, b * C_HEADS + h, 0)))
        out_specs.append(pl.BlockSpec((r // steps, c), lambda b, h: (b * C_HEADS + h, 0)))
        out_shape.append(jax.ShapeDtypeStruct((r, c), BF16))
        vmem += 2 * (r // steps) * c * 6
    res = pl.pallas_call(
        functools.partial(_attn_kernel, tk=tk, tq=tq, skew=skew, n_cast=len(cast_next)),
        grid=(batch, C_HEADS),
        in_specs=in_specs,
        out_specs=out_specs,
        out_shape=out_shape,
        scratch_shapes=[pltpu.VMEM((V_DIM + ONES_ROWS, seq), BF16),
                        pltpu.VMEM((1, seq), F32),
                        pltpu.VMEM((V_DIM + ONES_ROWS, seq), F32)],
        compiler_params=_params(("parallel", "parallel"), vmem),
        name="attention",
    )(q_tr, k, v_tr, *cast_next)
    return res if cast_next else res[0]


def _proj_residual_kernel(x_ref, a_ref, w_ref, o_ref):
    a = jnp.concatenate([a_ref[hd] for hd in range(a_ref.shape[0])], axis=1)
    o_ref[...] = x_ref[...] + _dot(a, w_ref[...])


def _proj_residual(x2d, a, w, *, tm):
    t, d = x2d.shape
    heads, _, hd_dim = a.shape
    k = heads * hd_dim
    vmem = 4 * tm * d * 4 + 2 * tm * k * 2 + w.size * 2 + 2 * tm * d * 4
    return pl.pallas_call(
        _proj_residual_kernel,
        grid=(t // tm,),
        in_specs=[pl.BlockSpec((tm, d), lambda i: (i, 0)), pl.BlockSpec((heads, tm, hd_dim), lambda i: (0, i, 0)),
                  _const_spec(w.shape)],
        out_specs=pl.BlockSpec((tm, d), lambda i: (i, 0)),
        out_shape=jax.ShapeDtypeStruct((t, d), F32),
        compiler_params=_params(("parallel",), vmem),
        name="attn_out_proj",
    )(x2d, a, w)


def _swap_halves(w):
    half = w.shape[-1] // 2
    return jnp.concatenate([w[..., half:], w[..., :half]], axis=-1)


def kernel(x, g_mix, g_ffn, g_final, w_in_ab, g_v, w_s, b_s, w_pool, pool_scale, w_out_ab, w_in_c, g_cq, g_ckv, w_uq, w_ukv, w_out_c, w_gate, w_up, w_down):
    batch, seq, d = x.shape
    x2d = x.reshape(batch * seq, d)
    row = lambda a: a.reshape(1, -1)

    b_s_b = jnp.broadcast_to(b_s[0][:, :, None], (A_HEADS, GMLP_BLOCK, GMLP_BLOCK))
    x1 = _l0_mixer(x2d, row(g_mix[0]), w_in_ab[0].astype(BF16), row(g_v[0]), w_s[0], b_s_b,
                   w_pool[0].astype(BF16), row(pool_scale[0]), w_out_ab[0].astype(BF16),
                   seq=seq, tm=512)
    x2_head, wg0, wu0, wd0 = _ffn(x1, row(g_ffn[0]), w_gate, w_up, w_down, layer=0, tm=1024, tf=256,
                                  n_tiles=1, emit_bf16=True)
    x2 = _ffn(x1, row(g_ffn[0]), wg0, wu0, wd0, tm=1024, tf=512, tile_lo=1, out_init=x2_head)

    inv_freq = ROPE_THETA ** (-jnp.arange(0, ROPE_DIM, 2, dtype=F32) / ROPE_DIM)
    cos_t, sin_t, cos_tr, sin_tr = _rope_tables(
        jnp.tile(inv_freq, V7X_LANES // inv_freq.shape[0]).reshape(1, -1), seq, 512)
    wc = w_in_c[0]
    w_kr = wc[:, Q_LORA + KV_LORA:]
    w_in_ext = jnp.concatenate([wc, _swap_halves(w_kr)], axis=1).astype(BF16)
    w_uq_tr = w_uq[0].T.astype(BF16)
    wkv = w_ukv[0].reshape(KV_LORA, C_HEADS, NOPE_DIM + V_DIM)
    w_kn = wkv[..., :NOPE_DIM].reshape(KV_LORA, -1).astype(BF16)
    w_v_tr = wkv[..., NOPE_DIM:].reshape(KV_LORA, -1).T.astype(BF16)
    score_scale = (NOPE_DIM + ROPE_DIM) ** -0.5 * 1.4426950408889634
    q_tr, k, v_tr = _mla_proj(x2, row(g_mix[1]), w_in_ext, row(g_cq[0]) * score_scale,
                              row(g_ckv[0]), w_uq_tr, w_kn, w_v_tr, cos_t, sin_t, cos_tr, sin_tr,
                              seq=seq, tm=512)
    o, w_gate1, w_up1, w_down1 = _attention(q_tr, k, v_tr, batch=batch, seq=seq, tk=512, tq=256, skew=3,
                                            cast_next=(w_gate, w_up, w_down), cast_layer=1)
    x3 = _proj_residual(x2, o, w_out_c[0].astype(BF16), tm=512)
    out = _ffn(x3, row(g_ffn[1]), w_gate1, w_up1, w_down1, tm=1024, tf=512, g_final=row(g_final))
    return out.reshape(batch, seq, d)
```

```python
import functools

import jax
import jax.numpy as jnp
from jax import lax
from jax.experimental import pallas as pl
from jax.experimental.pallas import tpu as pltpu

F32 = jnp.float32
BF16 = jnp.bfloat16

CHUNK = 64
EPS = 1e-6
GMLP_BLOCK = 128
A_HEADS = 8
POOL_WINDOWS = (2, 4, 8, 16)
POOL_HALO = 16
C_HEADS = 16
Q_LORA = 512
KV_LORA = 512
NOPE_DIM = 128
ROPE_DIM = 64
V_DIM = 128
ROPE_THETA = 10000.0
QK_PAD = 256
ONES_ROWS = 16

V7X_LANES = 128
V7X_SCOPED_VMEM_BYTES = 60000 * 1024


def _rms(xf, g):
    ms = jnp.mean(xf * xf, axis=-1, keepdims=True)
    return xf * lax.rsqrt(ms + EPS) * g


def _gelu_tanh(x):
    c = 0.7978845608028654
    return 0.5 * x * (1.0 + jnp.tanh(c * (x + 0.044715 * (x * x * x))))


def _dot(a, b):
    return jnp.dot(a, b, preferred_element_type=F32)


def _const_spec(shape):
    nd = len(shape)
    return pl.BlockSpec(shape, lambda *_: (0,) * nd, pipeline_mode=pl.Buffered(1))


def _params(semantics, vmem_bytes):
    return pltpu.CompilerParams(dimension_semantics=semantics,
                                vmem_limit_bytes=min(int(vmem_bytes), V7X_SCOPED_VMEM_BYTES))


def _rope_table_kernel(invf_ref, cos_ref, sin_ref, cos_tr_ref, sin_tr_ref, *, ts):
    i = pl.program_id(0)
    pos = (lax.broadcasted_iota(jnp.int32, (ts, V7X_LANES), 0) + i * ts).astype(F32)
    lane = lax.broadcasted_iota(jnp.int32, (ts, V7X_LANES), 1)
    ang = pos * invf_ref[...]
    c = jnp.cos(ang)
    s = jnp.sin(ang)
    half = ROPE_DIM // 2
    cos_t = jnp.where(lane < ROPE_DIM, c, 0.0)
    sin_t = jnp.where(lane < half, -s, jnp.where(lane < ROPE_DIM, s, 0.0))
    cos_ref[...] = cos_t
    sin_ref[...] = sin_t
    cos_tr_ref[...] = cos_t.T[0:ROPE_DIM]
    sin_tr_ref[...] = sin_t.T[0:ROPE_DIM]


def _rope_tables(invf_row, seq, ts):
    return pl.pallas_call(
        functools.partial(_rope_table_kernel, ts=ts),
        grid=(seq // ts,),
        in_specs=[pl.BlockSpec((1, V7X_LANES), lambda i: (0, 0))],
        out_specs=[pl.BlockSpec((ts, V7X_LANES), lambda i: (i, 0)),
                   pl.BlockSpec((ts, V7X_LANES), lambda i: (i, 0)),
                   pl.BlockSpec((ROPE_DIM, ts), lambda i: (0, i)),
                   pl.BlockSpec((ROPE_DIM, ts), lambda i: (0, i))],
        out_shape=[jax.ShapeDtypeStruct((seq, V7X_LANES), F32)] * 2
                  + [jax.ShapeDtypeStruct((ROPE_DIM, seq), F32)] * 2,
        compiler_params=_params(("parallel",), 16 * ts * V7X_LANES * 4 + (4 << 20)),
        name="rope_tables",
    )(invf_row)


def _l0_mixer_kernel(x_ref, gmix_ref, win_ref, gv_ref, ws_ref, bs_ref, wpool_ref, psc_ref, wout_ref,
                     o_ref, h_ref, u_ref, v_ref, z_ref, cat_ref, *, tm, tiles_per_seq, d_a, d_g):
    i = pl.program_id(0)
    seq_tile = i % tiles_per_seq
    nb = tm // GMLP_BLOCK

    h_ref[...] = _rms(x_ref[...], gmix_ref[...]).astype(BF16)

    u_ref[...] = _gelu_tanh(_dot(h_ref[...], win_ref[:, 0:d_a]))
    v_ref[...] = _rms(_gelu_tanh(_dot(h_ref[...], win_ref[:, d_a:2 * d_a])), gv_ref[...]).astype(BF16)

    tt = lax.broadcasted_iota(jnp.int32, (GMLP_BLOCK, GMLP_BLOCK), 0)
    ss = lax.broadcasted_iota(jnp.int32, (GMLP_BLOCK, GMLP_BLOCK), 1)
    tri = (ss // CHUNK <= tt // CHUNK).astype(F32)
    hd_w = d_a // A_HEADS
    for hd in range(A_HEADS):
        c0 = hd * hd_w
        wm = (ws_ref[hd] * tri).astype(BF16)
        rhs = jnp.concatenate(
            [v_ref[n * GMLP_BLOCK:(n + 1) * GMLP_BLOCK, c0:c0 + hd_w] for n in range(nb)], axis=1)
        mix = _dot(wm, rhs)
        bias = bs_ref[hd]
        for n in range(nb):
            r0 = n * GMLP_BLOCK
            a = u_ref[r0:r0 + GMLP_BLOCK, c0:c0 + hd_w] * (mix[:, n * hd_w:(n + 1) * hd_w] + bias)
            cat_ref[r0:r0 + GMLP_BLOCK, c0:c0 + hd_w] = a.astype(BF16)

    @pl.when(seq_tile == 0)
    def _():
        z_ref[0:POOL_HALO, :] = jnp.zeros((POOL_HALO, z_ref.shape[1]), F32)

    z_ref[POOL_HALO:POOL_HALO + tm, :] = _dot(h_ref[...], win_ref[:, 2 * d_a:])
    pos1 = lax.broadcasted_iota(jnp.int32, (tm, d_g), 0) + (seq_tile * tm + 1)
    for g, win in enumerate(POOL_WINDOWS):
        zz = z_ref[:, g * d_g:(g + 1) * d_g]
        s = zz
        k = 1
        while k < win:
            s = s + pltpu.roll(s, k, axis=0)
            k *= 2
        cnt = jnp.minimum(pos1, win).astype(F32)
        d = (s[POOL_HALO:] / cnt - zz[POOL_HALO:]).astype(BF16)
        y = _dot(d, wpool_ref[g]) * psc_ref[:, g * d_g:(g + 1) * d_g]
        cat_ref[:, d_a + g * d_g:d_a + (g + 1) * d_g] = y.astype(BF16)
    z_ref[0:POOL_HALO, :] = z_ref[tm:tm + POOL_HALO, :]

    o_ref[...] = x_ref[...] + _dot(cat_ref[...], wout_ref[...])


def _l0_mixer(x2d, gmix, w_in, g_v, w_s, b_s_b, w_pool, pscale, w_out, *, seq, tm):
    t, d = x2d.shape
    n_in = w_in.shape[1]
    d_a = g_v.shape[1]
    d_b = pscale.shape[1]
    d_g = d_b // len(POOL_WINDOWS)
    assert n_in == 2 * d_a + d_b and seq % tm == 0 and tm % GMLP_BLOCK == 0
    vmem = (4 * tm * d * 4
            + w_in.size * 2 + w_out.size * 2 + w_pool.size * 2 + (w_s.size + b_s_b.size) * 4
            + tm * d * 2 + tm * d_a * 4 + tm * d_a * 2 + (tm + POOL_HALO) * d_b * 4 + tm * (d_a + d_b) * 2
            + 4 * tm * d_a * 4)
    return pl.pallas_call(
        functools.partial(_l0_mixer_kernel, tm=tm, tiles_per_seq=seq // tm, d_a=d_a, d_g=d_g),
        grid=(t // tm,),
        in_specs=[pl.BlockSpec((tm, d), lambda i: (i, 0)),
                  _const_spec(gmix.shape), _const_spec(w_in.shape), _const_spec(g_v.shape),
                  _const_spec(w_s.shape), _const_spec(b_s_b.shape), _const_spec(w_pool.shape),
                  _const_spec(pscale.shape), _const_spec(w_out.shape)],
        out_specs=pl.BlockSpec((tm, d), lambda i: (i, 0)),
        out_shape=jax.ShapeDtypeStruct((t, d), F32),
        scratch_shapes=[pltpu.VMEM((tm, d), BF16),
                        pltpu.VMEM((tm, d_a), F32),
                        pltpu.VMEM((tm, d_a), BF16),
                        pltpu.VMEM((tm + POOL_HALO, d_b), F32),
                        pltpu.VMEM((tm, d_a + d_b), BF16)],
        compiler_params=_params(("arbitrary",), vmem),
        name="l0_mixer",
    )(x2d, gmix, w_in, g_v, w_s, b_s_b, w_pool, pscale, w_out)


def _ffn_kernel(*refs, final_norm, emit_bf16):
    it = iter(refs)
    x_ref, g_ref, wg_ref, wu_ref, wd_ref = next(it), next(it), next(it), next(it), next(it)
    gfin_ref = next(it) if final_norm else None
    o_ref = next(it)
    bf16_out = [next(it) for _ in range(3)] if emit_bf16 else None
    h_ref = next(it)
    f = pl.program_id(1)

    @pl.when(f == 0)
    def _():
        x = x_ref[...]
        h_ref[...] = _rms(x, g_ref[...]).astype(BF16)
        o_ref[...] = x

    wg = wg_ref[...].astype(BF16)
    wu = wu_ref[...].astype(BF16)
    wd = wd_ref[...].astype(BF16)
    if emit_bf16:
        for dst_ref, w in zip(bf16_out, (wg, wu, wd)):
            dst_ref[...] = w

    h = h_ref[...]
    gate = _dot(h, wg)
    up = _dot(h, wu)
    act = (gate * jax.nn.sigmoid(gate) * up).astype(BF16)
    o_ref[...] += _dot(act, wd)

    if final_norm:
        @pl.when(f == pl.num_programs(1) - 1)
        def _():
            o_ref[...] = _rms(o_ref[...], gfin_ref[...])


def _ffn(x2d, g, w_gate, w_up, w_down, *, tm, tf, layer=None, tile_lo=0, n_tiles=None, in_place=False,
         emit_bf16=False, g_final=None):
    t, d = x2d.shape
    dff = w_gate.shape[-1]
    n_tiles = t // tm - tile_lo if n_tiles is None else n_tiles
    nf = dff // tf
    assert t % tm == 0 and dff % tf == 0 and (not emit_bf16 or n_tiles == 1)
    final_norm = g_final is not None
    wbytes = w_gate.dtype.itemsize

    def wspec(w, blk, imap):
        if w.ndim == 2:
            return pl.BlockSpec(blk, imap)
        return pl.BlockSpec((None,) + blk, lambda i, f: (layer,) + imap(i, f))

    rows = lambda i, f: (i + tile_lo, 0)
    gate_cols = lambda i, f: (0, f)
    down_rows = lambda i, f: (f, 0)
    args = [x2d, g, w_gate, w_up, w_down]
    in_specs = [pl.BlockSpec((tm, d), rows), _const_spec(g.shape),
                wspec(w_gate, (d, tf), gate_cols), wspec(w_up, (d, tf), gate_cols),
                wspec(w_down, (tf, d), down_rows)]
    vmem = 4 * tm * d * 4 + tm * d * 2 + 2 * 3 * d * tf * wbytes + 3 * d * tf * 2 + 4 * tm * tf * 4
    if final_norm:
        args.append(g_final)
        in_specs.append(_const_spec(g_final.shape))
    out_specs = [pl.BlockSpec((tm, d), rows)]
    out_shape = [jax.ShapeDtypeStruct((t, d), F32)]
    if emit_bf16:
        out_specs += [pl.BlockSpec((d, tf), gate_cols), pl.BlockSpec((d, tf), gate_cols),
                      pl.BlockSpec((tf, d), down_rows)]
        out_shape += [jax.ShapeDtypeStruct((d, dff), BF16)] * 2 + [jax.ShapeDtypeStruct((dff, d), BF16)]
        vmem += 2 * 3 * d * tf * 2
    res = pl.pallas_call(
        functools.partial(_ffn_kernel, final_norm=final_norm, emit_bf16=emit_bf16),
        grid=(n_tiles, nf),
        in_specs=in_specs,
        out_specs=out_specs,
        out_shape=out_shape,
        input_output_aliases={0: 0} if in_place else {},
        scratch_shapes=[pltpu.VMEM((tm, d), BF16)],
        compiler_params=_params(("parallel", "arbitrary"), vmem),
        name="ffn_final" if final_norm else ("ffn_head" if emit_bf16 else "ffn"),
    )(*args)
    return res if emit_bf16 else res[0]


def _mla_proj_kernel(x_ref, gmix_ref, win_ref, gcq_ref, gckv_ref, wuq_tr_ref, wkn_ref, wv_tr_ref,
                     cos_ref, sin_ref, cos_tr_ref, sin_tr_ref,
                     q_tr_ref, k_ref, v_tr_ref, h_ref, cq_tr_ref, ckv_ref, ckv_tr_ref, *, chunk):
    h_ref[...] = _rms(x_ref[...], gmix_ref[...]).astype(BF16)
    p = _dot(h_ref[...], win_ref[...])
    cq_tr_ref[...] = _rms(p[:, 0:Q_LORA], gcq_ref[...]).T.astype(BF16)
    ckv = _rms(p[:, Q_LORA:Q_LORA + KV_LORA], gckv_ref[...])
    ckv_ref[...] = ckv.astype(BF16)
    ckv_tr_ref[...] = ckv.T.astype(BF16)
    a = p[:, Q_LORA + KV_LORA:]
    k_rope = (a * cos_ref[...] + pltpu.roll(a, ROPE_DIM, axis=1) * sin_ref[...]).astype(BF16)

    rope_lo, rope_mid, rope_hi = NOPE_DIM, NOPE_DIM + ROPE_DIM // 2, NOPE_DIM + ROPE_DIM
    for hd in range(C_HEADS):
        qh = _dot(wuq_tr_ref[hd * rope_hi:(hd + 1) * rope_hi, :], cq_tr_ref[...])
        r0 = hd * QK_PAD
        q_tr_ref[r0:r0 + rope_lo, :] = qh[0:rope_lo].astype(BF16)
        swapped = jnp.concatenate([qh[rope_mid:rope_hi], qh[rope_lo:rope_mid]], axis=0)
        roped = qh[rope_lo:rope_hi] * cos_tr_ref[...] + swapped * sin_tr_ref[...]
        q_tr_ref[r0 + rope_lo:r0 + rope_hi, :] = roped.astype(BF16)
        q_tr_ref[r0 + rope_hi:r0 + QK_PAD, :] = jnp.zeros((QK_PAD - rope_hi, qh.shape[1]), BF16)

    heads_per_chunk = chunk // NOPE_DIM
    for c in range(C_HEADS // heads_per_chunk):
        sl = slice(c * chunk, (c + 1) * chunk)
        kn = _dot(ckv_ref[...], wkn_ref[:, sl]).astype(BF16)
        for hh in range(heads_per_chunk):
            hd = c * heads_per_chunk + hh
            k_ref[hd, :, 0:NOPE_DIM] = kn[:, hh * NOPE_DIM:(hh + 1) * NOPE_DIM]
            k_ref[hd, :, NOPE_DIM:] = k_rope
        v_tr_ref[sl, :] = _dot(wv_tr_ref[sl, :], ckv_tr_ref[...]).astype(BF16)


def _mla_proj(x2d, gmix, w_in_ext, g_cq_s, g_ckv, w_uq_tr, w_kn, w_v_tr, cos_t, sin_t, cos_tr, sin_tr,
              *, seq, tm):
    t, d = x2d.shape
    n_q = C_HEADS * QK_PAD
    assert w_uq_tr.shape[0] == C_HEADS * (NOPE_DIM + ROPE_DIM)
    n_kv = w_kn.shape[1]
    tiles_per_seq = seq // tm
    assert seq % tm == 0 and w_in_ext.shape[1] == Q_LORA + KV_LORA + 2 * ROPE_DIM
    vmem = (2 * tm * d * 4 + (w_in_ext.size + w_uq_tr.size + w_kn.size + w_v_tr.size) * 2
            + 2 * 2 * tm * (V7X_LANES + ROPE_DIM) * 4
            + 2 * tm * (n_q + C_HEADS * QK_PAD + n_kv) * 2
            + tm * d * 2 + tm * (Q_LORA + 2 * KV_LORA) * 2
            + 4 * tm * w_in_ext.shape[1] * 4)
    row = lambda i: (i, 0)
    col = lambda i: (0, i)
    return pl.pallas_call(
        functools.partial(_mla_proj_kernel, chunk=512),
        grid=(t // tm,),
        in_specs=[pl.BlockSpec((tm, d), row),
                  _const_spec(gmix.shape), _const_spec(w_in_ext.shape), _const_spec(g_cq_s.shape),
                  _const_spec(g_ckv.shape), _const_spec(w_uq_tr.shape), _const_spec(w_kn.shape),
                  _const_spec(w_v_tr.shape),
                  pl.BlockSpec((tm, V7X_LANES), lambda i: (i % tiles_per_seq, 0)),
                  pl.BlockSpec((tm, V7X_LANES), lambda i: (i % tiles_per_seq, 0)),
                  pl.BlockSpec((ROPE_DIM, tm), lambda i: (0, i % tiles_per_seq)),
                  pl.BlockSpec((ROPE_DIM, tm), lambda i: (0, i % tiles_per_seq))],
        out_specs=[pl.BlockSpec((n_q, tm), col), pl.BlockSpec((C_HEADS, tm, QK_PAD), lambda i: (0, i, 0)),
                   pl.BlockSpec((n_kv, tm), col)],
        out_shape=[jax.ShapeDtypeStruct((n_q, t), BF16), jax.ShapeDtypeStruct((C_HEADS, t, QK_PAD), BF16),
                   jax.ShapeDtypeStruct((n_kv, t), BF16)],
        scratch_shapes=[pltpu.VMEM((tm, d), BF16), pltpu.VMEM((Q_LORA, tm), BF16),
                        pltpu.VMEM((tm, KV_LORA), BF16), pltpu.VMEM((KV_LORA, tm), BF16)],
        compiler_params=_params(("parallel",), vmem),
        name="mla_proj",
    )(x2d, gmix, w_in_ext, g_cq_s, g_ckv, w_uq_tr, w_kn, w_v_tr, cos_t, sin_t, cos_tr, sin_tr)


def _attn_kernel(*refs, tk, tq, skew, n_cast):
    q_tr_ref, k_ref, v_tr_ref = refs[0:3]
    cast_in = refs[3:3 + n_cast]
    o_ref = refs[3 + n_cast]
    cast_out = refs[4 + n_cast:4 + 2 * n_cast]
    vaug_ref, m_ref, acc_ref = refs[4 + 2 * n_cast:]
    for src_ref, dst_ref in zip(cast_in, cast_out):
        dst_ref[...] = src_ref[...].astype(BF16)
    seq = k_ref.shape[0]
    vaug_ref[0:V_DIM, :] = v_tr_ref[...]
    vaug_ref[V_DIM:, :] = jnp.ones((vaug_ref.shape[0] - V_DIM, seq), BF16)

    units = []
    for j in range(seq // tk):
        for i in range(seq // tq):
            k_lo, q_lo = j * tk, i * tq
            k_hi = min(k_lo + tk, q_lo + tq)
            if k_hi > k_lo:
                units.append((j == 0, k_lo, k_hi, q_lo))

    def scores(unit):
        _, k_lo, k_hi, q_lo = unit
        s = _dot(k_ref[k_lo:k_hi, :], q_tr_ref[:, q_lo:q_lo + tq])
        if k_hi > q_lo + CHUNK:
            kc = (lax.broadcasted_iota(jnp.int32, s.shape, 0) + k_lo) // CHUNK
            qc = (lax.broadcasted_iota(jnp.int32, s.shape, 1) + q_lo) // CHUNK
            s = jnp.where(kc <= qc, s, -jnp.inf)
        return s

    pending = [scores(u) for u in units[:skew]]
    for n, (first, k_lo, k_hi, q_lo) in enumerate(units):
        if n + skew < len(units):
            pending.append(scores(units[n + skew]))
        s = pending.pop(0)
        qs = slice(q_lo, q_lo + tq)
        v_j = vaug_ref[:, k_lo:k_hi]
        m_blk = jnp.max(s, axis=0, keepdims=True)
        if first:
            m_new = m_blk
            acc_ref[:, qs] = _dot(v_j, jnp.exp2(s - m_new).astype(BF16))
        else:
            m_prev = m_ref[:, qs]
            m_new = jnp.maximum(m_prev, m_blk)
            alpha = jnp.exp2(m_prev - m_new)
            acc_ref[:, qs] = alpha * acc_ref[:, qs] + _dot(v_j, jnp.exp2(s - m_new).astype(BF16))
        m_ref[:, qs] = m_new

    for i in range(seq // tq):
        qs = slice(i * tq, (i + 1) * tq)
        o_ref[qs, :] = (acc_ref[0:V_DIM, qs] / acc_ref[V_DIM:V_DIM + 1, qs]).T.astype(o_ref.dtype)


def _attention(q_tr, k, v_tr, *, batch, seq, tk, tq, skew, cast_next=(), cast_layer=None):
    t = k.shape[1]
    steps = batch * C_HEADS
    assert seq % tk == 0 and seq % tq == 0 and tk % CHUNK == 0 and tq % CHUNK == 0
    assert V_DIM == V7X_LANES and NOPE_DIM == V7X_LANES
    vmem = (2 * (2 * QK_PAD + V_DIM) * seq * 2 + 2 * seq * V_DIM * 2
            + (V_DIM + ONES_ROWS) * seq * 6 + 8 * seq * 4 + 24 * tk * tq * 4)
    in_specs = [pl.BlockSpec((QK_PAD, seq), lambda b, h: (h, b)),
                pl.BlockSpec((None, seq, QK_PAD), lambda b, h: (h, b, 0)),
                pl.BlockSpec((V_DIM, seq), lambda b, h: (h, b))]
    out_specs = [pl.BlockSpec((None, seq, V_DIM), lambda b, h: (h, b, 0))]
    out_shape = [jax.ShapeDtypeStruct((C_HEADS, t, V_DIM), BF16)]
    for w in cast_next:
        _, r, c = w.shape
        assert r % (steps * 16) == 0
        in_specs.append(pl.BlockSpec((None, r // steps, c), lambda b, h: (cast_layer, b * C_HEADS + h, 0)))
        out_specs.append(pl.BlockSpec((r // steps, c), lambda b, h: (b * C_HEADS + h, 0)))
        out_shape.append(jax.ShapeDtypeStruct((r, c), BF16))
        vmem += 2 * (r // steps) * c * 6
    res = pl.pallas_call(
        functools.partial(_attn_kernel, tk=tk, tq=tq, skew=skew, n_cast=len(cast_next)),
        grid=(batch, C_HEADS),
        in_specs=in_specs,
        out_specs=out_specs,
        out_shape=out_shape,
        scratch_shapes=[pltpu.VMEM((V_DIM + ONES_ROWS, seq), BF16),
                        pltpu.VMEM((1, seq), F32),
                        pltpu.VMEM((V_DIM + ONES_ROWS, seq), F32)],
        compiler_params=_params(("parallel", "parallel"), vmem),
        name="attention",
    )(q_tr, k, v_tr, *cast_next)
    return res if cast_next else res[0]


def _proj_residual_kernel(x_ref, a_ref, w_ref, o_ref):
    a = jnp.concatenate([a_ref[hd] for hd in range(a_ref.shape[0])], axis=1)
    o_ref[...] = x_ref[...] + _dot(a, w_ref[...])


def _proj_residual(x2d, a, w, *, tm):
    t, d = x2d.shape
    heads, _, hd_dim = a.shape
    k = heads * hd_dim
    vmem = 4 * tm * d * 4 + 2 * tm * k * 2 + w.size * 2 + 2 * tm * d * 4
    return pl.pallas_call(
        _proj_residual_kernel,
        grid=(t // tm,),
        in_specs=[pl.BlockSpec((tm, d), lambda i: (i, 0)), pl.BlockSpec((heads, tm, hd_dim), lambda i: (0, i, 0)),
                  _const_spec(w.shape)],
        out_specs=pl.BlockSpec((tm, d), lambda i: (i, 0)),
        out_shape=jax.ShapeDtypeStruct((t, d), F32),
        compiler_params=_params(("parallel",), vmem),
        name="attn_out_proj",
    )(x2d, a, w)


def _swap_halves(w):
    half = w.shape[-1] // 2
    return jnp.concatenate([w[..., half:], w[..., :half]], axis=-1)


def kernel(x, g_mix, g_ffn, g_final, w_in_ab, g_v, w_s, b_s, w_pool, pool_scale, w_out_ab, w_in_c, g_cq, g_ckv, w_uq, w_ukv, w_out_c, w_gate, w_up, w_down):
    batch, seq, d = x.shape
    x2d = x.reshape(batch * seq, d)
    row = lambda a: a.reshape(1, -1)

    b_s_b = jnp.broadcast_to(b_s[0][:, :, None], (A_HEADS, GMLP_BLOCK, GMLP_BLOCK))
    x1 = _l0_mixer(x2d, row(g_mix[0]), w_in_ab[0].astype(BF16), row(g_v[0]), w_s[0], b_s_b,
                   w_pool[0].astype(BF16), row(pool_scale[0]), w_out_ab[0].astype(BF16),
                   seq=seq, tm=512)
    x2_head, wg0, wu0, wd0 = _ffn(x1, row(g_ffn[0]), w_gate, w_up, w_down, layer=0, tm=1024, tf=256,
                                  n_tiles=1, in_place=True, emit_bf16=True)
    x2 = _ffn(x2_head, row(g_ffn[0]), wg0, wu0, wd0, tm=1024, tf=512, tile_lo=1, in_place=True)

    inv_freq = ROPE_THETA ** (-jnp.arange(0, ROPE_DIM, 2, dtype=F32) / ROPE_DIM)
    cos_t, sin_t, cos_tr, sin_tr = _rope_tables(
        jnp.tile(inv_freq, V7X_LANES // inv_freq.shape[0]).reshape(1, -1), seq, 512)
    wc = w_in_c[0]
    w_kr = wc[:, Q_LORA + KV_LORA:]
    w_in_ext = jnp.concatenate([wc, _swap_halves(w_kr)], axis=1).astype(BF16)
    w_uq_tr = w_uq[0].T.astype(BF16)
    wkv = w_ukv[0].reshape(KV_LORA, C_HEADS, NOPE_DIM + V_DIM)
    w_kn = wkv[..., :NOPE_DIM].reshape(KV_LORA, -1).astype(BF16)
    w_v_tr = wkv[..., NOPE_DIM:].reshape(KV_LORA, -1).T.astype(BF16)
    score_scale = (NOPE_DIM + ROPE_DIM) ** -0.5 * 1.4426950408889634
    q_tr, k, v_tr = _mla_proj(x2, row(g_mix[1]), w_in_ext, row(g_cq[0]) * score_scale,
                              row(g_ckv[0]), w_uq_tr, w_kn, w_v_tr, cos_t, sin_t, cos_tr, sin_tr,
                              seq=seq, tm=512)
    o, w_gate1, w_up1, w_down1 = _attention(q_tr, k, v_tr, batch=batch, seq=seq, tk=512, tq=256, skew=3,
                                            cast_next=(w_gate, w_up, w_down), cast_layer=1)
    x3 = _proj_residual(x2, o, w_out_c[0].astype(BF16), tm=512)
    out = _ffn(x3, row(g_ffn[1]), w_gate1, w_up1, w_down1, tm=1024, tf=512, g_final=row(g_final))
    return out.reshape(batch, seq, d)
```

```python
import functools

import jax
import jax.numpy as jnp
from jax import lax
from jax.experimental import pallas as pl
from jax.experimental.pallas import tpu as pltpu

F32 = jnp.float32
BF16 = jnp.bfloat16

CHUNK = 64
EPS = 1e-6
GMLP_BLOCK = 128
A_HEADS = 8
POOL_WINDOWS = (2, 4, 8, 16)
POOL_HALO = 16
C_HEADS = 16
Q_LORA = 512
KV_LORA = 512
NOPE_DIM = 128
ROPE_DIM = 64
V_DIM = 128
ROPE_THETA = 10000.0
QK_PAD = 256
ONES_ROWS = 16

V7X_LANES = 128
V7X_SCOPED_VMEM_BYTES = 60000 * 1024


def _rms(xf, g):
    ms = jnp.mean(xf * xf, axis=-1, keepdims=True)
    return xf * lax.rsqrt(ms + EPS) * g


def _gelu_tanh(x):
    c = 0.7978845608028654
    return 0.5 * x * (1.0 + jnp.tanh(c * (x + 0.044715 * (x * x * x))))


def _dot(a, b):
    return jnp.dot(a, b, preferred_element_type=F32)


def _const_spec(shape):
    nd = len(shape)
    return pl.BlockSpec(shape, lambda *_: (0,) * nd, pipeline_mode=pl.Buffered(1))


def _params(semantics, vmem_bytes):
    return pltpu.CompilerParams(dimension_semantics=semantics,
                                vmem_limit_bytes=min(int(vmem_bytes), V7X_SCOPED_VMEM_BYTES))


def _rope_table_kernel(invf_ref, cos_ref, sin_ref, cos_tr_ref, sin_tr_ref, *, ts):
    i = pl.program_id(0)
    pos = (lax.broadcasted_iota(jnp.int32, (ts, V7X_LANES), 0) + i * ts).astype(F32)
    lane = lax.broadcasted_iota(jnp.int32, (ts, V7X_LANES), 1)
    ang = pos * invf_ref[...]
    c = jnp.cos(ang)
    s = jnp.sin(ang)
    half = ROPE_DIM // 2
    cos_t = jnp.where(lane < ROPE_DIM, c, 0.0)
    sin_t = jnp.where(lane < half, -s, jnp.where(lane < ROPE_DIM, s, 0.0))
    cos_ref[...] = cos_t
    sin_ref[...] = sin_t
    cos_tr_ref[...] = cos_t.T[0:ROPE_DIM]
    sin_tr_ref[...] = sin_t.T[0:ROPE_DIM]


def _rope_tables(invf_row, seq, ts):
    return pl.pallas_call(
        functools.partial(_rope_table_kernel, ts=ts),
        grid=(seq // ts,),
        in_specs=[pl.BlockSpec((1, V7X_LANES), lambda i: (0, 0))],
        out_specs=[pl.BlockSpec((ts, V7X_LANES), lambda i: (i, 0)),
                   pl.BlockSpec((ts, V7X_LANES), lambda i: (i, 0)),
                   pl.BlockSpec((ROPE_DIM, ts), lambda i: (0, i)),
                   pl.BlockSpec((ROPE_DIM, ts), lambda i: (0, i))],
        out_shape=[jax.ShapeDtypeStruct((seq, V7X_LANES), F32)] * 2
                  + [jax.ShapeDtypeStruct((ROPE_DIM, seq), F32)] * 2,
        compiler_params=_params(("parallel",), 16 * ts * V7X_LANES * 4 + (4 << 20)),
        name="rope_tables",
    )(invf_row)


def _l0_mixer_kernel(x_ref, gmix_ref, win_ref, gv_ref, ws_ref, bs_ref, wpool_ref, psc_ref, wout_ref,
                     o_ref, h_ref, u_ref, v_ref, z_ref, cat_ref, *, tm, tiles_per_seq, d_a, d_g):
    i = pl.program_id(0)
    seq_tile = i % tiles_per_seq
    nb = tm // GMLP_BLOCK

    h_ref[...] = _rms(x_ref[...], gmix_ref[...]).astype(BF16)

    u_ref[...] = _gelu_tanh(_dot(h_ref[...], win_ref[:, 0:d_a]))
    v_ref[...] = _rms(_gelu_tanh(_dot(h_ref[...], win_ref[:, d_a:2 * d_a])), gv_ref[...]).astype(BF16)

    tt = lax.broadcasted_iota(jnp.int32, (GMLP_BLOCK, GMLP_BLOCK), 0)
    ss = lax.broadcasted_iota(jnp.int32, (GMLP_BLOCK, GMLP_BLOCK), 1)
    tri = (ss // CHUNK <= tt // CHUNK).astype(F32)
    hd_w = d_a // A_HEADS
    for hd in range(A_HEADS):
        c0 = hd * hd_w
        wm = (ws_ref[hd] * tri).astype(BF16)
        rhs = jnp.concatenate(
            [v_ref[n * GMLP_BLOCK:(n + 1) * GMLP_BLOCK, c0:c0 + hd_w] for n in range(nb)], axis=1)
        mix = _dot(wm, rhs)
        bias = bs_ref[hd]
        for n in range(nb):
            r0 = n * GMLP_BLOCK
            a = u_ref[r0:r0 + GMLP_BLOCK, c0:c0 + hd_w] * (mix[:, n * hd_w:(n + 1) * hd_w] + bias)
            cat_ref[r0:r0 + GMLP_BLOCK, c0:c0 + hd_w] = a.astype(BF16)

    @pl.when(seq_tile == 0)
    def _():
        z_ref[0:POOL_HALO, :] = jnp.zeros((POOL_HALO, z_ref.shape[1]), F32)

    z_ref[POOL_HALO:POOL_HALO + tm, :] = _dot(h_ref[...], win_ref[:, 2 * d_a:])
    pos1 = lax.broadcasted_iota(jnp.int32, (tm, d_g), 0) + (seq_tile * tm + 1)
    for g, win in enumerate(POOL_WINDOWS):
        zz = z_ref[:, g * d_g:(g + 1) * d_g]
        s = zz
        k = 1
        while k < win:
            s = s + pltpu.roll(s, k, axis=0)
            k *= 2
        cnt = jnp.minimum(pos1, win).astype(F32)
        d = (s[POOL_HALO:] / cnt - zz[POOL_HALO:]).astype(BF16)
        y = _dot(d, wpool_ref[g]) * psc_ref[:, g * d_g:(g + 1) * d_g]
        cat_ref[:, d_a + g * d_g:d_a + (g + 1) * d_g] = y.astype(BF16)
    z_ref[0:POOL_HALO, :] = z_ref[tm:tm + POOL_HALO, :]

    o_ref[...] = x_ref[...] + _dot(cat_ref[...], wout_ref[...])


def _l0_mixer(x2d, gmix, w_in, g_v, w_s, b_s_b, w_pool, pscale, w_out, *, seq, tm):
    t, d = x2d.shape
    n_in = w_in.shape[1]
    d_a = g_v.shape[1]
    d_b = pscale.shape[1]
    d_g = d_b // len(POOL_WINDOWS)
    assert n_in == 2 * d_a + d_b and seq % tm == 0 and tm % GMLP_BLOCK == 0
    vmem = (4 * tm * d * 4
            + w_in.size * 2 + w_out.size * 2 + w_pool.size * 2 + (w_s.size + b_s_b.size) * 4
            + tm * d * 2 + tm * d_a * 4 + tm * d_a * 2 + (tm + POOL_HALO) * d_b * 4 + tm * (d_a + d_b) * 2
            + 4 * tm * d_a * 4)
    return pl.pallas_call(
        functools.partial(_l0_mixer_kernel, tm=tm, tiles_per_seq=seq // tm, d_a=d_a, d_g=d_g),
        grid=(t // tm,),
        in_specs=[pl.BlockSpec((tm, d), lambda i: (i, 0)),
                  _const_spec(gmix.shape), _const_spec(w_in.shape), _const_spec(g_v.shape),
                  _const_spec(w_s.shape), _const_spec(b_s_b.shape), _const_spec(w_pool.shape),
                  _const_spec(pscale.shape), _const_spec(w_out.shape)],
        out_specs=pl.BlockSpec((tm, d), lambda i: (i, 0)),
        out_shape=jax.ShapeDtypeStruct((t, d), F32),
        scratch_shapes=[pltpu.VMEM((tm, d), BF16),
                        pltpu.VMEM((tm, d_a), F32),
                        pltpu.VMEM((tm, d_a), BF16),
                        pltpu.VMEM((tm + POOL_HALO, d_b), F32),
                        pltpu.VMEM((tm, d_a + d_b), BF16)],
        compiler_params=_params(("arbitrary",), vmem),
        name="l0_mixer",
    )(x2d, gmix, w_in, g_v, w_s, b_s_b, w_pool, pscale, w_out)


def _ffn_kernel(*refs, final_norm, emit_bf16):
    it = iter(refs)
    x_ref, g_ref, wg_ref, wu_ref, wd_ref = next(it), next(it), next(it), next(it), next(it)
    gfin_ref = next(it) if final_norm else None
    o_ref = next(it)
    bf16_out = [next(it) for _ in range(3)] if emit_bf16 else None
    h_ref = next(it)
    f = pl.program_id(1)
    last = pl.num_programs(1) - 1

    def step(first, final):
        if first:
            h_ref[...] = _rms(x_ref[...], g_ref[...]).astype(BF16)
        wg = wg_ref[...].astype(BF16)
        wu = wu_ref[...].astype(BF16)
        wd = wd_ref[...].astype(BF16)
        if emit_bf16:
            for dst_ref, w in zip(bf16_out, (wg, wu, wd)):
                dst_ref[...] = w
        h = h_ref[...]
        gate = _dot(h, wg)
        up = _dot(h, wu)
        act = (gate * jax.nn.sigmoid(gate) * up).astype(BF16)
        acc = (x_ref[...] if first else o_ref[...]) + _dot(act, wd)
        o_ref[...] = _rms(acc, gfin_ref[...]) if final else acc

    pl.when(f == 0)(functools.partial(step, True, False))
    if final_norm:
        pl.when((f > 0) & (f < last))(functools.partial(step, False, False))
        pl.when(f == last)(functools.partial(step, False, True))
    else:
        pl.when(f > 0)(functools.partial(step, False, False))


def _ffn(x2d, g, w_gate, w_up, w_down, *, tm, tf, layer=None, tile_lo=0, n_tiles=None, in_place=False,
         emit_bf16=False, g_final=None):
    t, d = x2d.shape
    dff = w_gate.shape[-1]
    n_tiles = t // tm - tile_lo if n_tiles is None else n_tiles
    nf = dff // tf
    assert t % tm == 0 and dff % tf == 0 and (not emit_bf16 or n_tiles == 1)
    final_norm = g_final is not None
    wbytes = w_gate.dtype.itemsize

    def wspec(w, blk, imap):
        if w.ndim == 2:
            return pl.BlockSpec(blk, imap)
        return pl.BlockSpec((None,) + blk, lambda i, f: (layer,) + imap(i, f))

    rows = lambda i, f: (i + tile_lo, 0)
    gate_cols = lambda i, f: (0, f)
    down_rows = lambda i, f: (f, 0)
    args = [x2d, g, w_gate, w_up, w_down]
    in_specs = [pl.BlockSpec((tm, d), rows), _const_spec(g.shape),
                wspec(w_gate, (d, tf), gate_cols), wspec(w_up, (d, tf), gate_cols),
                wspec(w_down, (tf, d), down_rows)]
    vmem = 4 * tm * d * 4 + tm * d * 2 + 2 * 3 * d * tf * wbytes + 3 * d * tf * 2 + 4 * tm * tf * 4
    if final_norm:
        args.append(g_final)
        in_specs.append(_const_spec(g_final.shape))
    out_specs = [pl.BlockSpec((tm, d), rows)]
    out_shape = [jax.ShapeDtypeStruct((t, d), F32)]
    if emit_bf16:
        out_specs += [pl.BlockSpec((d, tf), gate_cols), pl.BlockSpec((d, tf), gate_cols),
                      pl.BlockSpec((tf, d), down_rows)]
        out_shape += [jax.ShapeDtypeStruct((d, dff), BF16)] * 2 + [jax.ShapeDtypeStruct((dff, d), BF16)]
        vmem += 2 * 3 * d * tf * 2
    res = pl.pallas_call(
        functools.partial(_ffn_kernel, final_norm=final_norm, emit_bf16=emit_bf16),
        grid=(n_tiles, nf),
        in_specs=in_specs,
        out_specs=out_specs,
        out_shape=out_shape,
        input_output_aliases={0: 0} if in_place else {},
        scratch_shapes=[pltpu.VMEM((tm, d), BF16)],
        compiler_params=_params(("parallel", "arbitrary"), vmem),
        name="ffn_final" if final_norm else ("ffn_head" if emit_bf16 else "ffn"),
    )(*args)
    return res if emit_bf16 else res[0]


def _mla_proj_kernel(x_ref, gmix_ref, win_ref, gcq_ref, gckv_ref, wuq_tr_ref, wkn_ref, wv_tr_ref,
                     cos_ref, sin_ref, cos_tr_ref, sin_tr_ref,
                     q_tr_ref, k_ref, v_tr_ref, h_ref, cq_tr_ref, ckv_ref, ckv_tr_ref, *, chunk):
    h_ref[...] = _rms(x_ref[...], gmix_ref[...]).astype(BF16)
    p = _dot(h_ref[...], win_ref[...])
    cq_tr_ref[...] = _rms(p[:, 0:Q_LORA], gcq_ref[...]).T.astype(BF16)
    ckv = _rms(p[:, Q_LORA:Q_LORA + KV_LORA], gckv_ref[...])
    ckv_ref[...] = ckv.astype(BF16)
    ckv_tr_ref[...] = ckv.T.astype(BF16)
    a = p[:, Q_LORA + KV_LORA:]
    k_rope = (a * cos_ref[...] + pltpu.roll(a, ROPE_DIM, axis=1) * sin_ref[...]).astype(BF16)

    rope_lo, rope_mid, rope_hi = NOPE_DIM, NOPE_DIM + ROPE_DIM // 2, NOPE_DIM + ROPE_DIM
    for hd in range(C_HEADS):
        qh = _dot(wuq_tr_ref[hd * rope_hi:(hd + 1) * rope_hi, :], cq_tr_ref[...])
        r0 = hd * QK_PAD
        q_tr_ref[r0:r0 + rope_lo, :] = qh[0:rope_lo].astype(BF16)
        swapped = jnp.concatenate([qh[rope_mid:rope_hi], qh[rope_lo:rope_mid]], axis=0)
        roped = qh[rope_lo:rope_hi] * cos_tr_ref[...] + swapped * sin_tr_ref[...]
        q_tr_ref[r0 + rope_lo:r0 + rope_hi, :] = roped.astype(BF16)
        q_tr_ref[r0 + rope_hi:r0 + QK_PAD, :] = jnp.zeros((QK_PAD - rope_hi, qh.shape[1]), BF16)

    heads_per_chunk = chunk // NOPE_DIM
    for c in range(C_HEADS // heads_per_chunk):
        sl = slice(c * chunk, (c + 1) * chunk)
        kn = _dot(ckv_ref[...], wkn_ref[:, sl]).astype(BF16)
        for hh in range(heads_per_chunk):
            hd = c * heads_per_chunk + hh
            k_ref[hd, :, 0:NOPE_DIM] = kn[:, hh * NOPE_DIM:(hh + 1) * NOPE_DIM]
            k_ref[hd, :, NOPE_DIM:] = k_rope
        v_tr_ref[sl, :] = _dot(wv_tr_ref[sl, :], ckv_tr_ref[...]).astype(BF16)


def _mla_proj(x2d, gmix, w_in_ext, g_cq_s, g_ckv, w_uq_tr, w_kn, w_v_tr, cos_t, sin_t, cos_tr, sin_tr,
              *, seq, tm):
    t, d = x2d.shape
    n_q = C_HEADS * QK_PAD
    assert w_uq_tr.shape[0] == C_HEADS * (NOPE_DIM + ROPE_DIM)
    n_kv = w_kn.shape[1]
    tiles_per_seq = seq // tm
    assert seq % tm == 0 and w_in_ext.shape[1] == Q_LORA + KV_LORA + 2 * ROPE_DIM
    vmem = (2 * tm * d * 4 + (w_in_ext.size + w_uq_tr.size + w_kn.size + w_v_tr.size) * 2
            + 2 * 2 * tm * (V7X_LANES + ROPE_DIM) * 4
            + 2 * tm * (n_q + C_HEADS * QK_PAD + n_kv) * 2
            + tm * d * 2 + tm * (Q_LORA + 2 * KV_LORA) * 2
            + 4 * tm * w_in_ext.shape[1] * 4)
    row = lambda i: (i, 0)
    col = lambda i: (0, i)
    return pl.pallas_call(
        functools.partial(_mla_proj_kernel, chunk=512),
        grid=(t // tm,),
        in_specs=[pl.BlockSpec((tm, d), row),
                  _const_spec(gmix.shape), _const_spec(w_in_ext.shape), _const_spec(g_cq_s.shape),
                  _const_spec(g_ckv.shape), _const_spec(w_uq_tr.shape), _const_spec(w_kn.shape),
                  _const_spec(w_v_tr.shape),
                  pl.BlockSpec((tm, V7X_LANES), lambda i: (i % tiles_per_seq, 0)),
                  pl.BlockSpec((tm, V7X_LANES), lambda i: (i % tiles_per_seq, 0)),
                  pl.BlockSpec((ROPE_DIM, tm), lambda i: (0, i % tiles_per_seq)),
                  pl.BlockSpec((ROPE_DIM, tm), lambda i: (0, i % tiles_per_seq))],
        out_specs=[pl.BlockSpec((n_q, tm), col), pl.BlockSpec((C_HEADS, tm, QK_PAD), lambda i: (0, i, 0)),
                   pl.BlockSpec((n_kv, tm), col)],
        out_shape=[jax.ShapeDtypeStruct((n_q, t), BF16), jax.ShapeDtypeStruct((C_HEADS, t, QK_PAD), BF16),
                   jax.ShapeDtypeStruct((n_kv, t), BF16)],
        scratch_shapes=[pltpu.VMEM((tm, d), BF16), pltpu.VMEM((Q_LORA, tm), BF16),
                        pltpu.VMEM((tm, KV_LORA), BF16), pltpu.VMEM((KV_LORA, tm), BF16)],
        compiler_params=_params(("parallel",), vmem),
        name="mla_proj",
    )(x2d, gmix, w_in_ext, g_cq_s, g_ckv, w_uq_tr, w_kn, w_v_tr, cos_t, sin_t, cos_tr, sin_tr)


def _attn_kernel(*refs, tk, tq, skew, n_cast):
    q_tr_ref, k_ref, v_tr_ref = refs[0:3]
    cast_in = refs[3:3 + n_cast]
    o_ref = refs[3 + n_cast]
    cast_out = refs[4 + n_cast:4 + 2 * n_cast]
    vaug_ref, m_ref, acc_ref = refs[4 + 2 * n_cast:]
    for src_ref, dst_ref in zip(cast_in, cast_out):
        dst_ref[...] = src_ref[...].astype(BF16)
    seq = k_ref.shape[0]
    vaug_ref[0:V_DIM, :] = v_tr_ref[...]
    vaug_ref[V_DIM:, :] = jnp.ones((vaug_ref.shape[0] - V_DIM, seq), BF16)

    units = []
    for j in range(seq // tk):
        for i in range(seq // tq):
            k_lo, q_lo = j * tk, i * tq
            k_hi = min(k_lo + tk, q_lo + tq)
            if k_hi > k_lo:
                units.append((j == 0, k_lo, k_hi, q_lo))

    def scores(unit):
        _, k_lo, k_hi, q_lo = unit
        s = _dot(k_ref[k_lo:k_hi, :], q_tr_ref[:, q_lo:q_lo + tq])
        if k_hi > q_lo + CHUNK:
            kc = (lax.broadcasted_iota(jnp.int32, s.shape, 0) + k_lo) // CHUNK
            qc = (lax.broadcasted_iota(jnp.int32, s.shape, 1) + q_lo) // CHUNK
            s = jnp.where(kc <= qc, s, -jnp.inf)
        return s

    pending = [scores(u) for u in units[:skew]]
    for n, (first, k_lo, k_hi, q_lo) in enumerate(units):
        if n + skew < len(units):
            pending.append(scores(units[n + skew]))
        s = pending.pop(0)
        qs = slice(q_lo, q_lo + tq)
        v_j = vaug_ref[:, k_lo:k_hi]
        m_blk = jnp.max(s, axis=0, keepdims=True)
        if first:
            m_new = m_blk
            acc_ref[:, qs] = _dot(v_j, jnp.exp2(s - m_new).astype(BF16))
        else:
            m_prev = m_ref[:, qs]
            m_new = jnp.maximum(m_prev, m_blk)
            alpha = jnp.exp2(m_prev - m_new)
            acc_ref[:, qs] = alpha * acc_ref[:, qs] + _dot(v_j, jnp.exp2(s - m_new).astype(BF16))
        m_ref[:, qs] = m_new

    for i in range(seq // tq):
        qs = slice(i * tq, (i + 1) * tq)
        o_ref[qs, :] = (acc_ref[0:V_DIM, qs] / acc_ref[V_DIM:V_DIM + 1, qs]).T.astype(o_ref.dtype)


def _attention(q_tr, k, v_tr, *, batch, seq, tk, tq, skew, cast_next=(), cast_layer=None):
    t = k.shape[1]
    steps = batch * C_HEADS
    assert seq % tk == 0 and seq % tq == 0 and tk % CHUNK == 0 and tq % CHUNK == 0
    assert V_DIM == V7X_LANES and NOPE_DIM == V7X_LANES
    vmem = (2 * (2 * QK_PAD + V_DIM) * seq * 2 + 2 * seq * V_DIM * 2
            + (V_DIM + ONES_ROWS) * seq * 6 + 8 * seq * 4 + 24 * tk * tq * 4)
    in_specs = [pl.BlockSpec((QK_PAD, seq), lambda b, h: (h, b)),
                pl.BlockSpec((None, seq, QK_PAD), lambda b, h: (h, b, 0)),
                pl.BlockSpec((V_DIM, seq), lambda b, h: (h, b))]
    out_specs = [pl.BlockSpec((None, seq, V_DIM), lambda b, h: (h, b, 0))]
    out_shape = [jax.ShapeDtypeStruct((C_HEADS, t, V_DIM), BF16)]
    for w in cast_next:
        _, r, c = w.shape
        assert r % (steps * 16) == 0
        in_specs.append(pl.BlockSpec((None, r // steps, c), lambda b, h: (cast_layer, b * C_HEADS + h, 0)))
        out_specs.append(pl.BlockSpec((r // steps, c), lambda b, h: (b * C_HEADS + h, 0)))
        out_shape.append(jax.ShapeDtypeStruct((r, c), BF16))
        vmem += 2 * (r // steps) * c * 6
    res = pl.pallas_call(
        functools.partial(_attn_kernel, tk=tk, tq=tq, skew=skew, n_cast=len(cast_next)),
        grid=(batch, C_HEADS),
        in_specs=in_specs,
        out_specs=out_specs,
        out_shape=out_shape,
        scratch_shapes=[pltpu.VMEM((V_DIM + ONES_ROWS, seq), BF16),
                        pltpu.VMEM((1, seq), F32),
                        pltpu.VMEM((V_DIM + ONES_ROWS, seq), F32)],
        compiler_params=_params(("parallel", "parallel"), vmem),
        name="attention",
    )(q_tr, k, v_tr, *cast_next)
    return res if cast_next else res[0]


def _proj_residual_kernel(x_ref, a_ref, w_ref, o_ref):
    a = jnp.concatenate([a_ref[hd] for hd in range(a_ref.shape[0])], axis=1)
    o_ref[...] = x_ref[...] + _dot(a, w_ref[...])


def _proj_residual(x2d, a, w, *, tm):
    t, d = x2d.shape
    heads, _, hd_dim = a.shape
    k = heads * hd_dim
    vmem = 4 * tm * d * 4 + 2 * tm * k * 2 + w.size * 2 + 2 * tm * d * 4
    return pl.pallas_call(
        _proj_residual_kernel,
        grid=(t // tm,),
        in_specs=[pl.BlockSpec((tm, d), lambda i: (i, 0)), pl.BlockSpec((heads, tm, hd_dim), lambda i: (0, i, 0)),
                  _const_spec(w.shape)],
        out_specs=pl.BlockSpec((tm, d), lambda i: (i, 0)),
        out_shape=jax.ShapeDtypeStruct((t, d), F32),
        compiler_params=_params(("parallel",), vmem),
        name="attn_out_proj",
    )(x2d, a, w)


def _swap_halves(w):
    half = w.shape[-1] // 2
    return jnp.concatenate([w[..., half:], w[..., :half]], axis=-1)


def kernel(x, g_mix, g_ffn, g_final, w_in_ab, g_v, w_s, b_s, w_pool, pool_scale, w_out_ab, w_in_c, g_cq, g_ckv, w_uq, w_ukv, w_out_c, w_gate, w_up, w_down):
    batch, seq, d = x.shape
    x2d = x.reshape(batch * seq, d)
    row = lambda a: a.reshape(1, -1)

    b_s_b = jnp.broadcast_to(b_s[0][:, :, None], (A_HEADS, GMLP_BLOCK, GMLP_BLOCK))
    x1 = _l0_mixer(x2d, row(g_mix[0]), w_in_ab[0].astype(BF16), row(g_v[0]), w_s[0], b_s_b,
                   w_pool[0].astype(BF16), row(pool_scale[0]), w_out_ab[0].astype(BF16),
                   seq=seq, tm=512)
    x2_head, wg0, wu0, wd0 = _ffn(x1, row(g_ffn[0]), w_gate, w_up, w_down, layer=0, tm=1024, tf=256,
                                  n_tiles=1, in_place=True, emit_bf16=True)
    x2 = _ffn(x2_head, row(g_ffn[0]), wg0, wu0, wd0, tm=1024, tf=512, tile_lo=1, in_place=True)

    inv_freq = ROPE_THETA ** (-jnp.arange(0, ROPE_DIM, 2, dtype=F32) / ROPE_DIM)
    cos_t, sin_t, cos_tr, sin_tr = _rope_tables(
        jnp.tile(inv_freq, V7X_LANES // inv_freq.shape[0]).reshape(1, -1), seq, 512)
    wc = w_in_c[0]
    w_kr = wc[:, Q_LORA + KV_LORA:]
    w_in_ext = jnp.concatenate([wc, _swap_halves(w_kr)], axis=1).astype(BF16)
    w_uq_tr = w_uq[0].T.astype(BF16)
    wkv = w_ukv[0].reshape(KV_LORA, C_HEADS, NOPE_DIM + V_DIM)
    w_kn = wkv[..., :NOPE_DIM].reshape(KV_LORA, -1).astype(BF16)
    w_v_tr = wkv[..., NOPE_DIM:].reshape(KV_LORA, -1).T.astype(BF16)
    score_scale = (NOPE_DIM + ROPE_DIM) ** -0.5 * 1.4426950408889634
    q_tr, k, v_tr = _mla_proj(x2, row(g_mix[1]), w_in_ext, row(g_cq[0]) * score_scale,
                              row(g_ckv[0]), w_uq_tr, w_kn, w_v_tr, cos_t, sin_t, cos_tr, sin_tr,
                              seq=seq, tm=512)
    o, w_gate1, w_up1, w_down1 = _attention(q_tr, k, v_tr, batch=batch, seq=seq, tk=512, tq=256, skew=3,
                                            cast_next=(w_gate, w_up, w_down), cast_layer=1)
    x3 = _proj_residual(x2, o, w_out_c[0].astype(BF16), tm=512)
    out = _ffn(x3, row(g_ffn[1]), w_gate1, w_up1, w_down1, tm=1024, tf=512, g_final=row(g_final))
    return out.reshape(batch, seq, d)
```

```python
import functools

import jax
import jax.numpy as jnp
from jax import lax
from jax.experimental import pallas as pl
from jax.experimental.pallas import tpu as pltpu

F32 = jnp.float32
BF16 = jnp.bfloat16

CHUNK = 64
EPS = 1e-6
GMLP_BLOCK = 128
A_HEADS = 8
POOL_WINDOWS = (2, 4, 8, 16)
POOL_HALO = 16
C_HEADS = 16
Q_LORA = 512
KV_LORA = 512
NOPE_DIM = 128
ROPE_DIM = 64
V_DIM = 128
ROPE_THETA = 10000.0
QK_PAD = 256
ONES_ROWS = 16

V7X_LANES = 128
V7X_SCOPED_VMEM_BYTES = 60000 * 1024

PROJ_ROWS = 512
FFN_ROWS = 1024
FFN_HIDDEN_F32 = 256
FFN_HIDDEN = 512
KV_PROJ_COLS = 512
ATTN_KEYS = 512
ATTN_QUERIES = 256
ATTN_LOOKAHEAD = 3
LOG2_E = 1.4426950408889634


def _rms(xf, g):
    ms = jnp.mean(xf * xf, axis=-1, keepdims=True)
    return xf * lax.rsqrt(ms + EPS) * g


def _gelu_tanh(x):
    c = 0.7978845608028654
    return 0.5 * x * (1.0 + jnp.tanh(c * (x + 0.044715 * (x * x * x))))


def _dot(a, b):
    return jnp.dot(a, b, preferred_element_type=F32)


def _const_spec(shape):
    nd = len(shape)
    return pl.BlockSpec(shape, lambda *_: (0,) * nd, pipeline_mode=pl.Buffered(1))


def _params(semantics, vmem_bytes):
    return pltpu.CompilerParams(dimension_semantics=semantics,
                                vmem_limit_bytes=min(int(vmem_bytes), V7X_SCOPED_VMEM_BYTES))


def _rope_table_kernel(invf_ref, cos_ref, sin_ref, cos_tr_ref, sin_tr_ref, *, ts):
    i = pl.program_id(0)
    pos = (lax.broadcasted_iota(jnp.int32, (ts, V7X_LANES), 0) + i * ts).astype(F32)
    lane = lax.broadcasted_iota(jnp.int32, (ts, V7X_LANES), 1)
    ang = pos * invf_ref[...]
    c = jnp.cos(ang)
    s = jnp.sin(ang)
    half = ROPE_DIM // 2
    cos_t = jnp.where(lane < ROPE_DIM, c, 0.0)
    sin_t = jnp.where(lane < half, -s, jnp.where(lane < ROPE_DIM, s, 0.0))
    cos_ref[...] = cos_t
    sin_ref[...] = sin_t
    cos_tr_ref[...] = cos_t.T[0:ROPE_DIM]
    sin_tr_ref[...] = sin_t.T[0:ROPE_DIM]


def _rope_tables(invf_row, seq, ts):
    return pl.pallas_call(
        functools.partial(_rope_table_kernel, ts=ts),
        grid=(seq // ts,),
        in_specs=[pl.BlockSpec((1, V7X_LANES), lambda i: (0, 0))],
        out_specs=[pl.BlockSpec((ts, V7X_LANES), lambda i: (i, 0)),
                   pl.BlockSpec((ts, V7X_LANES), lambda i: (i, 0)),
                   pl.BlockSpec((ROPE_DIM, ts), lambda i: (0, i)),
                   pl.BlockSpec((ROPE_DIM, ts), lambda i: (0, i))],
        out_shape=[jax.ShapeDtypeStruct((seq, V7X_LANES), F32)] * 2
                  + [jax.ShapeDtypeStruct((ROPE_DIM, seq), F32)] * 2,
        compiler_params=_params(("parallel",), 16 * ts * V7X_LANES * 4 + (4 << 20)),
        name="rope_tables",
    )(invf_row)


def _l0_mixer_kernel(x_ref, gmix_ref, win_ref, gv_ref, ws_ref, bs_ref, wpool_ref, psc_ref, wout_ref,
                     o_ref, h_ref, u_ref, v_ref, z_ref, cat_ref, *, tm, tiles_per_seq, d_a, d_g):
    i = pl.program_id(0)
    seq_tile = i % tiles_per_seq
    nb = tm // GMLP_BLOCK

    @pl.when(seq_tile == 0)
    def _():
        z_ref[0:POOL_HALO, :] = jnp.zeros((POOL_HALO, z_ref.shape[1]), F32)

    h_ref[...] = _rms(x_ref[...], gmix_ref[...]).astype(BF16)

    pre_u = _dot(h_ref[...], win_ref[:, 0:d_a])
    pre_v = _dot(h_ref[...], win_ref[:, d_a:2 * d_a])
    z_ref[POOL_HALO:POOL_HALO + tm, :] = _dot(h_ref[...], win_ref[:, 2 * d_a:])

    u_ref[...] = _gelu_tanh(pre_u)
    v_ref[...] = _rms(_gelu_tanh(pre_v), gv_ref[...]).astype(BF16)

    tt = lax.broadcasted_iota(jnp.int32, (GMLP_BLOCK, GMLP_BLOCK), 0)
    ss = lax.broadcasted_iota(jnp.int32, (GMLP_BLOCK, GMLP_BLOCK), 1)
    tri = (ss // CHUNK <= tt // CHUNK).astype(F32)
    hd_w = d_a // A_HEADS
    for hd in range(A_HEADS):
        c0 = hd * hd_w
        wm = (ws_ref[hd] * tri).astype(BF16)
        rhs = jnp.concatenate(
            [v_ref[n * GMLP_BLOCK:(n + 1) * GMLP_BLOCK, c0:c0 + hd_w] for n in range(nb)], axis=1)
        mix = _dot(wm, rhs)
        bias = bs_ref[hd]
        for n in range(nb):
            r0 = n * GMLP_BLOCK
            a = u_ref[r0:r0 + GMLP_BLOCK, c0:c0 + hd_w] * (mix[:, n * hd_w:(n + 1) * hd_w] + bias)
            cat_ref[r0:r0 + GMLP_BLOCK, c0:c0 + hd_w] = a.astype(BF16)

    pos1 =lax.broadcasted_iota(jnp.int32, (tm, d_g), 0) + (seq_tile * tm + 1)
    for g, win in enumerate(POOL_WINDOWS):
        zz = z_ref[:, g * d_g:(g + 1) * d_g]
        s = zz
        k = 1
        while k < win:
            s = s + pltpu.roll(s, k, axis=0)
            k *= 2
        cnt = jnp.minimum(pos1, win).astype(F32)
        d = (s[POOL_HALO:] / cnt - zz[POOL_HALO:]).astype(BF16)
        y = _dot(d, wpool_ref[g]) * psc_ref[:, g * d_g:(g + 1) * d_g]
        cat_ref[:, d_a + g * d_g:d_a + (g + 1) * d_g] = y.astype(BF16)
    z_ref[0:POOL_HALO, :] = z_ref[tm:tm + POOL_HALO, :]

    o_ref[...] = x_ref[...] + _dot(cat_ref[...], wout_ref[...])


def _l0_mixer(x2d, gmix, w_in, g_v, w_s, b_s_b, w_pool, pscale, w_out, *, seq, tm):
    t, d = x2d.shape
    n_in = w_in.shape[1]
    d_a = g_v.shape[1]
    d_b = pscale.shape[1]
    d_g = d_b // len(POOL_WINDOWS)
    assert n_in == 2 * d_a + d_b and seq % tm == 0 and tm % GMLP_BLOCK == 0
    vmem = (4 * tm * d * 4
            + w_in.size * 2 + w_out.size * 2 + w_pool.size * 2 + (w_s.size + b_s_b.size) * 4
            + tm * d * 2 + tm * d_a * 4 + tm * d_a * 2 + (tm + POOL_HALO) * d_b * 4 + tm * (d_a + d_b) * 2
            + 4 * tm * d_a * 4)
    return pl.pallas_call(
        functools.partial(_l0_mixer_kernel, tm=tm, tiles_per_seq=seq // tm, d_a=d_a, d_g=d_g),
        grid=(t // tm,),
        in_specs=[pl.BlockSpec((tm, d), lambda i: (i, 0)),
                  _const_spec(gmix.shape), _const_spec(w_in.shape), _const_spec(g_v.shape),
                  _const_spec(w_s.shape), _const_spec(b_s_b.shape), _const_spec(w_pool.shape),
                  _const_spec(pscale.shape), _const_spec(w_out.shape)],
        out_specs=pl.BlockSpec((tm, d), lambda i: (i, 0)),
        out_shape=jax.ShapeDtypeStruct((t, d), F32),
        scratch_shapes=[pltpu.VMEM((tm, d), BF16),
                        pltpu.VMEM((tm, d_a), F32),
                        pltpu.VMEM((tm, d_a), BF16),
                        pltpu.VMEM((tm + POOL_HALO, d_b), F32),
                        pltpu.VMEM((tm, d_a + d_b), BF16)],
        compiler_params=_params(("arbitrary",), vmem),
        name="l0_mixer",
    )(x2d, gmix, w_in, g_v, w_s, b_s_b, w_pool, pscale, w_out)


def _ffn_kernel(*refs, final_norm, emit_bf16):
    it = iter(refs)
    x_ref, g_ref, wg_ref, wu_ref, wd_ref = next(it), next(it), next(it), next(it), next(it)
    gfin_ref = next(it) if final_norm else None
    o_ref = next(it)
    bf16_out = [next(it) for _ in range(3)] if emit_bf16 else None
    h_ref = next(it)
    f = pl.program_id(1)

    @pl.when(f == 0)
    def _():
        x = x_ref[...]
        h_ref[...] = _rms(x, g_ref[...]).astype(BF16)
        o_ref[...] = x

    wg = wg_ref[...].astype(BF16)
    wu = wu_ref[...].astype(BF16)
    wd = wd_ref[...].astype(BF16)
    if emit_bf16:
        for dst_ref, w in zip(bf16_out, (wg, wu, wd)):
            dst_ref[...] = w

    h = h_ref[...]
    gate = _dot(h, wg)
    up = _dot(h, wu)
    act = (gate * jax.nn.sigmoid(gate) * up).astype(BF16)
    o_ref[...] += _dot(act, wd)

    if final_norm:
        @pl.when(f == pl.num_programs(1) - 1)
        def _():
            o_ref[...] = _rms(o_ref[...], gfin_ref[...])


def _ffn(x2d, g, w_gate, w_up, w_down, *, tm, tf, layer=None, tile_lo=0, n_tiles=None, in_place=False,
         emit_bf16=False, g_final=None):
    t, d = x2d.shape
    dff = w_gate.shape[-1]
    n_tiles = t // tm - tile_lo if n_tiles is None else n_tiles
    nf = dff // tf
    assert t % tm == 0 and dff % tf == 0 and (not emit_bf16 or n_tiles == 1)
    final_norm = g_final is not None
    wbytes = w_gate.dtype.itemsize

    def wspec(w, blk, imap):
        if w.ndim == 2:
            return pl.BlockSpec(blk, imap)
        return pl.BlockSpec((None,) + blk, lambda i, f: (layer,) + imap(i, f))

    rows = lambda i, f: (i + tile_lo, 0)
    gate_cols = lambda i, f: (0, f)
    down_rows = lambda i, f: (f, 0)
    args = [x2d, g, w_gate, w_up, w_down]
    in_specs = [pl.BlockSpec((tm, d), rows), _const_spec(g.shape),
                wspec(w_gate, (d, tf), gate_cols), wspec(w_up, (d, tf), gate_cols),
                wspec(w_down, (tf, d), down_rows)]
    vmem = 4 * tm * d * 4 + tm * d * 2 + 2 * 3 * d * tf * wbytes + 3 * d * tf * 2 + 4 * tm * tf * 4
    if final_norm:
        args.append(g_final)
        in_specs.append(_const_spec(g_final.shape))
    out_specs = [pl.BlockSpec((tm, d), rows)]
    out_shape = [jax.ShapeDtypeStruct((t, d), F32)]
    if emit_bf16:
        out_specs += [pl.BlockSpec((d, tf), gate_cols), pl.BlockSpec((d, tf), gate_cols),
                      pl.BlockSpec((tf, d), down_rows)]
        out_shape += [jax.ShapeDtypeStruct((d, dff), BF16)] * 2 + [jax.ShapeDtypeStruct((dff, d), BF16)]
        vmem += 2 * 3 * d * tf * 2
    res = pl.pallas_call(
        functools.partial(_ffn_kernel, final_norm=final_norm, emit_bf16=emit_bf16),
        grid=(n_tiles, nf),
        in_specs=in_specs,
        out_specs=out_specs,
        out_shape=out_shape,
        input_output_aliases={0: 0} if in_place else {},
        scratch_shapes=[pltpu.VMEM((tm, d), BF16)],
        compiler_params=_params(("parallel", "arbitrary"), vmem),
        name="ffn_final" if final_norm else ("ffn_head" if emit_bf16 else "ffn"),
    )(*args)
    return res if emit_bf16 else res[0]


def _mla_proj_kernel(x_ref, gmix_ref, win_ref, gcq_ref, gckv_ref, wuq_tr_ref, wkn_ref, wv_tr_ref,
                     cos_ref, sin_ref, cos_tr_ref, sin_tr_ref,
                     q_tr_ref, k_ref, v_tr_ref, h_ref, cq_tr_ref, ckv_ref, ckv_tr_ref, *, chunk):
    h_ref[...] = _rms(x_ref[...], gmix_ref[...]).astype(BF16)
    p = _dot(h_ref[...], win_ref[...])
    cq_tr_ref[...] = _rms(p[:, 0:Q_LORA], gcq_ref[...]).T.astype(BF16)
    ckv = _rms(p[:, Q_LORA:Q_LORA + KV_LORA], gckv_ref[...])
    ckv_ref[...] = ckv.astype(BF16)
    ckv_tr_ref[...] = ckv.T.astype(BF16)
    a = p[:, Q_LORA + KV_LORA:]
    k_rope = (a * cos_ref[...] + pltpu.roll(a, ROPE_DIM, axis=1) * sin_ref[...]).astype(BF16)

    rope_lo, rope_mid, rope_hi = NOPE_DIM, NOPE_DIM + ROPE_DIM // 2, NOPE_DIM + ROPE_DIM
    for hd in range(C_HEADS):
        qh = _dot(wuq_tr_ref[hd * rope_hi:(hd + 1) * rope_hi, :], cq_tr_ref[...])
        r0 = hd * QK_PAD
        q_tr_ref[r0:r0 + rope_lo, :] = qh[0:rope_lo].astype(BF16)
        swapped = jnp.concatenate([qh[rope_mid:rope_hi], qh[rope_lo:rope_mid]], axis=0)
        roped = qh[rope_lo:rope_hi] * cos_tr_ref[...] + swapped * sin_tr_ref[...]
        q_tr_ref[r0 + rope_lo:r0 + rope_hi, :] = roped.astype(BF16)
        q_tr_ref[r0 + rope_hi:r0 + QK_PAD, :] = jnp.zeros((QK_PAD - rope_hi, qh.shape[1]), BF16)

    heads_per_chunk = chunk // NOPE_DIM
    for c in range(C_HEADS // heads_per_chunk):
        sl = slice(c * chunk, (c + 1) * chunk)
        kn = _dot(ckv_ref[...], wkn_ref[:, sl]).astype(BF16)
        for hh in range(heads_per_chunk):
            hd = c * heads_per_chunk + hh
            k_ref[hd, :, 0:NOPE_DIM] = kn[:, hh * NOPE_DIM:(hh + 1) * NOPE_DIM]
            k_ref[hd, :, NOPE_DIM:] = k_rope
        v_tr_ref[sl, :] = _dot(wv_tr_ref[sl, :], ckv_tr_ref[...]).astype(BF16)


def _mla_proj(x2d, gmix, w_in_ext, g_cq_s, g_ckv, w_uq_tr, w_kn, w_v_tr, cos_t, sin_t, cos_tr, sin_tr,
              *, seq, tm):
    t, d = x2d.shape
    n_q = C_HEADS * QK_PAD
    assert w_uq_tr.shape[0] == C_HEADS * (NOPE_DIM + ROPE_DIM)
    n_kv = w_kn.shape[1]
    tiles_per_seq = seq // tm
    assert seq % tm == 0 and w_in_ext.shape[1] == Q_LORA + KV_LORA + 2 * ROPE_DIM
    vmem = (2 * tm * d * 4 + (w_in_ext.size + w_uq_tr.size + w_kn.size + w_v_tr.size) * 2
            + 2 * 2 * tm * (V7X_LANES + ROPE_DIM) * 4
            + 2 * tm * (n_q + C_HEADS * QK_PAD + n_kv) * 2
            + tm * d * 2 + tm * (Q_LORA + 2 * KV_LORA) * 2
            + 4 * tm * w_in_ext.shape[1] * 4)
    row = lambda i: (i, 0)
    col = lambda i: (0, i)
    return pl.pallas_call(
        functools.partial(_mla_proj_kernel, chunk=KV_PROJ_COLS),
        grid=(t // tm,),
        in_specs=[pl.BlockSpec((tm, d), row),
                  _const_spec(gmix.shape), _const_spec(w_in_ext.shape), _const_spec(g_cq_s.shape),
                  _const_spec(g_ckv.shape), _const_spec(w_uq_tr.shape), _const_spec(w_kn.shape),
                  _const_spec(w_v_tr.shape),
                  pl.BlockSpec((tm, V7X_LANES), lambda i: (i % tiles_per_seq, 0)),
                  pl.BlockSpec((tm, V7X_LANES), lambda i: (i % tiles_per_seq, 0)),
                  pl.BlockSpec((ROPE_DIM, tm), lambda i: (0, i % tiles_per_seq)),
                  pl.BlockSpec((ROPE_DIM, tm), lambda i: (0, i % tiles_per_seq))],
        out_specs=[pl.BlockSpec((n_q, tm), col), pl.BlockSpec((C_HEADS, tm, QK_PAD), lambda i: (0, i, 0)),
                   pl.BlockSpec((n_kv, tm), col)],
        out_shape=[jax.ShapeDtypeStruct((n_q, t), BF16), jax.ShapeDtypeStruct((C_HEADS, t, QK_PAD), BF16),
                   jax.ShapeDtypeStruct((n_kv, t), BF16)],
        scratch_shapes=[pltpu.VMEM((tm, d), BF16), pltpu.VMEM((Q_LORA, tm), BF16),
                        pltpu.VMEM((tm, KV_LORA), BF16), pltpu.VMEM((KV_LORA, tm), BF16)],
        compiler_params=_params(("parallel",), vmem),
        name="mla_proj",
    )(x2d, gmix, w_in_ext, g_cq_s, g_ckv, w_uq_tr, w_kn, w_v_tr, cos_t, sin_t, cos_tr, sin_tr)


def _attn_kernel(*refs, tk, tq, skew, n_cast):
    q_tr_ref, k_ref, v_tr_ref = refs[0:3]
    cast_in = refs[3:3 + n_cast]
    o_ref = refs[3 + n_cast]
    cast_out = refs[4 + n_cast:4 + 2 * n_cast]
    vaug_ref, m_ref, acc_ref = refs[4 + 2 * n_cast:]
    for src_ref, dst_ref in zip(cast_in, cast_out):
        dst_ref[...] = src_ref[...].astype(BF16)
    seq = k_ref.shape[0]
    vaug_ref[0:V_DIM, :] = v_tr_ref[...]
    vaug_ref[V_DIM:, :] = jnp.ones((vaug_ref.shape[0] - V_DIM, seq), BF16)

    units = []
    for j in range(seq // tk):
        for i in range(seq // tq):
            k_lo, q_lo = j * tk, i * tq
            k_hi = min(k_lo + tk, q_lo + tq)
            if k_hi > k_lo:
                units.append((j == 0, k_lo, k_hi, q_lo))

    def scores(unit):
        _, k_lo, k_hi, q_lo = unit
        s = _dot(k_ref[k_lo:k_hi, :], q_tr_ref[:, q_lo:q_lo + tq])
        if k_hi > q_lo + CHUNK:
            kc = (lax.broadcasted_iota(jnp.int32, s.shape, 0) + k_lo) // CHUNK
            qc = (lax.broadcasted_iota(jnp.int32, s.shape, 1) + q_lo) // CHUNK
            s = jnp.where(kc <= qc, s, -jnp.inf)
        return s

    pending = [scores(u) for u in units[:skew]]
    for n, (first, k_lo, k_hi, q_lo) in enumerate(units):
        if n + skew < len(units):
            pending.append(scores(units[n + skew]))
        s = pending.pop(0)
        qs = slice(q_lo, q_lo + tq)
        v_j = vaug_ref[:, k_lo:k_hi]
        m_blk = jnp.max(s, axis=0, keepdims=True)
        if first:
            m_new = m_blk
            acc_ref[:, qs] = _dot(v_j, jnp.exp2(s - m_new).astype(BF16))
        else:
            m_prev = m_ref[:, qs]
            m_new = jnp.maximum(m_prev, m_blk)
            alpha = jnp.exp2(m_prev - m_new)
            acc_ref[:, qs] = alpha * acc_ref[:, qs] + _dot(v_j, jnp.exp2(s - m_new).astype(BF16))
        m_ref[:, qs] = m_new

    for i in range(seq // tq):
        qs = slice(i * tq, (i + 1) * tq)
        o_ref[qs, :] = (acc_ref[0:V_DIM, qs] / acc_ref[V_DIM:V_DIM + 1, qs]).T.astype(o_ref.dtype)


def _attention(q_tr, k, v_tr, *, batch, seq, tk, tq, skew, cast_next=(), cast_layer=None):
    t = k.shape[1]
    steps = batch * C_HEADS
    assert seq % tk == 0 and seq % tq == 0 and tk % CHUNK == 0 and tq % CHUNK == 0
    assert V_DIM == V7X_LANES and NOPE_DIM == V7X_LANES
    vmem = (2 * (2 * QK_PAD + V_DIM) * seq * 2 + 2 * seq * V_DIM * 2
            + (V_DIM + ONES_ROWS) * seq * 6 + 8 * seq * 4 + 24 * tk * tq * 4)
    in_specs = [pl.BlockSpec((QK_PAD, seq), lambda b, h: (h, b)),
                pl.BlockSpec((None, seq, QK_PAD), lambda b, h: (h, b, 0)),
                pl.BlockSpec((V_DIM, seq), lambda b, h: (h, b))]
    out_specs = [pl.BlockSpec((None, seq, V_DIM), lambda b, h: (h, b, 0))]
    out_shape = [jax.ShapeDtypeStruct((C_HEADS, t, V_DIM), BF16)]
    for w in cast_next:
        _, r, c = w.shape
        assert r % (steps * 16) == 0
        in_specs.append(pl.BlockSpec((None, r // steps, c), lambda b, h: (cast_layer, b * C_HEADS + h, 0)))
        out_specs.append(pl.BlockSpec((r // steps, c), lambda b, h: (b * C_HEADS + h, 0)))
        out_shape.append(jax.ShapeDtypeStruct((r, c), BF16))
        vmem += 2 * (r // steps) * c * 6
    res = pl.pallas_call(
        functools.partial(_attn_kernel, tk=tk, tq=tq, skew=skew, n_cast=len(cast_next)),
        grid=(batch, C_HEADS),
        in_specs=in_specs,
        out_specs=out_specs,
        out_shape=out_shape,
        scratch_shapes=[pltpu.VMEM((V_DIM + ONES_ROWS, seq), BF16),
                        pltpu.VMEM((1, seq), F32),
                        pltpu.VMEM((V_DIM + ONES_ROWS, seq), F32)],
        compiler_params=_params(("parallel", "parallel"), vmem),
        name="attention",
    )(q_tr, k, v_tr, *cast_next)
    return res if cast_next else res[0]


def _proj_residual_kernel(x_ref, a_ref, w_ref, o_ref):
    a = jnp.concatenate([a_ref[hd] for hd in range(a_ref.shape[0])], axis=1)
    o_ref[...] = x_ref[...] + _dot(a, w_ref[...])


def _proj_residual(x2d, a, w, *, tm):
    t, d = x2d.shape
    heads, _, hd_dim = a.shape
    k = heads * hd_dim
    vmem = 4 * tm * d * 4 + 2 * tm * k * 2 + w.size * 2 + 2 * tm * d * 4
    return pl.pallas_call(
        _proj_residual_kernel,
        grid=(t // tm,),
        in_specs=[pl.BlockSpec((tm, d), lambda i: (i, 0)), pl.BlockSpec((heads, tm, hd_dim), lambda i: (0, i, 0)),
                  _const_spec(w.shape)],
        out_specs=pl.BlockSpec((tm, d), lambda i: (i, 0)),
        out_shape=jax.ShapeDtypeStruct((t, d), F32),
        compiler_params=_params(("parallel",), vmem),
        name="attn_out_proj",
    )(x2d, a, w)


def _swap_halves(w):
    half = w.shape[-1] // 2
    return jnp.concatenate([w[..., half:], w[..., :half]], axis=-1)


def kernel(x, g_mix, g_ffn, g_final, w_in_ab, g_v, w_s, b_s, w_pool, pool_scale, w_out_ab, w_in_c, g_cq, g_ckv, w_uq, w_ukv, w_out_c, w_gate, w_up, w_down):
    batch, seq, d = x.shape
    x2d = x.reshape(batch * seq, d)
    row = lambda a: a.reshape(1, -1)

    b_s_b = jnp.broadcast_to(b_s[0][:, :, None], (A_HEADS, GMLP_BLOCK, GMLP_BLOCK))
    x1 = _l0_mixer(x2d, row(g_mix[0]), w_in_ab[0].astype(BF16), row(g_v[0]), w_s[0], b_s_b,
                   w_pool[0].astype(BF16), row(pool_scale[0]), w_out_ab[0].astype(BF16),
                   seq=seq, tm=PROJ_ROWS)
    x2_head, wg0, wu0, wd0 = _ffn(x1, row(g_ffn[0]), w_gate, w_up, w_down, layer=0, tm=FFN_ROWS,
                                  tf=FFN_HIDDEN_F32, n_tiles=1, in_place=True, emit_bf16=True)
    x2 = _ffn(x2_head, row(g_ffn[0]), wg0, wu0, wd0, tm=FFN_ROWS, tf=FFN_HIDDEN, tile_lo=1, in_place=True)

    inv_freq = ROPE_THETA ** (-jnp.arange(0, ROPE_DIM, 2, dtype=F32) / ROPE_DIM)
    cos_t, sin_t, cos_tr, sin_tr = _rope_tables(
        jnp.tile(inv_freq, V7X_LANES // inv_freq.shape[0]).reshape(1, -1), seq, PROJ_ROWS)
    wc = w_in_c[0]
    w_kr = wc[:, Q_LORA + KV_LORA:]
    w_in_ext = jnp.concatenate([wc, _swap_halves(w_kr)], axis=1).astype(BF16)
    w_uq_tr = w_uq[0].T.astype(BF16)
    wkv = w_ukv[0].reshape(KV_LORA, C_HEADS, NOPE_DIM + V_DIM)
    w_kn = wkv[..., :NOPE_DIM].reshape(KV_LORA, -1).astype(BF16)
    w_v_tr = wkv[..., NOPE_DIM:].reshape(KV_LORA, -1).T.astype(BF16)
    score_scale = (NOPE_DIM + ROPE_DIM) ** -0.5 * LOG2_E
    q_tr, k, v_tr = _mla_proj(x2, row(g_mix[1]), w_in_ext, row(g_cq[0]) * score_scale,
                              row(g_ckv[0]), w_uq_tr, w_kn, w_v_tr, cos_t, sin_t, cos_tr, sin_tr,
                              seq=seq, tm=PROJ_ROWS)
    o, w_gate1, w_up1, w_down1 = _attention(q_tr, k, v_tr, batch=batch, seq=seq, tk=ATTN_KEYS, tq=ATTN_QUERIES,
                                            skew=ATTN_LOOKAHEAD, cast_next=(w_gate, w_up, w_down), cast_layer=1)
    x3 = _proj_residual(x2, o, w_out_c[0].astype(BF16), tm=PROJ_ROWS)
    out = _ffn(x3, row(g_ffn[1]), w_gate1, w_up1, w_down1, tm=FFN_ROWS, tf=FFN_HIDDEN, g_final=row(g_final))
    return out.reshape(batch, seq, d)
```

```python
import functools

import jax
import jax.numpy as jnp
from jax import lax
from jax.experimental import pallas as pl
from jax.experimental.pallas import tpu as pltpu

F32 = jnp.float32
BF16 = jnp.bfloat16

CHUNK = 64
EPS = 1e-6
GMLP_BLOCK = 128
A_HEADS = 8
POOL_WINDOWS = (2, 4, 8, 16)
POOL_HALO = 16
C_HEADS = 16
Q_LORA = 512
KV_LORA = 512
NOPE_DIM = 128
ROPE_DIM = 64
V_DIM = 128
ROPE_THETA = 10000.0
QK_PAD = 256
ONES_ROWS = 16

V7X_LANES = 128
V7X_MXU_COLS = 256
V7X_SCOPED_VMEM_BYTES = 60000 * 1024

PROJ_ROWS = 512
FFN_ROWS = 1024
FFN_HIDDEN_F32 = 256
FFN_HIDDEN = 512
KV_PROJ_COLS = 512
ATTN_KEYS = 512
ATTN_QUERIES = 256
ATTN_LOOKAHEAD = 3
LOG2_E = 1.4426950408889634


def _rms(xf, g):
    ms = jnp.mean(xf * xf, axis=-1, keepdims=True)
    return xf * lax.rsqrt(ms + EPS) * g


def _gelu_tanh(x):
    c = 0.7978845608028654
    return 0.5 * x * (1.0 + jnp.tanh(c * (x + 0.044715 * (x * x * x))))


def _dot(a, b):
    return jnp.dot(a, b, preferred_element_type=F32)


def _const_spec(shape):
    nd = len(shape)
    return pl.BlockSpec(shape, lambda *_: (0,) * nd, pipeline_mode=pl.Buffered(1))


def _params(semantics, vmem_bytes):
    return pltpu.CompilerParams(dimension_semantics=semantics,
                                vmem_limit_bytes=min(int(vmem_bytes), V7X_SCOPED_VMEM_BYTES))


def _rope_table_kernel(invf_ref, cos_ref, sin_ref, cos_tr_ref, sin_tr_ref, *, ts):
    i = pl.program_id(0)
    pos = (lax.broadcasted_iota(jnp.int32, (ts, V7X_LANES), 0) + i * ts).astype(F32)
    lane = lax.broadcasted_iota(jnp.int32, (ts, V7X_LANES), 1)
    ang = pos * invf_ref[...]
    c = jnp.cos(ang)
    s = jnp.sin(ang)
    half = ROPE_DIM // 2
    cos_t = jnp.where(lane < ROPE_DIM, c, 0.0)
    sin_t = jnp.where(lane < half, -s, jnp.where(lane < ROPE_DIM, s, 0.0))
    cos_ref[...] = cos_t
    sin_ref[...] = sin_t
    cos_tr_ref[...] = cos_t.T[0:ROPE_DIM]
    sin_tr_ref[...] = sin_t.T[0:ROPE_DIM]


def _rope_tables(invf_row, seq, ts):
    return pl.pallas_call(
        functools.partial(_rope_table_kernel, ts=ts),
        grid=(seq // ts,),
        in_specs=[pl.BlockSpec((1, V7X_LANES), lambda i: (0, 0))],
        out_specs=[pl.BlockSpec((ts, V7X_LANES), lambda i: (i, 0)),
                   pl.BlockSpec((ts, V7X_LANES), lambda i: (i, 0)),
                   pl.BlockSpec((ROPE_DIM, ts), lambda i: (0, i)),
                   pl.BlockSpec((ROPE_DIM, ts), lambda i: (0, i))],
        out_shape=[jax.ShapeDtypeStruct((seq, V7X_LANES), F32)] * 2
                  + [jax.ShapeDtypeStruct((ROPE_DIM, seq), F32)] * 2,
        compiler_params=_params(("parallel",), 16 * ts * V7X_LANES * 4 + (4 << 20)),
        name="rope_tables",
    )(invf_row)


def _l0_mixer_kernel(x_ref, gmix_ref, win_ref, gv_ref, ws_ref, bs_ref, wpool_ref, psc_ref, wout_ref,
                     o_ref, h_ref, u_ref, v_ref, z_ref, cat_ref, *, tm, tiles_per_seq, d_a, d_g):
    i = pl.program_id(0)
    seq_tile = i % tiles_per_seq
    nb = tm // GMLP_BLOCK

    @pl.when(seq_tile == 0)
    def _():
        z_ref[0:POOL_HALO, :] = jnp.zeros((POOL_HALO, z_ref.shape[1]), F32)

    h_ref[...] = _rms(x_ref[...], gmix_ref[...]).astype(BF16)

    pre_u = _dot(h_ref[...], win_ref[:, 0:d_a])
    pre_v = _dot(h_ref[...], win_ref[:, d_a:2 * d_a])
    z_ref[POOL_HALO:POOL_HALO + tm, :] = _dot(h_ref[...], win_ref[:, 2 * d_a:])

    u_ref[...] = _gelu_tanh(pre_u)
    v_ref[...] = _rms(_gelu_tanh(pre_v), gv_ref[...]).astype(BF16)

    tt = lax.broadcasted_iota(jnp.int32, (GMLP_BLOCK, GMLP_BLOCK), 0)
    ss = lax.broadcasted_iota(jnp.int32, (GMLP_BLOCK, GMLP_BLOCK), 1)
    tri = (ss // CHUNK <= tt // CHUNK).astype(F32)
    hd_w = d_a // A_HEADS
    for hd in range(A_HEADS):
        c0 = hd * hd_w
        wm = (ws_ref[hd] * tri).astype(BF16)
        rhs = jnp.concatenate(
            [v_ref[n * GMLP_BLOCK:(n + 1) * GMLP_BLOCK, c0:c0 + hd_w] for n in range(nb)], axis=1)
        mix = _dot(wm, rhs)
        bias = bs_ref[hd]
        for n in range(nb):
            r0 = n * GMLP_BLOCK
            a = u_ref[r0:r0 + GMLP_BLOCK, c0:c0 + hd_w] * (mix[:, n * hd_w:(n + 1) * hd_w] + bias)
            cat_ref[r0:r0 + GMLP_BLOCK, c0:c0 + hd_w] = a.astype(BF16)

    pos1 =lax.broadcasted_iota(jnp.int32, (tm, d_g), 0) + (seq_tile * tm + 1)
    for g, win in enumerate(POOL_WINDOWS):
        zz = z_ref[:, g * d_g:(g + 1) * d_g]
        s = zz
        k = 1
        while k < win:
            s = s + pltpu.roll(s, k, axis=0)
            k *= 2
        cnt = jnp.minimum(pos1, win).astype(F32)
        d = (s[POOL_HALO:] / cnt - zz[POOL_HALO:]).astype(BF16)
        y = _dot(d, wpool_ref[g]) * psc_ref[:, g * d_g:(g + 1) * d_g]
        cat_ref[:, d_a + g * d_g:d_a + (g + 1) * d_g] = y.astype(BF16)
    z_ref[0:POOL_HALO, :] = z_ref[tm:tm + POOL_HALO, :]

    o_ref[...] = x_ref[...] + _dot(cat_ref[...], wout_ref[...])


def _l0_mixer(x2d, gmix, w_in, g_v, w_s, b_s_b, w_pool, pscale, w_out, *, seq, tm):
    t, d = x2d.shape
    n_in = w_in.shape[1]
    d_a = g_v.shape[1]
    d_b = pscale.shape[1]
    d_g = d_b // len(POOL_WINDOWS)
    assert n_in == 2 * d_a + d_b and seq % tm == 0 and tm % GMLP_BLOCK == 0
    vmem = (4 * tm * d * 4
            + w_in.size * 2 + w_out.size * 2 + w_pool.size * 2 + (w_s.size + b_s_b.size) * 4
            + tm * d * 2 + tm * d_a * 4 + tm * d_a * 2 + (tm + POOL_HALO) * d_b * 4 + tm * (d_a + d_b) * 2
            + 4 * tm * d_a * 4)
    return pl.pallas_call(
        functools.partial(_l0_mixer_kernel, tm=tm, tiles_per_seq=seq // tm, d_a=d_a, d_g=d_g),
        grid=(t // tm,),
        in_specs=[pl.BlockSpec((tm, d), lambda i: (i, 0)),
                  _const_spec(gmix.shape), _const_spec(w_in.shape), _const_spec(g_v.shape),
                  _const_spec(w_s.shape), _const_spec(b_s_b.shape), _const_spec(w_pool.shape),
                  _const_spec(pscale.shape), _const_spec(w_out.shape)],
        out_specs=pl.BlockSpec((tm, d), lambda i: (i, 0)),
        out_shape=jax.ShapeDtypeStruct((t, d), F32),
        scratch_shapes=[pltpu.VMEM((tm, d), BF16),
                        pltpu.VMEM((tm, d_a), F32),
                        pltpu.VMEM((tm, d_a), BF16),
                        pltpu.VMEM((tm + POOL_HALO, d_b), F32),
                        pltpu.VMEM((tm, d_a + d_b), BF16)],
        compiler_params=_params(("arbitrary",), vmem),
        name="l0_mixer",
    )(x2d, gmix, w_in, g_v, w_s, b_s_b, w_pool, pscale, w_out)


def _ffn_kernel(*refs, final_norm, emit_bf16):
    it = iter(refs)
    x_ref, g_ref, wg_ref, wu_ref, wd_ref = next(it), next(it), next(it), next(it), next(it)
    gfin_ref = next(it) if final_norm else None
    o_ref = next(it)
    bf16_out = [next(it) for _ in range(3)] if emit_bf16 else None
    h_ref = next(it)
    f = pl.program_id(1)

    @pl.when(f == 0)
    def _():
        x = x_ref[...]
        h_ref[...] = _rms(x, g_ref[...]).astype(BF16)
        o_ref[...] = x

    wg = wg_ref[...].astype(BF16)
    wu = wu_ref[...].astype(BF16)
    wd = wd_ref[...].astype(BF16)
    if emit_bf16:
        for dst_ref, w in zip(bf16_out, (wg, wu, wd)):
            dst_ref[...] = w

    h = h_ref[...]
    tf = wg.shape[1]
    assert tf % V7X_MXU_COLS == 0
    groups = [slice(c, c + V7X_MXU_COLS) for c in range(0, tf, V7X_MXU_COLS)]
    pre = [(_dot(h, wg[:, sl]), _dot(h, wu[:, sl])) for sl in groups]
    acts = [(gate * jax.nn.sigmoid(gate) * up).astype(BF16) for gate, up in pre]
    o_ref[...] += functools.reduce(lambda a, b: a + b, [_dot(a, wd[sl, :]) for a, sl in zip(acts, groups)])

    if final_norm:
        @pl.when(f == pl.num_programs(1) - 1)
        def _():
            o_ref[...] = _rms(o_ref[...], gfin_ref[...])


def _ffn(x2d, g, w_gate, w_up, w_down, *, tm, tf, layer=None, tile_lo=0, n_tiles=None, in_place=False,
         emit_bf16=False, g_final=None):
    t, d = x2d.shape
    dff = w_gate.shape[-1]
    n_tiles = t // tm - tile_lo if n_tiles is None else n_tiles
    nf = dff // tf
    assert t % tm == 0 and dff % tf == 0 and (not emit_bf16 or n_tiles == 1)
    final_norm = g_final is not None
    wbytes = w_gate.dtype.itemsize

    def wspec(w, blk, imap):
        if w.ndim == 2:
            return pl.BlockSpec(blk, imap)
        return pl.BlockSpec((None,) + blk, lambda i, f: (layer,) + imap(i, f))

    rows = lambda i, f: (i + tile_lo, 0)
    gate_cols = lambda i, f: (0, f)
    down_rows = lambda i, f: (f, 0)
    args = [x2d, g, w_gate, w_up, w_down]
    in_specs = [pl.BlockSpec((tm, d), rows), _const_spec(g.shape),
                wspec(w_gate, (d, tf), gate_cols), wspec(w_up, (d, tf), gate_cols),
                wspec(w_down, (tf, d), down_rows)]
    vmem = 4 * tm * d * 4 + tm * d * 2 + 2 * 3 * d * tf * wbytes + 3 * d * tf * 2 + 4 * tm * tf * 4
    if final_norm:
        args.append(g_final)
        in_specs.append(_const_spec(g_final.shape))
    out_specs = [pl.BlockSpec((tm, d), rows)]
    out_shape = [jax.ShapeDtypeStruct((t, d), F32)]
    if emit_bf16:
        out_specs += [pl.BlockSpec((d, tf), gate_cols), pl.BlockSpec((d, tf), gate_cols),
                      pl.BlockSpec((tf, d), down_rows)]
        out_shape += [jax.ShapeDtypeStruct((d, dff), BF16)] * 2 + [jax.ShapeDtypeStruct((dff, d), BF16)]
        vmem += 2 * 3 * d * tf * 2
    res = pl.pallas_call(
        functools.partial(_ffn_kernel, final_norm=final_norm, emit_bf16=emit_bf16),
        grid=(n_tiles, nf),
        in_specs=in_specs,
        out_specs=out_specs,
        out_shape=out_shape,
        input_output_aliases={0: 0} if in_place else {},
        scratch_shapes=[pltpu.VMEM((tm, d), BF16)],
        compiler_params=_params(("parallel", "arbitrary"), vmem),
        name="ffn_final" if final_norm else ("ffn_head" if emit_bf16 else "ffn"),
    )(*args)
    return res if emit_bf16 else res[0]


def _mla_proj_kernel(x_ref, gmix_ref, win_ref, gcq_ref, gckv_ref, wuq_tr_ref, wkn_ref, wv_tr_ref,
                     cos_ref, sin_ref, cos_tr_ref, sin_tr_ref,
                     q_tr_ref, k_ref, v_tr_ref, h_ref, cq_tr_ref, ckv_ref, ckv_tr_ref, *, chunk):
    h_ref[...] = _rms(x_ref[...], gmix_ref[...]).astype(BF16)
    p = _dot(h_ref[...], win_ref[...])
    cq_tr_ref[...] = _rms(p[:, 0:Q_LORA], gcq_ref[...]).T.astype(BF16)
    ckv = _rms(p[:, Q_LORA:Q_LORA + KV_LORA], gckv_ref[...])
    ckv_ref[...] = ckv.astype(BF16)
    ckv_tr_ref[...] = ckv.T.astype(BF16)
    a = p[:, Q_LORA + KV_LORA:]
    k_rope = (a * cos_ref[...] + pltpu.roll(a, ROPE_DIM, axis=1) * sin_ref[...]).astype(BF16)

    rope_lo, rope_mid, rope_hi = NOPE_DIM, NOPE_DIM + ROPE_DIM // 2, NOPE_DIM + ROPE_DIM
    for hd in range(C_HEADS):
        qh = _dot(wuq_tr_ref[hd * rope_hi:(hd + 1) * rope_hi, :], cq_tr_ref[...])
        r0 = hd * QK_PAD
        q_tr_ref[r0:r0 + rope_lo, :] = qh[0:rope_lo].astype(BF16)
        swapped = jnp.concatenate([qh[rope_mid:rope_hi], qh[rope_lo:rope_mid]], axis=0)
        roped = qh[rope_lo:rope_hi] * cos_tr_ref[...] + swapped * sin_tr_ref[...]
        q_tr_ref[r0 + rope_lo:r0 + rope_hi, :] = roped.astype(BF16)
        q_tr_ref[r0 + rope_hi:r0 + QK_PAD, :] = jnp.zeros((QK_PAD - rope_hi, qh.shape[1]), BF16)

    heads_per_chunk = chunk // NOPE_DIM
    for c in range(C_HEADS // heads_per_chunk):
        sl = slice(c * chunk, (c + 1) * chunk)
        kn = _dot(ckv_ref[...], wkn_ref[:, sl]).astype(BF16)
        for hh in range(heads_per_chunk):
            hd = c * heads_per_chunk + hh
            k_ref[hd, :, 0:NOPE_DIM] = kn[:, hh * NOPE_DIM:(hh + 1) * NOPE_DIM]
            k_ref[hd, :, NOPE_DIM:] = k_rope
        v_tr_ref[sl, :] = _dot(wv_tr_ref[sl, :], ckv_tr_ref[...]).astype(BF16)


def _mla_proj(x2d, gmix, w_in_ext, g_cq_s, g_ckv, w_uq_tr, w_kn, w_v_tr, cos_t, sin_t, cos_tr, sin_tr,
              *, seq, tm):
    t, d = x2d.shape
    n_q = C_HEADS * QK_PAD
    assert w_uq_tr.shape[0] == C_HEADS * (NOPE_DIM + ROPE_DIM)
    n_kv = w_kn.shape[1]
    tiles_per_seq = seq // tm
    assert seq % tm == 0 and w_in_ext.shape[1] == Q_LORA + KV_LORA + 2 * ROPE_DIM
    vmem = (2 * tm * d * 4 + (w_in_ext.size + w_uq_tr.size + w_kn.size + w_v_tr.size) * 2
            + 2 * 2 * tm * (V7X_LANES + ROPE_DIM) * 4
            + 2 * tm * (n_q + C_HEADS * QK_PAD + n_kv) * 2
            + tm * d * 2 + tm * (Q_LORA + 2 * KV_LORA) * 2
            + 4 * tm * w_in_ext.shape[1] * 4)
    row = lambda i: (i, 0)
    col = lambda i: (0, i)
    return pl.pallas_call(
        functools.partial(_mla_proj_kernel, chunk=KV_PROJ_COLS),
        grid=(t // tm,),
        in_specs=[pl.BlockSpec((tm, d), row),
                  _const_spec(gmix.shape), _const_spec(w_in_ext.shape), _const_spec(g_cq_s.shape),
                  _const_spec(g_ckv.shape), _const_spec(w_uq_tr.shape), _const_spec(w_kn.shape),
                  _const_spec(w_v_tr.shape),
                  pl.BlockSpec((tm, V7X_LANES), lambda i: (i % tiles_per_seq, 0)),
                  pl.BlockSpec((tm, V7X_LANES), lambda i: (i % tiles_per_seq, 0)),
                  pl.BlockSpec((ROPE_DIM, tm), lambda i: (0, i % tiles_per_seq)),
                  pl.BlockSpec((ROPE_DIM, tm), lambda i: (0, i % tiles_per_seq))],
        out_specs=[pl.BlockSpec((n_q, tm), col), pl.BlockSpec((C_HEADS, tm, QK_PAD), lambda i: (0, i, 0)),
                   pl.BlockSpec((n_kv, tm), col)],
        out_shape=[jax.ShapeDtypeStruct((n_q, t), BF16), jax.ShapeDtypeStruct((C_HEADS, t, QK_PAD), BF16),
                   jax.ShapeDtypeStruct((n_kv, t), BF16)],
        scratch_shapes=[pltpu.VMEM((tm, d), BF16), pltpu.VMEM((Q_LORA, tm), BF16),
                        pltpu.VMEM((tm, KV_LORA), BF16), pltpu.VMEM((KV_LORA, tm), BF16)],
        compiler_params=_params(("parallel",), vmem),
        name="mla_proj",
    )(x2d, gmix, w_in_ext, g_cq_s, g_ckv, w_uq_tr, w_kn, w_v_tr, cos_t, sin_t, cos_tr, sin_tr)


def _attn_kernel(*refs, tk, tq, skew, n_cast):
    q_tr_ref, k_ref, v_tr_ref = refs[0:3]
    cast_in = refs[3:3 + n_cast]
    o_ref = refs[3 + n_cast]
    cast_out = refs[4 + n_cast:4 + 2 * n_cast]
    vaug_ref, m_ref, acc_ref = refs[4 + 2 * n_cast:]
    for src_ref, dst_ref in zip(cast_in, cast_out):
        dst_ref[...] = src_ref[...].astype(BF16)
    seq = k_ref.shape[0]
    vaug_ref[0:V_DIM, :] = v_tr_ref[...]
    vaug_ref[V_DIM:, :] = jnp.ones((vaug_ref.shape[0] - V_DIM, seq), BF16)

    units = []
    for j in range(seq // tk):
        for i in range(seq // tq):
            k_lo, q_lo = j * tk, i * tq
            k_hi = min(k_lo + tk, q_lo + tq)
            if k_hi > k_lo:
                units.append((j == 0, k_lo, k_hi, q_lo))

    def scores(unit):
        _, k_lo, k_hi, q_lo = unit
        s = _dot(k_ref[k_lo:k_hi, :], q_tr_ref[:, q_lo:q_lo + tq])
        if k_hi > q_lo + CHUNK:
            kc = (lax.broadcasted_iota(jnp.int32, s.shape, 0) + k_lo) // CHUNK
            qc = (lax.broadcasted_iota(jnp.int32, s.shape, 1) + q_lo) // CHUNK
            s = jnp.where(kc <= qc, s, -jnp.inf)
        return s

    pending = [scores(u) for u in units[:skew]]
    for n, (first, k_lo, k_hi, q_lo) in enumerate(units):
        if n + skew < len(units):
            pending.append(scores(units[n + skew]))
        s = pending.pop(0)
        qs = slice(q_lo, q_lo + tq)
        v_j = vaug_ref[:, k_lo:k_hi]
        m_blk = jnp.max(s, axis=0, keepdims=True)
        if first:
            m_new = m_blk
            acc_ref[:, qs] = _dot(v_j, jnp.exp2(s - m_new).astype(BF16))
        else:
            m_prev = m_ref[:, qs]
            m_new = jnp.maximum(m_prev, m_blk)
            alpha = jnp.exp2(m_prev - m_new)
            acc_ref[:, qs] = alpha * acc_ref[:, qs] + _dot(v_j, jnp.exp2(s - m_new).astype(BF16))
        m_ref[:, qs] = m_new

    for i in range(seq // tq):
        qs = slice(i * tq, (i + 1) * tq)
        o_ref[qs, :] = (acc_ref[0:V_DIM, qs] / acc_ref[V_DIM:V_DIM + 1, qs]).T.astype(o_ref.dtype)


def _attention(q_tr, k, v_tr, *, batch, seq, tk, tq, skew, cast_next=(), cast_layer=None):
    t = k.shape[1]
    steps = batch * C_HEADS
    assert seq % tk == 0 and seq % tq == 0 and tk % CHUNK == 0 and tq % CHUNK == 0
    assert V_DIM == V7X_LANES and NOPE_DIM == V7X_LANES
    vmem = (2 * (2 * QK_PAD + V_DIM) * seq * 2 + 2 * seq * V_DIM * 2
            + (V_DIM + ONES_ROWS) * seq * 6 + 8 * seq * 4 + 24 * tk * tq * 4)
    in_specs = [pl.BlockSpec((QK_PAD, seq), lambda b, h: (h, b)),
                pl.BlockSpec((None, seq, QK_PAD), lambda b, h: (h, b, 0)),
                pl.BlockSpec((V_DIM, seq), lambda b, h: (h, b))]
    out_specs = [pl.BlockSpec((None, seq, V_DIM), lambda b, h: (h, b, 0))]
    out_shape = [jax.ShapeDtypeStruct((C_HEADS, t, V_DIM), BF16)]
    for w in cast_next:
        _, r, c = w.shape
        assert r % (steps * 16) == 0
        in_specs.append(pl.BlockSpec((None, r // steps, c), lambda b, h: (cast_layer, b * C_HEADS + h, 0)))
        out_specs.append(pl.BlockSpec((r // steps, c), lambda b, h: (b * C_HEADS + h, 0)))
        out_shape.append(jax.ShapeDtypeStruct((r, c), BF16))
        vmem += 2 * (r // steps) * c * 6
    res = pl.pallas_call(
        functools.partial(_attn_kernel, tk=tk, tq=tq, skew=skew, n_cast=len(cast_next)),
        grid=(batch, C_HEADS),
        in_specs=in_specs,
        out_specs=out_specs,
        out_shape=out_shape,
        scratch_shapes=[pltpu.VMEM((V_DIM + ONES_ROWS, seq), BF16),
                        pltpu.VMEM((1, seq), F32),
                        pltpu.VMEM((V_DIM + ONES_ROWS, seq), F32)],
        compiler_params=_params(("parallel", "parallel"), vmem),
        name="attention",
    )(q_tr, k, v_tr, *cast_next)
    return res if cast_next else res[0]


def _proj_residual_kernel(x_ref, a_ref, w_ref, o_ref):
    a = jnp.concatenate([a_ref[hd] for hd in range(a_ref.shape[0])], axis=1)
    o_ref[...] = x_ref[...] + _dot(a, w_ref[...])


def _proj_residual(x2d, a, w, *, tm):
    t, d = x2d.shape
    heads, _, hd_dim = a.shape
    k = heads * hd_dim
    vmem = 4 * tm * d * 4 + 2 * tm * k * 2 + w.size * 2 + 2 * tm * d * 4
    return pl.pallas_call(
        _proj_residual_kernel,
        grid=(t // tm,),
        in_specs=[pl.BlockSpec((tm, d), lambda i: (i, 0)), pl.BlockSpec((heads, tm, hd_dim), lambda i: (0, i, 0)),
                  _const_spec(w.shape)],
        out_specs=pl.BlockSpec((tm, d), lambda i: (i, 0)),
        out_shape=jax.ShapeDtypeStruct((t, d), F32),
        compiler_params=_params(("parallel",), vmem),
        name="attn_out_proj",
    )(x2d, a, w)


def _swap_halves(w):
    half = w.shape[-1] // 2
    return jnp.concatenate([w[..., half:], w[..., :half]], axis=-1)


def kernel(x, g_mix, g_ffn, g_final, w_in_ab, g_v, w_s, b_s, w_pool, pool_scale, w_out_ab, w_in_c, g_cq, g_ckv, w_uq, w_ukv, w_out_c, w_gate, w_up, w_down):
    batch, seq, d = x.shape
    x2d = x.reshape(batch * seq, d)
    row = lambda a: a.reshape(1, -1)

    b_s_b = jnp.broadcast_to(b_s[0][:, :, None], (A_HEADS, GMLP_BLOCK, GMLP_BLOCK))
    x1 = _l0_mixer(x2d, row(g_mix[0]), w_in_ab[0].astype(BF16), row(g_v[0]), w_s[0], b_s_b,
                   w_pool[0].astype(BF16), row(pool_scale[0]), w_out_ab[0].astype(BF16),
                   seq=seq, tm=PROJ_ROWS)
    x2_head, wg0, wu0, wd0 = _ffn(x1, row(g_ffn[0]), w_gate, w_up, w_down, layer=0, tm=FFN_ROWS,
                                  tf=FFN_HIDDEN_F32, n_tiles=1, in_place=True, emit_bf16=True)
    x2 = _ffn(x2_head, row(g_ffn[0]), wg0, wu0, wd0, tm=FFN_ROWS, tf=FFN_HIDDEN, tile_lo=1, in_place=True)

    inv_freq = ROPE_THETA ** (-jnp.arange(0, ROPE_DIM, 2, dtype=F32) / ROPE_DIM)
    cos_t, sin_t, cos_tr, sin_tr = _rope_tables(
        jnp.tile(inv_freq, V7X_LANES // inv_freq.shape[0]).reshape(1, -1), seq, PROJ_ROWS)
    wc = w_in_c[0]
    w_kr = wc[:, Q_LORA + KV_LORA:]
    w_in_ext = jnp.concatenate([wc, _swap_halves(w_kr)], axis=1).astype(BF16)
    w_uq_tr = w_uq[0].T.astype(BF16)
    wkv = w_ukv[0].reshape(KV_LORA, C_HEADS, NOPE_DIM + V_DIM)
    w_kn = wkv[..., :NOPE_DIM].reshape(KV_LORA, -1).astype(BF16)
    w_v_tr = wkv[..., NOPE_DIM:].reshape(KV_LORA, -1).T.astype(BF16)
    score_scale = (NOPE_DIM + ROPE_DIM) ** -0.5 * LOG2_E
    q_tr, k, v_tr = _mla_proj(x2, row(g_mix[1]), w_in_ext, row(g_cq[0]) * score_scale,
                              row(g_ckv[0]), w_uq_tr, w_kn, w_v_tr, cos_t, sin_t, cos_tr, sin_tr,
                              seq=seq, tm=PROJ_ROWS)
    o, w_gate1, w_up1, w_down1 = _attention(q_tr, k, v_tr, batch=batch, seq=seq, tk=ATTN_KEYS, tq=ATTN_QUERIES,
                                            skew=ATTN_LOOKAHEAD, cast_next=(w_gate, w_up, w_down), cast_layer=1)
    x3 = _proj_residual(x2, o, w_out_c[0].astype(BF16), tm=PROJ_ROWS)
    out = _ffn(x3, row(g_ffn[1]), w_gate1, w_up1, w_down1, tm=FFN_ROWS, tf=FFN_HIDDEN, g_final=row(g_final))
    return out.reshape(batch, seq, d)
```

```python
import functools

import jax
import jax.numpy as jnp
from jax import lax
from jax.experimental import pallas as pl
from jax.experimental.pallas import tpu as pltpu

F32 = jnp.float32
BF16 = jnp.bfloat16

CHUNK = 64
EPS = 1e-6
GMLP_BLOCK = 128
A_HEADS = 8
POOL_WINDOWS = (2, 4, 8, 16)
POOL_HALO = 16
C_HEADS = 16
Q_LORA = 512
KV_LORA = 512
NOPE_DIM = 128
ROPE_DIM = 64
V_DIM = 128
ROPE_THETA = 10000.0
QK_PAD = 256
ONES_ROWS = 16

V7X_LANES = 128
V7X_MXU_COLS = 256
V7X_SCOPED_VMEM_BYTES = 60000 * 1024

PROJ_ROWS = 512
FFN_ROWS = 1024
FFN_HIDDEN_F32 = 256
FFN_HIDDEN = 512
KV_PROJ_COLS = 512
ATTN_KEYS = 512
ATTN_QUERIES = 256
ATTN_LOOKAHEAD = 3
LOG2_E = 1.4426950408889634


def _rms(xf, g):
    ms = jnp.mean(xf * xf, axis=-1, keepdims=True)
    return xf * lax.rsqrt(ms + EPS) * g


def _gelu_tanh(x):
    c = 0.7978845608028654
    return 0.5 * x * (1.0 + jnp.tanh(c * (x + 0.044715 * (x * x * x))))


def _dot(a, b):
    return jnp.dot(a, b, preferred_element_type=F32)


def _const_spec(shape):
    nd = len(shape)
    return pl.BlockSpec(shape, lambda *_: (0,) * nd, pipeline_mode=pl.Buffered(1))


def _params(semantics, vmem_bytes):
    return pltpu.CompilerParams(dimension_semantics=semantics,
                                vmem_limit_bytes=min(int(vmem_bytes), V7X_SCOPED_VMEM_BYTES))


def _rope_table_kernel(invf_ref, cos_ref, sin_ref, cos_tr_ref, sin_tr_ref, *, ts):
    i = pl.program_id(0)
    pos = (lax.broadcasted_iota(jnp.int32, (ts, V7X_LANES), 0) + i * ts).astype(F32)
    lane = lax.broadcasted_iota(jnp.int32, (ts, V7X_LANES), 1)
    ang = pos * invf_ref[...]
    c = jnp.cos(ang)
    s = jnp.sin(ang)
    half = ROPE_DIM // 2
    cos_t = jnp.where(lane < ROPE_DIM, c, 0.0)
    sin_t = jnp.where(lane < half, -s, jnp.where(lane < ROPE_DIM, s, 0.0))
    cos_ref[...] = cos_t
    sin_ref[...] = sin_t
    cos_tr_ref[...] = cos_t.T[0:ROPE_DIM]
    sin_tr_ref[...] = sin_t.T[0:ROPE_DIM]


def _rope_tables(invf_row, seq, ts):
    return pl.pallas_call(
        functools.partial(_rope_table_kernel, ts=ts),
        grid=(seq // ts,),
        in_specs=[pl.BlockSpec((1, V7X_LANES), lambda i: (0, 0))],
        out_specs=[pl.BlockSpec((ts, V7X_LANES), lambda i: (i, 0)),
                   pl.BlockSpec((ts, V7X_LANES), lambda i: (i, 0)),
                   pl.BlockSpec((ROPE_DIM, ts), lambda i: (0, i)),
                   pl.BlockSpec((ROPE_DIM, ts), lambda i: (0, i))],
        out_shape=[jax.ShapeDtypeStruct((seq, V7X_LANES), F32)] * 2
                  + [jax.ShapeDtypeStruct((ROPE_DIM, seq), F32)] * 2,
        compiler_params=_params(("parallel",), 16 * ts * V7X_LANES * 4 + (4 << 20)),
        name="rope_tables",
    )(invf_row)


def _l0_mixer_kernel(x_ref, gmix_ref, win_ref, gv_ref, ws_ref, bs_ref, wpool_ref, psc_ref, wout_ref,
                     o_ref, h_ref, u_ref, v_ref, z_ref, cat_ref, *, tm, tiles_per_seq, d_a, d_g):
    i = pl.program_id(0)
    seq_tile = i % tiles_per_seq
    nb = tm // GMLP_BLOCK

    @pl.when(seq_tile == 0)
    def _():
        z_ref[0:POOL_HALO, :] = jnp.zeros((POOL_HALO, z_ref.shape[1]), F32)

    h_ref[...] = _rms(x_ref[...], gmix_ref[...]).astype(BF16)

    pre_u = _dot(h_ref[...], win_ref[:, 0:d_a])
    pre_v = _dot(h_ref[...], win_ref[:, d_a:2 * d_a])
    z_ref[POOL_HALO:POOL_HALO + tm, :] = _dot(h_ref[...], win_ref[:, 2 * d_a:])

    u_ref[...] = _gelu_tanh(pre_u)
    v_ref[...] = _rms(_gelu_tanh(pre_v), gv_ref[...]).astype(BF16)

    tt = lax.broadcasted_iota(jnp.int32, (GMLP_BLOCK, GMLP_BLOCK), 0)
    ss = lax.broadcasted_iota(jnp.int32, (GMLP_BLOCK, GMLP_BLOCK), 1)
    tri = (ss // CHUNK <= tt // CHUNK).astype(F32)
    hd_w = d_a // A_HEADS
    for hd in range(A_HEADS):
        c0 = hd * hd_w
        wm = (ws_ref[hd] * tri).astype(BF16)
        rhs = jnp.concatenate(
            [v_ref[n * GMLP_BLOCK:(n + 1) * GMLP_BLOCK, c0:c0 + hd_w] for n in range(nb)], axis=1)
        mix = _dot(wm, rhs)
        bias = bs_ref[hd]
        for n in range(nb):
            r0 = n * GMLP_BLOCK
            a = u_ref[r0:r0 + GMLP_BLOCK, c0:c0 + hd_w] * (mix[:, n * hd_w:(n + 1) * hd_w] + bias)
            cat_ref[r0:r0 + GMLP_BLOCK, c0:c0 + hd_w] = a.astype(BF16)

    o_ref[...] = x_ref[...] + _dot(cat_ref[:, 0:d_a], wout_ref[0:d_a, :])

    pos1 =lax.broadcasted_iota(jnp.int32, (tm, d_g), 0) + (seq_tile * tm + 1)
    for g, win in enumerate(POOL_WINDOWS):
        zz = z_ref[:, g * d_g:(g + 1) * d_g]
        s = zz
        k = 1
        while k < win:
            s = s + pltpu.roll(s, k, axis=0)
            k *= 2
        cnt = jnp.minimum(pos1, win).astype(F32)
        d = (s[POOL_HALO:] / cnt - zz[POOL_HALO:]).astype(BF16)
        y = _dot(d, wpool_ref[g]) * psc_ref[:, g * d_g:(g + 1) * d_g]
        cat_ref[:, d_a + g * d_g:d_a + (g + 1) * d_g] = y.astype(BF16)
    z_ref[0:POOL_HALO, :] = z_ref[tm:tm + POOL_HALO, :]

    o_ref[...] += _dot(cat_ref[:, d_a:], wout_ref[d_a:, :])


def _l0_mixer(x2d, gmix, w_in, g_v, w_s, b_s_b, w_pool, pscale, w_out, *, seq, tm):
    t, d = x2d.shape
    n_in = w_in.shape[1]
    d_a = g_v.shape[1]
    d_b = pscale.shape[1]
    d_g = d_b // len(POOL_WINDOWS)
    assert n_in == 2 * d_a + d_b and seq % tm == 0 and tm % GMLP_BLOCK == 0
    vmem = (4 * tm * d * 4
            + w_in.size * 2 + w_out.size * 2 + w_pool.size * 2 + (w_s.size + b_s_b.size) * 4
            + tm * d * 2 + tm * d_a * 4 + tm * d_a * 2 + (tm + POOL_HALO) * d_b * 4 + tm * (d_a + d_b) * 2
            + 4 * tm * d_a * 4)
    return pl.pallas_call(
        functools.partial(_l0_mixer_kernel, tm=tm, tiles_per_seq=seq // tm, d_a=d_a, d_g=d_g),
        grid=(t // tm,),
        in_specs=[pl.BlockSpec((tm, d), lambda i: (i, 0)),
                  _const_spec(gmix.shape), _const_spec(w_in.shape), _const_spec(g_v.shape),
                  _const_spec(w_s.shape), _const_spec(b_s_b.shape), _const_spec(w_pool.shape),
                  _const_spec(pscale.shape), _const_spec(w_out.shape)],
        out_specs=pl.BlockSpec((tm, d), lambda i: (i, 0)),
        out_shape=jax.ShapeDtypeStruct((t, d), F32),
        scratch_shapes=[pltpu.VMEM((tm, d), BF16),
                        pltpu.VMEM((tm, d_a), F32),
                        pltpu.VMEM((tm, d_a), BF16),
                        pltpu.VMEM((tm + POOL_HALO, d_b), F32),
                        pltpu.VMEM((tm, d_a + d_b), BF16)],
        compiler_params=_params(("arbitrary",), vmem),
        name="l0_mixer",
    )(x2d, gmix, w_in, g_v, w_s, b_s_b, w_pool, pscale, w_out)


def _ffn_kernel(*refs, final_norm, emit_bf16):
    it = iter(refs)
    x_ref, g_ref, wg_ref, wu_ref, wd_ref = next(it), next(it), next(it), next(it), next(it)
    gfin_ref = next(it) if final_norm else None
    o_ref = next(it)
    bf16_out = [next(it) for _ in range(3)] if emit_bf16 else None
    h_ref = next(it)
    f = pl.program_id(1)

    @pl.when(f == 0)
    def _():
        x = x_ref[...]
        h_ref[...] = _rms(x, g_ref[...]).astype(BF16)
        o_ref[...] = x

    wg = wg_ref[...].astype(BF16)
    wu = wu_ref[...].astype(BF16)
    wd = wd_ref[...].astype(BF16)
    if emit_bf16:
        for dst_ref, w in zip(bf16_out, (wg, wu, wd)):
            dst_ref[...] = w

    h = h_ref[...]
    tf = wg.shape[1]
    assert tf % V7X_MXU_COLS == 0
    groups = [slice(c, c + V7X_MXU_COLS) for c in range(0, tf, V7X_MXU_COLS)]
    pre = [(_dot(h, wg[:, sl]), _dot(h, wu[:, sl])) for sl in groups]
    acts = [(gate * jax.nn.sigmoid(gate) * up).astype(BF16) for gate, up in pre]
    o_ref[...] += functools.reduce(lambda a, b: a + b, [_dot(a, wd[sl, :]) for a, sl in zip(acts, groups)])

    if final_norm:
        @pl.when(f == pl.num_programs(1) - 1)
        def _():
            o_ref[...] = _rms(o_ref[...], gfin_ref[...])


def _ffn(x2d, g, w_gate, w_up, w_down, *, tm, tf, layer=None, tile_lo=0, n_tiles=None, in_place=False,
         emit_bf16=False, g_final=None):
    t, d = x2d.shape
    dff = w_gate.shape[-1]
    n_tiles = t // tm - tile_lo if n_tiles is None else n_tiles
    nf = dff // tf
    assert t % tm == 0 and dff % tf == 0 and (not emit_bf16 or n_tiles == 1)
    final_norm = g_final is not None
    wbytes = w_gate.dtype.itemsize

    def wspec(w, blk, imap):
        if w.ndim == 2:
            return pl.BlockSpec(blk, imap)
        return pl.BlockSpec((None,) + blk, lambda i, f: (layer,) + imap(i, f))

    rows = lambda i, f: (i + tile_lo, 0)
    gate_cols = lambda i, f: (0, f)
    down_rows = lambda i, f: (f, 0)
    args = [x2d, g, w_gate, w_up, w_down]
    in_specs = [pl.BlockSpec((tm, d), rows), _const_spec(g.shape),
                wspec(w_gate, (d, tf), gate_cols), wspec(w_up, (d, tf), gate_cols),
                wspec(w_down, (tf, d), down_rows)]
    vmem = 4 * tm * d * 4 + tm * d * 2 + 2 * 3 * d * tf * wbytes + 3 * d * tf * 2 + 4 * tm * tf * 4
    if final_norm:
        args.append(g_final)
        in_specs.append(_const_spec(g_final.shape))
    out_specs = [pl.BlockSpec((tm, d), rows)]
    out_shape = [jax.ShapeDtypeStruct((t, d), F32)]
    if emit_bf16:
        out_specs += [pl.BlockSpec((d, tf), gate_cols), pl.BlockSpec((d, tf), gate_cols),
                      pl.BlockSpec((tf, d), down_rows)]
        out_shape += [jax.ShapeDtypeStruct((d, dff), BF16)] * 2 + [jax.ShapeDtypeStruct((dff, d), BF16)]
        vmem += 2 * 3 * d * tf * 2
    res = pl.pallas_call(
        functools.partial(_ffn_kernel, final_norm=final_norm, emit_bf16=emit_bf16),
        grid=(n_tiles, nf),
        in_specs=in_specs,
        out_specs=out_specs,
        out_shape=out_shape,
        input_output_aliases={0: 0} if in_place else {},
        scratch_shapes=[pltpu.VMEM((tm, d), BF16)],
        compiler_params=_params(("parallel", "arbitrary"), vmem),
        name="ffn_final" if final_norm else ("ffn_head" if emit_bf16 else "ffn"),
    )(*args)
    return res if emit_bf16 else res[0]


def _mla_proj_kernel(x_ref, gmix_ref, win_ref, gcq_ref, gckv_ref, wuq_tr_ref, wkn_ref, wv_tr_ref,
                     cos_ref, sin_ref, cos_tr_ref, sin_tr_ref,
                     q_tr_ref, k_ref, v_tr_ref, h_ref, cq_tr_ref, ckv_ref, ckv_tr_ref, *, chunk):
    h_ref[...] = _rms(x_ref[...], gmix_ref[...]).astype(BF16)
    p = _dot(h_ref[...], win_ref[...])
    cq_tr_ref[...] = _rms(p[:, 0:Q_LORA], gcq_ref[...]).T.astype(BF16)
    ckv = _rms(p[:, Q_LORA:Q_LORA + KV_LORA], gckv_ref[...])
    ckv_ref[...] = ckv.astype(BF16)
    ckv_tr_ref[...] = ckv.T.astype(BF16)
    a = p[:, Q_LORA + KV_LORA:]
    k_rope = (a * cos_ref[...] + pltpu.roll(a, ROPE_DIM, axis=1) * sin_ref[...]).astype(BF16)

    rope_lo, rope_mid, rope_hi = NOPE_DIM, NOPE_DIM + ROPE_DIM // 2, NOPE_DIM + ROPE_DIM
    for hd in range(C_HEADS):
        qh = _dot(wuq_tr_ref[hd * rope_hi:(hd + 1) * rope_hi, :], cq_tr_ref[...])
        r0 = hd * QK_PAD
        q_tr_ref[r0:r0 + rope_lo, :] = qh[0:rope_lo].astype(BF16)
        swapped = jnp.concatenate([qh[rope_mid:rope_hi], qh[rope_lo:rope_mid]], axis=0)
        roped = qh[rope_lo:rope_hi] * cos_tr_ref[...] + swapped * sin_tr_ref[...]
        q_tr_ref[r0 + rope_lo:r0 + rope_hi, :] = roped.astype(BF16)
        q_tr_ref[r0 + rope_hi:r0 + QK_PAD, :] = jnp.zeros((QK_PAD - rope_hi, qh.shape[1]), BF16)

    heads_per_chunk = chunk // NOPE_DIM
    for c in range(C_HEADS // heads_per_chunk):
        sl = slice(c * chunk, (c + 1) * chunk)
        kn = _dot(ckv_ref[...], wkn_ref[:, sl]).astype(BF16)
        for hh in range(heads_per_chunk):
            hd = c * heads_per_chunk + hh
            k_ref[hd, :, 0:NOPE_DIM] = kn[:, hh * NOPE_DIM:(hh + 1) * NOPE_DIM]
            k_ref[hd, :, NOPE_DIM:] = k_rope
        v_tr_ref[sl, :] = _dot(wv_tr_ref[sl, :], ckv_tr_ref[...]).astype(BF16)


def _mla_proj(x2d, gmix, w_in_ext, g_cq_s, g_ckv, w_uq_tr, w_kn, w_v_tr, cos_t, sin_t, cos_tr, sin_tr,
              *, seq, tm):
    t, d = x2d.shape
    n_q = C_HEADS * QK_PAD
    assert w_uq_tr.shape[0] == C_HEADS * (NOPE_DIM + ROPE_DIM)
    n_kv = w_kn.shape[1]
    tiles_per_seq = seq // tm
    assert seq % tm == 0 and w_in_ext.shape[1] == Q_LORA + KV_LORA + 2 * ROPE_DIM
    vmem = (2 * tm * d * 4 + (w_in_ext.size + w_uq_tr.size + w_kn.size + w_v_tr.size) * 2
            + 2 * 2 * tm * (V7X_LANES + ROPE_DIM) * 4
            + 2 * tm * (n_q + C_HEADS * QK_PAD + n_kv) * 2
            + tm * d * 2 + tm * (Q_LORA + 2 * KV_LORA) * 2
            + 4 * tm * w_in_ext.shape[1] * 4)
    row = lambda i: (i, 0)
    col = lambda i: (0, i)
    return pl.pallas_call(
        functools.partial(_mla_proj_kernel, chunk=KV_PROJ_COLS),
        grid=(t // tm,),
        in_specs=[pl.BlockSpec((tm, d), row),
                  _const_spec(gmix.shape), _const_spec(w_in_ext.shape), _const_spec(g_cq_s.shape),
                  _const_spec(g_ckv.shape), _const_spec(w_uq_tr.shape), _const_spec(w_kn.shape),
                  _const_spec(w_v_tr.shape),
                  pl.BlockSpec((tm, V7X_LANES), lambda i: (i % tiles_per_seq, 0)),
                  pl.BlockSpec((tm, V7X_LANES), lambda i: (i % tiles_per_seq, 0)),
                  pl.BlockSpec((ROPE_DIM, tm), lambda i: (0, i % tiles_per_seq)),
                  pl.BlockSpec((ROPE_DIM, tm), lambda i: (0, i % tiles_per_seq))],
        out_specs=[pl.BlockSpec((n_q, tm), col), pl.BlockSpec((C_HEADS, tm, QK_PAD), lambda i: (0, i, 0)),
                   pl.BlockSpec((n_kv, tm), col)],
        out_shape=[jax.ShapeDtypeStruct((n_q, t), BF16), jax.ShapeDtypeStruct((C_HEADS, t, QK_PAD), BF16),
                   jax.ShapeDtypeStruct((n_kv, t), BF16)],
        scratch_shapes=[pltpu.VMEM((tm, d), BF16), pltpu.VMEM((Q_LORA, tm), BF16),
                        pltpu.VMEM((tm, KV_LORA), BF16), pltpu.VMEM((KV_LORA, tm), BF16)],
        compiler_params=_params(("parallel",), vmem),
        name="mla_proj",
    )(x2d, gmix, w_in_ext, g_cq_s, g_ckv, w_uq_tr, w_kn, w_v_tr, cos_t, sin_t, cos_tr, sin_tr)


def _attn_kernel(*refs, tk, tq, skew, n_cast):
    q_tr_ref, k_ref, v_tr_ref = refs[0:3]
    cast_in = refs[3:3 + n_cast]
    o_ref = refs[3 + n_cast]
    cast_out = refs[4 + n_cast:4 + 2 * n_cast]
    vaug_ref, m_ref, acc_ref = refs[4 + 2 * n_cast:]
    for src_ref, dst_ref in zip(cast_in, cast_out):
        dst_ref[...] = src_ref[...].astype(BF16)
    seq = k_ref.shape[0]
    vaug_ref[0:V_DIM, :] = v_tr_ref[...]
    vaug_ref[V_DIM:, :] = jnp.ones((vaug_ref.shape[0] - V_DIM, seq), BF16)

    units = []
    for j in range(seq // tk):
        for i in range(seq // tq):
            k_lo, q_lo = j * tk, i * tq
            k_hi = min(k_lo + tk, q_lo + tq)
            if k_hi > k_lo:
                units.append((j == 0, k_lo, k_hi, q_lo))

    def scores(unit):
        _, k_lo, k_hi, q_lo = unit
        s = _dot(k_ref[k_lo:k_hi, :], q_tr_ref[:, q_lo:q_lo + tq])
        if k_hi > q_lo + CHUNK:
            kc = (lax.broadcasted_iota(jnp.int32, s.shape, 0) + k_lo) // CHUNK
            qc = (lax.broadcasted_iota(jnp.int32, s.shape, 1) + q_lo) // CHUNK
            s = jnp.where(kc <= qc, s, -jnp.inf)
        return s

    pending = [scores(u) for u in units[:skew]]
    for n, (first, k_lo, k_hi, q_lo) in enumerate(units):
        if n + skew < len(units):
            pending.append(scores(units[n + skew]))
        s = pending.pop(0)
        qs = slice(q_lo, q_lo + tq)
        v_j = vaug_ref[:, k_lo:k_hi]
        m_blk = jnp.max(s, axis=0, keepdims=True)
        if first:
            m_new = m_blk
            acc_ref[:, qs] = _dot(v_j, jnp.exp2(s - m_new).astype(BF16))
        else:
            m_prev = m_ref[:, qs]
            m_new = jnp.maximum(m_prev, m_blk)
            alpha = jnp.exp2(m_prev - m_new)
            acc_ref[:, qs] = alpha * acc_ref[:, qs] + _dot(v_j, jnp.exp2(s - m_new).astype(BF16))
        m_ref[:, qs] = m_new

    for i in range(seq // tq):
        qs = slice(i * tq, (i + 1) * tq)
        o_ref[qs, :] = (acc_ref[0:V_DIM, qs] / acc_ref[V_DIM:V_DIM + 1, qs]).T.astype(o_ref.dtype)


def _attention(q_tr, k, v_tr, *, batch, seq, tk, tq, skew, cast_next=(), cast_layer=None):
    t = k.shape[1]
    steps = batch * C_HEADS
    assert seq % tk == 0 and seq % tq == 0 and tk % CHUNK == 0 and tq % CHUNK == 0
    assert V_DIM == V7X_LANES and NOPE_DIM == V7X_LANES
    vmem = (2 * (2 * QK_PAD + V_DIM) * seq * 2 + 2 * seq * V_DIM * 2
            + (V_DIM + ONES_ROWS) * seq * 6 + 8 * seq * 4 + 24 * tk * tq * 4)
    in_specs = [pl.BlockSpec((QK_PAD, seq), lambda b, h: (h, b)),
                pl.BlockSpec((None, seq, QK_PAD), lambda b, h: (h, b, 0)),
                pl.BlockSpec((V_DIM, seq), lambda b, h: (h, b))]
    out_specs = [pl.BlockSpec((None, seq, V_DIM), lambda b, h: (h, b, 0))]
    out_shape = [jax.ShapeDtypeStruct((C_HEADS, t, V_DIM), BF16)]
    for w in cast_next:
        _, r, c = w.shape
        assert r % (steps * 16) == 0
        in_specs.append(pl.BlockSpec((None, r // steps, c), lambda b, h: (cast_layer, b * C_HEADS + h, 0)))
        out_specs.append(pl.BlockSpec((r // steps, c), lambda b, h: (b * C_HEADS + h, 0)))
        out_shape.append(jax.ShapeDtypeStruct((r, c), BF16))
        vmem += 2 * (r // steps) * c * 6
    res = pl.pallas_call(
        functools.partial(_attn_kernel, tk=tk, tq=tq, skew=skew, n_cast=len(cast_next)),
        grid=(batch, C_HEADS),
        in_specs=in_specs,
        out_specs=out_specs,
        out_shape=out_shape,
        scratch_shapes=[pltpu.VMEM((V_DIM + ONES_ROWS, seq), BF16),
                        pltpu.VMEM((1, seq), F32),
                        pltpu.VMEM((V_DIM + ONES_ROWS, seq), F32)],
        compiler_params=_params(("parallel", "parallel"), vmem),
        name="attention",
    )(q_tr, k, v_tr, *cast_next)
    return res if cast_next else res[0]


def _proj_residual_kernel(x_ref, a_ref, w_ref, o_ref):
    a = jnp.concatenate([a_ref[hd] for hd in range(a_ref.shape[0])], axis=1)
    o_ref[...] = x_ref[...] + _dot(a, w_ref[...])


def _proj_residual(x2d, a, w, *, tm):
    t, d = x2d.shape
    heads, _, hd_dim = a.shape
    k = heads * hd_dim
    vmem = 4 * tm * d * 4 + 2 * tm * k * 2 + w.size * 2 + 2 * tm * d * 4
    return pl.pallas_call(
        _proj_residual_kernel,
        grid=(t // tm,),
        in_specs=[pl.BlockSpec((tm, d), lambda i: (i, 0)), pl.BlockSpec((heads, tm, hd_dim), lambda i: (0, i, 0)),
                  _const_spec(w.shape)],
        out_specs=pl.BlockSpec((tm, d), lambda i: (i, 0)),
        out_shape=jax.ShapeDtypeStruct((t, d), F32),
        compiler_params=_params(("parallel",), vmem),
        name="attn_out_proj",
    )(x2d, a, w)


def _swap_halves(w):
    half = w.shape[-1] // 2
    return jnp.concatenate([w[..., half:], w[..., :half]], axis=-1)


def kernel(x, g_mix, g_ffn, g_final, w_in_ab, g_v, w_s, b_s, w_pool, pool_scale, w_out_ab, w_in_c, g_cq, g_ckv, w_uq, w_ukv, w_out_c, w_gate, w_up, w_down):
    batch, seq, d = x.shape
    x2d = x.reshape(batch * seq, d)
    row = lambda a: a.reshape(1, -1)

    b_s_b = jnp.broadcast_to(b_s[0][:, :, None], (A_HEADS, GMLP_BLOCK, GMLP_BLOCK))
    x1 = _l0_mixer(x2d, row(g_mix[0]), w_in_ab[0].astype(BF16), row(g_v[0]), w_s[0], b_s_b,
                   w_pool[0].astype(BF16), row(pool_scale[0]), w_out_ab[0].astype(BF16),
                   seq=seq, tm=PROJ_ROWS)
    x2_head, wg0, wu0, wd0 = _ffn(x1, row(g_ffn[0]), w_gate, w_up, w_down, layer=0, tm=FFN_ROWS,
                                  tf=FFN_HIDDEN_F32, n_tiles=1, in_place=True, emit_bf16=True)
    x2 = _ffn(x2_head, row(g_ffn[0]), wg0, wu0, wd0, tm=FFN_ROWS, tf=FFN_HIDDEN, tile_lo=1, in_place=True)

    inv_freq = ROPE_THETA ** (-jnp.arange(0, ROPE_DIM, 2, dtype=F32) / ROPE_DIM)
    cos_t, sin_t, cos_tr, sin_tr = _rope_tables(
        jnp.tile(inv_freq, V7X_LANES // inv_freq.shape[0]).reshape(1, -1), seq, PROJ_ROWS)
    wc = w_in_c[0]
    w_kr = wc[:, Q_LORA + KV_LORA:]
    w_in_ext = jnp.concatenate([wc, _swap_halves(w_kr)], axis=1).astype(BF16)
    w_uq_tr = w_uq[0].T.astype(BF16)
    wkv = w_ukv[0].reshape(KV_LORA, C_HEADS, NOPE_DIM + V_DIM)
    w_kn = wkv[..., :NOPE_DIM].reshape(KV_LORA, -1).astype(BF16)
    w_v_tr = wkv[..., NOPE_DIM:].reshape(KV_LORA, -1).T.astype(BF16)
    score_scale = (NOPE_DIM + ROPE_DIM) ** -0.5 * LOG2_E
    q_tr, k, v_tr = _mla_proj(x2, row(g_mix[1]), w_in_ext, row(g_cq[0]) * score_scale,
                              row(g_ckv[0]), w_uq_tr, w_kn, w_v_tr, cos_t, sin_t, cos_tr, sin_tr,
                              seq=seq, tm=PROJ_ROWS)
    o, w_gate1, w_up1, w_down1 = _attention(q_tr, k, v_tr, batch=batch, seq=seq, tk=ATTN_KEYS, tq=ATTN_QUERIES,
                                            skew=ATTN_LOOKAHEAD, cast_next=(w_gate, w_up, w_down), cast_layer=1)
    x3 = _proj_residual(x2, o, w_out_c[0].astype(BF16), tm=PROJ_ROWS)
    out = _ffn(x3, row(g_ffn[1]), w_gate1, w_up1, w_down1, tm=FFN_ROWS, tf=FFN_HIDDEN, g_final=row(g_final))
    return out.reshape(batch, seq, d)
```

```python
import functools

import jax
import jax.numpy as jnp
from jax import lax
from jax.experimental import pallas as pl
from jax.experimental.pallas import tpu as pltpu

F32 = jnp.float32
BF16 = jnp.bfloat16

CHUNK = 64
EPS = 1e-6
GMLP_BLOCK = 128
A_HEADS = 8
POOL_WINDOWS = (2, 4, 8, 16)
POOL_HALO = 16
C_HEADS = 16
Q_LORA = 512
KV_LORA = 512
NOPE_DIM = 128
ROPE_DIM = 64
V_DIM = 128
ROPE_THETA = 10000.0
QK_PAD = 256
ONES_ROWS = 16

V7X_LANES = 128
V7X_MXU_COLS = 256
V7X_SCOPED_VMEM_BYTES = 60000 * 1024

PROJ_ROWS = 512
FFN_ROWS = 1024
FFN_HIDDEN_F32 = 256
FFN_HIDDEN = 512
KV_PROJ_COLS = 512
ATTN_KEYS = 512
ATTN_QUERIES = 256
ATTN_LOOKAHEAD = 3
LOG2_E = 1.4426950408889634


def _rms(xf, g):
    ms = jnp.mean(xf * xf, axis=-1, keepdims=True)
    return xf * lax.rsqrt(ms + EPS) * g


def _gelu_tanh(x):
    c = 0.7978845608028654
    return 0.5 * x * (1.0 + jnp.tanh(c * (x + 0.044715 * (x * x * x))))


def _dot(a, b):
    return jnp.dot(a, b, preferred_element_type=F32)


def _const_spec(shape):
    nd = len(shape)
    return pl.BlockSpec(shape, lambda *_: (0,) * nd, pipeline_mode=pl.Buffered(1))


def _params(semantics, vmem_bytes):
    return pltpu.CompilerParams(dimension_semantics=semantics,
                                vmem_limit_bytes=min(int(vmem_bytes), V7X_SCOPED_VMEM_BYTES))


def _rope_table_kernel(invf_ref, cos_ref, sin_ref, cos_tr_ref, sin_tr_ref, *, ts):
    i = pl.program_id(0)
    pos = (lax.broadcasted_iota(jnp.int32, (ts, V7X_LANES), 0) + i * ts).astype(F32)
    lane = lax.broadcasted_iota(jnp.int32, (ts, V7X_LANES), 1)
    ang = pos * invf_ref[...]
    c = jnp.cos(ang)
    s = jnp.sin(ang)
    half = ROPE_DIM // 2
    cos_t = jnp.where(lane < ROPE_DIM, c, 0.0)
    sin_t = jnp.where(lane < half, -s, jnp.where(lane < ROPE_DIM, s, 0.0))
    cos_ref[...] = cos_t
    sin_ref[...] = sin_t
    cos_tr_ref[...] = cos_t.T[0:ROPE_DIM]
    sin_tr_ref[...] = sin_t.T[0:ROPE_DIM]


def _rope_tables(invf_row, seq, ts):
    return pl.pallas_call(
        functools.partial(_rope_table_kernel, ts=ts),
        grid=(seq // ts,),
        in_specs=[pl.BlockSpec((1, V7X_LANES), lambda i: (0, 0))],
        out_specs=[pl.BlockSpec((ts, V7X_LANES), lambda i: (i, 0)),
                   pl.BlockSpec((ts, V7X_LANES), lambda i: (i, 0)),
                   pl.BlockSpec((ROPE_DIM, ts), lambda i: (0, i)),
                   pl.BlockSpec((ROPE_DIM, ts), lambda i: (0, i))],
        out_shape=[jax.ShapeDtypeStruct((seq, V7X_LANES), F32)] * 2
                  + [jax.ShapeDtypeStruct((ROPE_DIM, seq), F32)] * 2,
        compiler_params=_params(("parallel",), 16 * ts * V7X_LANES * 4 + (4 << 20)),
        name="rope_tables",
    )(invf_row)


def _l0_mixer_kernel(x_ref, gmix_ref, win_ref, gv_ref, ws_ref, bs_ref, wpool_ref, psc_ref, wout_ref,
                     o_ref, h_ref, u_ref, v_ref, z_ref, cat_ref, *, tm, tiles_per_seq, d_a, d_g):
    i = pl.program_id(0)
    seq_tile = i % tiles_per_seq
    nb = tm // GMLP_BLOCK

    @pl.when(seq_tile == 0)
    def _():
        z_ref[0:POOL_HALO, :] = jnp.zeros((POOL_HALO, z_ref.shape[1]), F32)

    h_ref[...] = _rms(x_ref[...], gmix_ref[...]).astype(BF16)

    pre_u = _dot(h_ref[...], win_ref[:, 0:d_a])
    pre_v = _dot(h_ref[...], win_ref[:, d_a:2 * d_a])
    z_ref[POOL_HALO:POOL_HALO + tm, :] = _dot(h_ref[...], win_ref[:, 2 * d_a:])

    u_ref[...] = _gelu_tanh(pre_u)
    v_ref[...] = _rms(_gelu_tanh(pre_v), gv_ref[...]).astype(BF16)

    tt = lax.broadcasted_iota(jnp.int32, (GMLP_BLOCK, GMLP_BLOCK), 0)
    ss = lax.broadcasted_iota(jnp.int32, (GMLP_BLOCK, GMLP_BLOCK), 1)
    tri = (ss // CHUNK <= tt // CHUNK).astype(F32)
    hd_w = d_a // A_HEADS
    for hd in range(A_HEADS):
        c0 = hd * hd_w
        wm = (ws_ref[hd] * tri).astype(BF16)
        rhs = jnp.concatenate(
            [v_ref[n * GMLP_BLOCK:(n + 1) * GMLP_BLOCK, c0:c0 + hd_w] for n in range(nb)], axis=1)
        mix = _dot(wm, rhs)
        bias = bs_ref[hd]
        for n in range(nb):
            r0 = n * GMLP_BLOCK
            a = u_ref[r0:r0 + GMLP_BLOCK, c0:c0 + hd_w] * (mix[:, n * hd_w:(n + 1) * hd_w] + bias)
            cat_ref[r0:r0 + GMLP_BLOCK, c0:c0 + hd_w] = a.astype(BF16)

    o_ref[...] = x_ref[...] + _dot(cat_ref[:, 0:d_a], wout_ref[0:d_a, :])

    pos1 =lax.broadcasted_iota(jnp.int32, (tm, d_g), 0) + (seq_tile * tm + 1)
    for g, win in enumerate(POOL_WINDOWS):
        zz = z_ref[:, g * d_g:(g + 1) * d_g]
        s = zz
        k = 1
        while k < win:
            s = s + pltpu.roll(s, k, axis=0)
            k *= 2
        cnt = jnp.minimum(pos1, win).astype(F32)
        d = (s[POOL_HALO:] / cnt - zz[POOL_HALO:]).astype(BF16)
        y = _dot(d, wpool_ref[g]) * psc_ref[:, g * d_g:(g + 1) * d_g]
        cat_ref[:, d_a + g * d_g:d_a + (g + 1) * d_g] = y.astype(BF16)
    z_ref[0:POOL_HALO, :] = z_ref[tm:tm + POOL_HALO, :]

    o_ref[...] += _dot(cat_ref[:, d_a:], wout_ref[d_a:, :])


def _l0_mixer(x2d, gmix, w_in, g_v, w_s, b_s_b, w_pool, pscale, w_out, *, seq, tm):
    t, d = x2d.shape
    n_in = w_in.shape[1]
    d_a = g_v.shape[1]
    d_b = pscale.shape[1]
    d_g = d_b // len(POOL_WINDOWS)
    assert n_in == 2 * d_a + d_b and seq % tm == 0 and tm % GMLP_BLOCK == 0
    vmem = (4 * tm * d * 4
            + w_in.size * 2 + w_out.size * 2 + w_pool.size * 2 + (w_s.size + b_s_b.size) * 4
            + tm * d * 2 + tm * d_a * 4 + tm * d_a * 2 + (tm + POOL_HALO) * d_b * 4 + tm * (d_a + d_b) * 2
            + 4 * tm * d_a * 4)
    return pl.pallas_call(
        functools.partial(_l0_mixer_kernel, tm=tm, tiles_per_seq=seq // tm, d_a=d_a, d_g=d_g),
        grid=(t // tm,),
        in_specs=[pl.BlockSpec((tm, d), lambda i: (i, 0)),
                  _const_spec(gmix.shape), _const_spec(w_in.shape), _const_spec(g_v.shape),
                  _const_spec(w_s.shape), _const_spec(b_s_b.shape), _const_spec(w_pool.shape),
                  _const_spec(pscale.shape), _const_spec(w_out.shape)],
        out_specs=pl.BlockSpec((tm, d), lambda i: (i, 0)),
        out_shape=jax.ShapeDtypeStruct((t, d), F32),
        scratch_shapes=[pltpu.VMEM((tm, d), BF16),
                        pltpu.VMEM((tm, d_a), F32),
                        pltpu.VMEM((tm, d_a), BF16),
                        pltpu.VMEM((tm + POOL_HALO, d_b), F32),
                        pltpu.VMEM((tm, d_a + d_b), BF16)],
        compiler_params=_params(("arbitrary",), vmem),
        name="l0_mixer",
    )(x2d, gmix, w_in, g_v, w_s, b_s_b, w_pool, pscale, w_out)


def _ffn_kernel(*refs, final_norm, emit_bf16):
    it = iter(refs)
    x_ref, g_ref, wg_ref, wu_ref, wd_ref = next(it), next(it), next(it), next(it), next(it)
    gfin_ref = next(it) if final_norm else None
    o_ref = next(it)
    bf16_out = [next(it) for _ in range(3)] if emit_bf16 else None
    h_ref = next(it)
    f = pl.program_id(1)

    def step(first):
        if first:
            h_ref[...] = _rms(x_ref[...], g_ref[...]).astype(BF16)
        wg = wg_ref[...].astype(BF16)
        wu = wu_ref[...].astype(BF16)
        wd = wd_ref[...].astype(BF16)
        if emit_bf16:
            for dst_ref, w in zip(bf16_out, (wg, wu, wd)):
                dst_ref[...] = w

        h = h_ref[...]
        tf = wg.shape[1]
        assert tf % V7X_MXU_COLS == 0
        groups = [slice(c, c + V7X_MXU_COLS) for c in range(0, tf, V7X_MXU_COLS)]
        pre = [(_dot(h, wg[:, sl]), _dot(h, wu[:, sl])) for sl in groups]
        acts = [(gate * jax.nn.sigmoid(gate) * up).astype(BF16) for gate, up in pre]
        down = functools.reduce(lambda a, b: a + b, [_dot(a, wd[sl, :]) for a, sl in zip(acts, groups)])
        if first:
            o_ref[...] = x_ref[...] + down
        else:
            o_ref[...] += down

    if emit_bf16:
        @pl.when(f == 0)
        def _():
            x = x_ref[...]
            h_ref[...] = _rms(x, g_ref[...]).astype(BF16)
            o_ref[...] = x

        step(False)
    else:
        pl.when(f == 0)(functools.partial(step, True))
        pl.when(f > 0)(functools.partial(step, False))

    if final_norm:
        @pl.when(f == pl.num_programs(1) - 1)
        def _():
            o_ref[...] = _rms(o_ref[...], gfin_ref[...])


def _ffn(x2d, g, w_gate, w_up, w_down, *, tm, tf, layer=None, tile_lo=0, n_tiles=None, in_place=False,
         emit_bf16=False, g_final=None):
    t, d = x2d.shape
    dff = w_gate.shape[-1]
    n_tiles = t // tm - tile_lo if n_tiles is None else n_tiles
    nf = dff // tf
    assert t % tm == 0 and dff % tf == 0 and (not emit_bf16 or n_tiles == 1)
    final_norm = g_final is not None
    wbytes = w_gate.dtype.itemsize

    def wspec(w, blk, imap):
        if w.ndim == 2:
            return pl.BlockSpec(blk, imap)
        return pl.BlockSpec((None,) + blk, lambda i, f: (layer,) + imap(i, f))

    rows = lambda i, f: (i + tile_lo, 0)
    gate_cols = lambda i, f: (0, f)
    down_rows = lambda i, f: (f, 0)
    args = [x2d, g, w_gate, w_up, w_down]
    in_specs = [pl.BlockSpec((tm, d), rows), _const_spec(g.shape),
                wspec(w_gate, (d, tf), gate_cols), wspec(w_up, (d, tf), gate_cols),
                wspec(w_down, (tf, d), down_rows)]
    vmem = 4 * tm * d * 4 + tm * d * 2 + 2 * 3 * d * tf * wbytes + 3 * d * tf * 2 + 4 * tm * tf * 4
    if final_norm:
        args.append(g_final)
        in_specs.append(_const_spec(g_final.shape))
    out_specs = [pl.BlockSpec((tm, d), rows)]
    out_shape = [jax.ShapeDtypeStruct((t, d), F32)]
    if emit_bf16:
        out_specs += [pl.BlockSpec((d, tf), gate_cols), pl.BlockSpec((d, tf), gate_cols),
                      pl.BlockSpec((tf, d), down_rows)]
        out_shape += [jax.ShapeDtypeStruct((d, dff), BF16)] * 2 + [jax.ShapeDtypeStruct((dff, d), BF16)]
        vmem += 2 * 3 * d * tf * 2
    res = pl.pallas_call(
        functools.partial(_ffn_kernel, final_norm=final_norm, emit_bf16=emit_bf16),
        grid=(n_tiles, nf),
        in_specs=in_specs,
        out_specs=out_specs,
        out_shape=out_shape,
        input_output_aliases={0: 0} if in_place else {},
        scratch_shapes=[pltpu.VMEM((tm, d), BF16)],
        compiler_params=_params(("parallel", "arbitrary"), vmem),
        name="ffn_final" if final_norm else ("ffn_head" if emit_bf16 else "ffn"),
    )(*args)
    return res if emit_bf16 else res[0]


def _mla_proj_kernel(x_ref, gmix_ref, win_ref, gcq_ref, gckv_ref, wuq_tr_ref, wkn_ref, wv_tr_ref,
                     cos_ref, sin_ref, cos_tr_ref, sin_tr_ref,
                     q_tr_ref, k_ref, v_tr_ref, h_ref, cq_tr_ref, ckv_ref, ckv_tr_ref, *, chunk):
    h_ref[...] = _rms(x_ref[...], gmix_ref[...]).astype(BF16)
    p = _dot(h_ref[...], win_ref[...])
    cq_tr_ref[...] = _rms(p[:, 0:Q_LORA], gcq_ref[...]).T.astype(BF16)
    ckv = _rms(p[:, Q_LORA:Q_LORA + KV_LORA], gckv_ref[...])
    ckv_ref[...] = ckv.astype(BF16)
    ckv_tr_ref[...] = ckv.T.astype(BF16)
    a = p[:, Q_LORA + KV_LORA:]
    k_rope = (a * cos_ref[...] + pltpu.roll(a, ROPE_DIM, axis=1) * sin_ref[...]).astype(BF16)

    rope_lo, rope_mid, rope_hi = NOPE_DIM, NOPE_DIM + ROPE_DIM // 2, NOPE_DIM + ROPE_DIM
    for hd in range(C_HEADS):
        qh = _dot(wuq_tr_ref[hd * rope_hi:(hd + 1) * rope_hi, :], cq_tr_ref[...])
        r0 = hd * QK_PAD
        q_tr_ref[r0:r0 + rope_lo, :] = qh[0:rope_lo].astype(BF16)
        swapped = jnp.concatenate([qh[rope_mid:rope_hi], qh[rope_lo:rope_mid]], axis=0)
        roped = qh[rope_lo:rope_hi] * cos_tr_ref[...] + swapped * sin_tr_ref[...]
        q_tr_ref[r0 + rope_lo:r0 + rope_hi, :] = roped.astype(BF16)
        q_tr_ref[r0 + rope_hi:r0 + QK_PAD, :] = jnp.zeros((QK_PAD - rope_hi, qh.shape[1]), BF16)

    heads_per_chunk = chunk // NOPE_DIM
    for c in range(C_HEADS // heads_per_chunk):
        sl = slice(c * chunk, (c + 1) * chunk)
        kn = _dot(ckv_ref[...], wkn_ref[:, sl]).astype(BF16)
        for hh in range(heads_per_chunk):
            hd = c * heads_per_chunk + hh
            k_ref[hd, :, 0:NOPE_DIM] = kn[:, hh * NOPE_DIM:(hh + 1) * NOPE_DIM]
            k_ref[hd, :, NOPE_DIM:] = k_rope
        v_tr_ref[sl, :] = _dot(wv_tr_ref[sl, :], ckv_tr_ref[...]).astype(BF16)


def _mla_proj(x2d, gmix, w_in_ext, g_cq_s, g_ckv, w_uq_tr, w_kn, w_v_tr, cos_t, sin_t, cos_tr, sin_tr,
              *, seq, tm):
    t, d = x2d.shape
    n_q = C_HEADS * QK_PAD
    assert w_uq_tr.shape[0] == C_HEADS * (NOPE_DIM + ROPE_DIM)
    n_kv = w_kn.shape[1]
    tiles_per_seq = seq // tm
    assert seq % tm == 0 and w_in_ext.shape[1] == Q_LORA + KV_LORA + 2 * ROPE_DIM
    vmem = (2 * tm * d * 4 + (w_in_ext.size + w_uq_tr.size + w_kn.size + w_v_tr.size) * 2
            + 2 * 2 * tm * (V7X_LANES + ROPE_DIM) * 4
            + 2 * tm * (n_q + C_HEADS * QK_PAD + n_kv) * 2
            + tm * d * 2 + tm * (Q_LORA + 2 * KV_LORA) * 2
            + 4 * tm * w_in_ext.shape[1] * 4)
    row = lambda i: (i, 0)
    col = lambda i: (0, i)
    return pl.pallas_call(
        functools.partial(_mla_proj_kernel, chunk=KV_PROJ_COLS),
        grid=(t // tm,),
        in_specs=[pl.BlockSpec((tm, d), row),
                  _const_spec(gmix.shape), _const_spec(w_in_ext.shape), _const_spec(g_cq_s.shape),
                  _const_spec(g_ckv.shape), _const_spec(w_uq_tr.shape), _const_spec(w_kn.shape),
                  _const_spec(w_v_tr.shape),
                  pl.BlockSpec((tm, V7X_LANES), lambda i: (i % tiles_per_seq, 0)),
                  pl.BlockSpec((tm, V7X_LANES), lambda i: (i % tiles_per_seq, 0)),
                  pl.BlockSpec((ROPE_DIM, tm), lambda i: (0, i % tiles_per_seq)),
                  pl.BlockSpec((ROPE_DIM, tm), lambda i: (0, i % tiles_per_seq))],
        out_specs=[pl.BlockSpec((n_q, tm), col), pl.BlockSpec((C_HEADS, tm, QK_PAD), lambda i: (0, i, 0)),
                   pl.BlockSpec((n_kv, tm), col)],
        out_shape=[jax.ShapeDtypeStruct((n_q, t), BF16), jax.ShapeDtypeStruct((C_HEADS, t, QK_PAD), BF16),
                   jax.ShapeDtypeStruct((n_kv, t), BF16)],
        scratch_shapes=[pltpu.VMEM((tm, d), BF16), pltpu.VMEM((Q_LORA, tm), BF16),
                        pltpu.VMEM((tm, KV_LORA), BF16), pltpu.VMEM((KV_LORA, tm), BF16)],
        compiler_params=_params(("parallel",), vmem),
        name="mla_proj",
    )(x2d, gmix, w_in_ext, g_cq_s, g_ckv, w_uq_tr, w_kn, w_v_tr, cos_t, sin_t, cos_tr, sin_tr)


def _attn_kernel(*refs, tk, tq, skew, n_cast):
    q_tr_ref, k_ref, v_tr_ref = refs[0:3]
    cast_in = refs[3:3 + n_cast]
    o_ref = refs[3 + n_cast]
    cast_out = refs[4 + n_cast:4 + 2 * n_cast]
    vaug_ref, m_ref, acc_ref = refs[4 + 2 * n_cast:]
    for src_ref, dst_ref in zip(cast_in, cast_out):
        dst_ref[...] = src_ref[...].astype(BF16)
    seq = k_ref.shape[0]
    vaug_ref[0:V_DIM, :] = v_tr_ref[...]
    vaug_ref[V_DIM:, :] = jnp.ones((vaug_ref.shape[0] - V_DIM, seq), BF16)

    units = []
    for j in range(seq // tk):
        for i in range(seq // tq):
            k_lo, q_lo = j * tk, i * tq
            k_hi = min(k_lo + tk, q_lo + tq)
            if k_hi > k_lo:
                units.append((j == 0, k_lo, k_hi, q_lo))

    def scores(unit):
        _, k_lo, k_hi, q_lo = unit
        s = _dot(k_ref[k_lo:k_hi, :], q_tr_ref[:, q_lo:q_lo + tq])
        if k_hi > q_lo + CHUNK:
            kc = (lax.broadcasted_iota(jnp.int32, s.shape, 0) + k_lo) // CHUNK
            qc = (lax.broadcasted_iota(jnp.int32, s.shape, 1) + q_lo) // CHUNK
            s = jnp.where(kc <= qc, s, -jnp.inf)
        return s

    pending = [scores(u) for u in units[:skew]]
    for n, (first, k_lo, k_hi, q_lo) in enumerate(units):
        if n + skew < len(units):
            pending.append(scores(units[n + skew]))
        s = pending.pop(0)
        qs = slice(q_lo, q_lo + tq)
        v_j = vaug_ref[:, k_lo:k_hi]
        m_blk = jnp.max(s, axis=0, keepdims=True)
        if first:
            m_new = m_blk
            acc_ref[:, qs] = _dot(v_j, jnp.exp2(s - m_new).astype(BF16))
        else:
            m_prev = m_ref[:, qs]
            m_new = jnp.maximum(m_prev, m_blk)
            alpha = jnp.exp2(m_prev - m_new)
            acc_ref[:, qs] = alpha * acc_ref[:, qs] + _dot(v_j, jnp.exp2(s - m_new).astype(BF16))
        m_ref[:, qs] = m_new

    for i in range(seq // tq):
        qs = slice(i * tq, (i + 1) * tq)
        o_ref[qs, :] = (acc_ref[0:V_DIM, qs] / acc_ref[V_DIM:V_DIM + 1, qs]).T.astype(o_ref.dtype)


def _attention(q_tr, k, v_tr, *, batch, seq, tk, tq, skew, cast_next=(), cast_layer=None):
    t = k.shape[1]
    steps = batch * C_HEADS
    assert seq % tk == 0 and seq % tq == 0 and tk % CHUNK == 0 and tq % CHUNK == 0
    assert V_DIM == V7X_LANES and NOPE_DIM == V7X_LANES
    vmem = (2 * (2 * QK_PAD + V_DIM) * seq * 2 + 2 * seq * V_DIM * 2
            + (V_DIM + ONES_ROWS) * seq * 6 + 8 * seq * 4 + 24 * tk * tq * 4)
    in_specs = [pl.BlockSpec((QK_PAD, seq), lambda b, h: (h, b)),
                pl.BlockSpec((None, seq, QK_PAD), lambda b, h: (h, b, 0)),
                pl.BlockSpec((V_DIM, seq), lambda b, h: (h, b))]
    out_specs = [pl.BlockSpec((None, seq, V_DIM), lambda b, h: (h, b, 0))]
    out_shape = [jax.ShapeDtypeStruct((C_HEADS, t, V_DIM), BF16)]
    for w in cast_next:
        _, r, c = w.shape
        assert r % (steps * 16) == 0
        in_specs.append(pl.BlockSpec((None, r // steps, c), lambda b, h: (cast_layer, b * C_HEADS + h, 0)))
        out_specs.append(pl.BlockSpec((r // steps, c), lambda b, h: (b * C_HEADS + h, 0)))
        out_shape.append(jax.ShapeDtypeStruct((r, c), BF16))
        vmem += 2 * (r // steps) * c * 6
    res = pl.pallas_call(
        functools.partial(_attn_kernel, tk=tk, tq=tq, skew=skew, n_cast=len(cast_next)),
        grid=(batch, C_HEADS),
        in_specs=in_specs,
        out_specs=out_specs,
        out_shape=out_shape,
        scratch_shapes=[pltpu.VMEM((V_DIM + ONES_ROWS, seq), BF16),
                        pltpu.VMEM((1, seq), F32),
                        pltpu.VMEM((V_DIM + ONES_ROWS, seq), F32)],
        compiler_params=_params(("parallel", "parallel"), vmem),
        name="attention",
    )(q_tr, k, v_tr, *cast_next)
    return res if cast_next else res[0]


def _proj_residual_kernel(x_ref, a_ref, w_ref, o_ref):
    a = jnp.concatenate([a_ref[hd] for hd in range(a_ref.shape[0])], axis=1)
    o_ref[...] = x_ref[...] + _dot(a, w_ref[...])


def _proj_residual(x2d, a, w, *, tm):
    t, d = x2d.shape
    heads, _, hd_dim = a.shape
    k = heads * hd_dim
    vmem = 4 * tm * d * 4 + 2 * tm * k * 2 + w.size * 2 + 2 * tm * d * 4
    return pl.pallas_call(
        _proj_residual_kernel,
        grid=(t // tm,),
        in_specs=[pl.BlockSpec((tm, d), lambda i: (i, 0)), pl.BlockSpec((heads, tm, hd_dim), lambda i: (0, i, 0)),
                  _const_spec(w.shape)],
        out_specs=pl.BlockSpec((tm, d), lambda i: (i, 0)),
        out_shape=jax.ShapeDtypeStruct((t, d), F32),
        compiler_params=_params(("parallel",), vmem),
        name="attn_out_proj",
    )(x2d, a, w)


def _swap_halves(w):
    half = w.shape[-1] // 2
    return jnp.concatenate([w[..., half:], w[..., :half]], axis=-1)


def kernel(x, g_mix, g_ffn, g_final, w_in_ab, g_v, w_s, b_s, w_pool, pool_scale, w_out_ab, w_in_c, g_cq, g_ckv, w_uq, w_ukv, w_out_c, w_gate, w_up, w_down):
    batch, seq, d = x.shape
    x2d = x.reshape(batch * seq, d)
    row = lambda a: a.reshape(1, -1)

    b_s_b = jnp.broadcast_to(b_s[0][:, :, None], (A_HEADS, GMLP_BLOCK, GMLP_BLOCK))
    x1 = _l0_mixer(x2d, row(g_mix[0]), w_in_ab[0].astype(BF16), row(g_v[0]), w_s[0], b_s_b,
                   w_pool[0].astype(BF16), row(pool_scale[0]), w_out_ab[0].astype(BF16),
                   seq=seq, tm=PROJ_ROWS)
    x2_head, wg0, wu0, wd0 = _ffn(x1, row(g_ffn[0]), w_gate, w_up, w_down, layer=0, tm=FFN_ROWS,
                                  tf=FFN_HIDDEN_F32, n_tiles=1, in_place=True, emit_bf16=True)
    x2 = _ffn(x2_head, row(g_ffn[0]), wg0, wu0, wd0, tm=FFN_ROWS, tf=FFN_HIDDEN, tile_lo=1, in_place=True)

    inv_freq = ROPE_THETA ** (-jnp.arange(0, ROPE_DIM, 2, dtype=F32) / ROPE_DIM)
    cos_t, sin_t, cos_tr, sin_tr = _rope_tables(
        jnp.tile(inv_freq, V7X_LANES // inv_freq.shape[0]).reshape(1, -1), seq, PROJ_ROWS)
    wc = w_in_c[0]
    w_kr = wc[:, Q_LORA + KV_LORA:]
    w_in_ext = jnp.concatenate([wc, _swap_halves(w_kr)], axis=1).astype(BF16)
    w_uq_tr = w_uq[0].T.astype(BF16)
    wkv = w_ukv[0].reshape(KV_LORA, C_HEADS, NOPE_DIM + V_DIM)
    w_kn = wkv[..., :NOPE_DIM].reshape(KV_LORA, -1).astype(BF16)
    w_v_tr = wkv[..., NOPE_DIM:].reshape(KV_LORA, -1).T.astype(BF16)
    score_scale = (NOPE_DIM + ROPE_DIM) ** -0.5 * LOG2_E
    q_tr, k, v_tr = _mla_proj(x2, row(g_mix[1]), w_in_ext, row(g_cq[0]) * score_scale,
                              row(g_ckv[0]), w_uq_tr, w_kn, w_v_tr, cos_t, sin_t, cos_tr, sin_tr,
                              seq=seq, tm=PROJ_ROWS)
    o, w_gate1, w_up1, w_down1 = _attention(q_tr, k, v_tr, batch=batch, seq=seq, tk=ATTN_KEYS, tq=ATTN_QUERIES,
                                            skew=ATTN_LOOKAHEAD, cast_next=(w_gate, w_up, w_down), cast_layer=1)
    x3 = _proj_residual(x2, o, w_out_c[0].astype(BF16), tm=PROJ_ROWS)
    out = _ffn(x3, row(g_ffn[1]), w_gate1, w_up1, w_down1, tm=FFN_ROWS, tf=FFN_HIDDEN, g_final=row(g_final))
    return out.reshape(batch, seq, d)
```

```python
import functools

import jax
import jax.numpy as jnp
from jax import lax
from jax.experimental import pallas as pl
from jax.experimental.pallas import tpu as pltpu

F32 = jnp.float32
BF16 = jnp.bfloat16

CHUNK = 64
EPS = 1e-6
GMLP_BLOCK = 128
A_HEADS = 8
POOL_WINDOWS = (2, 4, 8, 16)
POOL_HALO = 16
C_HEADS = 16
Q_LORA = 512
KV_LORA = 512
NOPE_DIM = 128
ROPE_DIM = 64
V_DIM = 128
ROPE_THETA = 10000.0
QK_PAD = 256
ONES_ROWS = 16

V7X_LANES = 128
V7X_MXU_COLS = 256
V7X_SCOPED_VMEM_BYTES = 60000 * 1024

PROJ_ROWS = 512
FFN_ROWS = 1024
FFN_HIDDEN_F32 = 256
FFN_HIDDEN = 512
KV_PROJ_COLS = 512
ATTN_KEYS = 512
ATTN_QUERIES = 256
ATTN_HEADS_PER_STEP = 2
ATTN_LOOKAHEAD = 3
LOG2_E = 1.4426950408889634


def _rms(xf, g):
    ms = jnp.mean(xf * xf, axis=-1, keepdims=True)
    return xf * lax.rsqrt(ms + EPS) * g


def _gelu_tanh(x):
    c = 0.7978845608028654
    return 0.5 * x * (1.0 + jnp.tanh(c * (x + 0.044715 * (x * x * x))))


def _dot(a, b):
    return jnp.dot(a, b, preferred_element_type=F32)


def _const_spec(shape):
    nd = len(shape)
    return pl.BlockSpec(shape, lambda *_: (0,) * nd, pipeline_mode=pl.Buffered(1))


def _params(semantics, vmem_bytes):
    return pltpu.CompilerParams(dimension_semantics=semantics,
                                vmem_limit_bytes=min(int(vmem_bytes), V7X_SCOPED_VMEM_BYTES))


def _rope_table_kernel(invf_ref, cos_ref, sin_ref, cos_tr_ref, sin_tr_ref, *, ts):
    i = pl.program_id(0)
    pos = (lax.broadcasted_iota(jnp.int32, (ts, V7X_LANES), 0) + i * ts).astype(F32)
    lane = lax.broadcasted_iota(jnp.int32, (ts, V7X_LANES), 1)
    ang = pos * invf_ref[...]
    c = jnp.cos(ang)
    s = jnp.sin(ang)
    half = ROPE_DIM // 2
    cos_t = jnp.where(lane < ROPE_DIM, c, 0.0)
    sin_t = jnp.where(lane < half, -s, jnp.where(lane < ROPE_DIM, s, 0.0))
    cos_ref[...] = cos_t
    sin_ref[...] = sin_t
    cos_tr_ref[...] = cos_t.T[0:ROPE_DIM]
    sin_tr_ref[...] = sin_t.T[0:ROPE_DIM]


def _rope_tables(invf_row, seq, ts):
    return pl.pallas_call(
        functools.partial(_rope_table_kernel, ts=ts),
        grid=(seq // ts,),
        in_specs=[pl.BlockSpec((1, V7X_LANES), lambda i: (0, 0))],
        out_specs=[pl.BlockSpec((ts, V7X_LANES), lambda i: (i, 0)),
                   pl.BlockSpec((ts, V7X_LANES), lambda i: (i, 0)),
                   pl.BlockSpec((ROPE_DIM, ts), lambda i: (0, i)),
                   pl.BlockSpec((ROPE_DIM, ts), lambda i: (0, i))],
        out_shape=[jax.ShapeDtypeStruct((seq, V7X_LANES), F32)] * 2
                  + [jax.ShapeDtypeStruct((ROPE_DIM, seq), F32)] * 2,
        compiler_params=_params(("parallel",), 16 * ts * V7X_LANES * 4 + (4 << 20)),
        name="rope_tables",
    )(invf_row)


def _l0_mixer_kernel(x_ref, gmix_ref, win_ref, gv_ref, ws_ref, bs_ref, wpool_ref, psc_ref, wout_ref,
                     o_ref, h_ref, u_ref, v_ref, z_ref, cat_ref, *, tm, tiles_per_seq, d_a, d_g):
    i = pl.program_id(0)
    seq_tile = i % tiles_per_seq
    nb = tm // GMLP_BLOCK

    @pl.when(seq_tile == 0)
    def _():
        z_ref[0:POOL_HALO, :] = jnp.zeros((POOL_HALO, z_ref.shape[1]), F32)

    h_ref[...] = _rms(x_ref[...], gmix_ref[...]).astype(BF16)

    pre_u = _dot(h_ref[...], win_ref[:, 0:d_a])
    pre_v = _dot(h_ref[...], win_ref[:, d_a:2 * d_a])
    z_ref[POOL_HALO:POOL_HALO + tm, :] = _dot(h_ref[...], win_ref[:, 2 * d_a:])

    u_ref[...] = _gelu_tanh(pre_u)
    v_ref[...] = _rms(_gelu_tanh(pre_v), gv_ref[...]).astype(BF16)

    tt = lax.broadcasted_iota(jnp.int32, (GMLP_BLOCK, GMLP_BLOCK), 0)
    ss = lax.broadcasted_iota(jnp.int32, (GMLP_BLOCK, GMLP_BLOCK), 1)
    tri = (ss // CHUNK <= tt // CHUNK).astype(F32)
    hd_w = d_a // A_HEADS
    for hd in range(A_HEADS):
        c0 = hd * hd_w
        wm = (ws_ref[hd] * tri).astype(BF16)
        rhs = jnp.concatenate(
            [v_ref[n * GMLP_BLOCK:(n + 1) * GMLP_BLOCK, c0:c0 + hd_w] for n in range(nb)], axis=1)
        mix = _dot(wm, rhs)
        bias = bs_ref[hd]
        for n in range(nb):
            r0 = n * GMLP_BLOCK
            a = u_ref[r0:r0 + GMLP_BLOCK, c0:c0 + hd_w] * (mix[:, n * hd_w:(n + 1) * hd_w] + bias)
            cat_ref[r0:r0 + GMLP_BLOCK, c0:c0 + hd_w] = a.astype(BF16)

    o_ref[...] = x_ref[...] + _dot(cat_ref[:, 0:d_a], wout_ref[0:d_a, :])

    pos1 =lax.broadcasted_iota(jnp.int32, (tm, d_g), 0) + (seq_tile * tm + 1)
    for g, win in enumerate(POOL_WINDOWS):
        zz = z_ref[:, g * d_g:(g + 1) * d_g]
        s = zz
        k = 1
        while k < win:
            s = s + pltpu.roll(s, k, axis=0)
            k *= 2
        cnt = jnp.minimum(pos1, win).astype(F32)
        d = (s[POOL_HALO:] / cnt - zz[POOL_HALO:]).astype(BF16)
        y = _dot(d, wpool_ref[g]) * psc_ref[:, g * d_g:(g + 1) * d_g]
        cat_ref[:, d_a + g * d_g:d_a + (g + 1) * d_g] = y.astype(BF16)
    z_ref[0:POOL_HALO, :] = z_ref[tm:tm + POOL_HALO, :]

    o_ref[...] += _dot(cat_ref[:, d_a:], wout_ref[d_a:, :])


def _l0_mixer(x2d, gmix, w_in, g_v, w_s, b_s_b, w_pool, pscale, w_out, *, seq, tm):
    t, d = x2d.shape
    n_in = w_in.shape[1]
    d_a = g_v.shape[1]
    d_b = pscale.shape[1]
    d_g = d_b // len(POOL_WINDOWS)
    assert n_in == 2 * d_a + d_b and seq % tm == 0 and tm % GMLP_BLOCK == 0
    vmem = (4 * tm * d * 4
            + w_in.size * 2 + w_out.size * 2 + w_pool.size * 2 + (w_s.size + b_s_b.size) * 4
            + tm * d * 2 + tm * d_a * 4 + tm * d_a * 2 + (tm + POOL_HALO) * d_b * 4 + tm * (d_a + d_b) * 2
            + 4 * tm * d_a * 4)
    return pl.pallas_call(
        functools.partial(_l0_mixer_kernel, tm=tm, tiles_per_seq=seq // tm, d_a=d_a, d_g=d_g),
        grid=(t // tm,),
        in_specs=[pl.BlockSpec((tm, d), lambda i: (i, 0)),
                  _const_spec(gmix.shape), _const_spec(w_in.shape), _const_spec(g_v.shape),
                  _const_spec(w_s.shape), _const_spec(b_s_b.shape), _const_spec(w_pool.shape),
                  _const_spec(pscale.shape), _const_spec(w_out.shape)],
        out_specs=pl.BlockSpec((tm, d), lambda i: (i, 0)),
        out_shape=jax.ShapeDtypeStruct((t, d), F32),
        scratch_shapes=[pltpu.VMEM((tm, d), BF16),
                        pltpu.VMEM((tm, d_a), F32),
                        pltpu.VMEM((tm, d_a), BF16),
                        pltpu.VMEM((tm + POOL_HALO, d_b), F32),
                        pltpu.VMEM((tm, d_a + d_b), BF16)],
        compiler_params=_params(("arbitrary",), vmem),
        name="l0_mixer",
    )(x2d, gmix, w_in, g_v, w_s, b_s_b, w_pool, pscale, w_out)


def _ffn_kernel(*refs, final_norm, emit_bf16):
    it = iter(refs)
    x_ref, g_ref, wg_ref, wu_ref, wd_ref = next(it), next(it), next(it), next(it), next(it)
    gfin_ref = next(it) if final_norm else None
    o_ref = next(it)
    bf16_out = [next(it) for _ in range(3)] if emit_bf16 else None
    h_ref = next(it)
    f = pl.program_id(1)

    def step(first):
        if first:
            h_ref[...] = _rms(x_ref[...], g_ref[...]).astype(BF16)
        wg = wg_ref[...].astype(BF16)
        wu = wu_ref[...].astype(BF16)
        wd = wd_ref[...].astype(BF16)
        if emit_bf16:
            for dst_ref, w in zip(bf16_out, (wg, wu, wd)):
                dst_ref[...] = w

        h = h_ref[...]
        tf = wg.shape[1]
        assert tf % V7X_MXU_COLS == 0
        groups = [slice(c, c + V7X_MXU_COLS) for c in range(0, tf, V7X_MXU_COLS)]
        pre = [(_dot(h, wg[:, sl]), _dot(h, wu[:, sl])) for sl in groups]
        acts = [(gate * jax.nn.sigmoid(gate) * up).astype(BF16) for gate, up in pre]
        down = functools.reduce(lambda a, b: a + b, [_dot(a, wd[sl, :]) for a, sl in zip(acts, groups)])
        if first:
            o_ref[...] = x_ref[...] + down
        else:
            o_ref[...] += down

    if emit_bf16:
        @pl.when(f == 0)
        def _():
            x = x_ref[...]
            h_ref[...] = _rms(x, g_ref[...]).astype(BF16)
            o_ref[...] = x

        step(False)
    else:
        pl.when(f == 0)(functools.partial(step, True))
        pl.when(f > 0)(functools.partial(step, False))

    if final_norm:
        @pl.when(f == pl.num_programs(1) - 1)
        def _():
            o_ref[...] = _rms(o_ref[...], gfin_ref[...])


def _ffn(x2d, g, w_gate, w_up, w_down, *, tm, tf, layer=None, tile_lo=0, n_tiles=None, in_place=False,
         emit_bf16=False, g_final=None):
    t, d = x2d.shape
    dff = w_gate.shape[-1]
    n_tiles = t // tm - tile_lo if n_tiles is None else n_tiles
    nf = dff // tf
    assert t % tm == 0 and dff % tf == 0 and (not emit_bf16 or n_tiles == 1)
    final_norm = g_final is not None
    wbytes = w_gate.dtype.itemsize

    def wspec(w, blk, imap):
        if w.ndim == 2:
            return pl.BlockSpec(blk, imap)
        return pl.BlockSpec((None,) + blk, lambda i, f: (layer,) + imap(i, f))

    rows = lambda i, f: (i + tile_lo, 0)
    gate_cols = lambda i, f: (0, f)
    down_rows = lambda i, f: (f, 0)
    args = [x2d, g, w_gate, w_up, w_down]
    in_specs = [pl.BlockSpec((tm, d), rows), _const_spec(g.shape),
                wspec(w_gate, (d, tf), gate_cols), wspec(w_up, (d, tf), gate_cols),
                wspec(w_down, (tf, d), down_rows)]
    vmem = 4 * tm * d * 4 + tm * d * 2 + 2 * 3 * d * tf * wbytes + 3 * d * tf * 2 + 4 * tm * tf * 4
    if final_norm:
        args.append(g_final)
        in_specs.append(_const_spec(g_final.shape))
    out_specs = [pl.BlockSpec((tm, d), rows)]
    out_shape = [jax.ShapeDtypeStruct((t, d), F32)]
    if emit_bf16:
        out_specs += [pl.BlockSpec((d, tf), gate_cols), pl.BlockSpec((d, tf), gate_cols),
                      pl.BlockSpec((tf, d), down_rows)]
        out_shape += [jax.ShapeDtypeStruct((d, dff), BF16)] * 2 + [jax.ShapeDtypeStruct((dff, d), BF16)]
        vmem += 2 * 3 * d * tf * 2
    res = pl.pallas_call(
        functools.partial(_ffn_kernel, final_norm=final_norm, emit_bf16=emit_bf16),
        grid=(n_tiles, nf),
        in_specs=in_specs,
        out_specs=out_specs,
        out_shape=out_shape,
        input_output_aliases={0: 0} if in_place else {},
        scratch_shapes=[pltpu.VMEM((tm, d), BF16)],
        compiler_params=_params(("parallel", "arbitrary"), vmem),
        name="ffn_final" if final_norm else ("ffn_head" if emit_bf16 else "ffn"),
    )(*args)
    return res if emit_bf16 else res[0]


def _mla_proj_kernel(x_ref, gmix_ref, win_ref, gcq_ref, gckv_ref, wuq_tr_ref, wkn_ref, wv_tr_ref,
                     cos_ref, sin_ref, cos_tr_ref, sin_tr_ref,
                     q_tr_ref, k_ref, v_tr_ref, h_ref, cq_tr_ref, ckv_ref, ckv_tr_ref, *, chunk):
    h_ref[...] = _rms(x_ref[...], gmix_ref[...]).astype(BF16)
    p = _dot(h_ref[...], win_ref[...])
    cq_tr_ref[...] = _rms(p[:, 0:Q_LORA], gcq_ref[...]).T.astype(BF16)
    ckv = _rms(p[:, Q_LORA:Q_LORA + KV_LORA], gckv_ref[...])
    ckv_ref[...] = ckv.astype(BF16)
    ckv_tr_ref[...] = ckv.T.astype(BF16)
    a = p[:, Q_LORA + KV_LORA:]
    k_rope = (a * cos_ref[...] + pltpu.roll(a, ROPE_DIM, axis=1) * sin_ref[...]).astype(BF16)

    rope_lo, rope_mid, rope_hi = NOPE_DIM, NOPE_DIM + ROPE_DIM // 2, NOPE_DIM + ROPE_DIM
    for hd in range(C_HEADS):
        qh = _dot(wuq_tr_ref[hd * rope_hi:(hd + 1) * rope_hi, :], cq_tr_ref[...])
        r0 = hd * QK_PAD
        q_tr_ref[r0:r0 + rope_lo, :] = qh[0:rope_lo].astype(BF16)
        swapped = jnp.concatenate([qh[rope_mid:rope_hi], qh[rope_lo:rope_mid]], axis=0)
        roped = qh[rope_lo:rope_hi] * cos_tr_ref[...] + swapped * sin_tr_ref[...]
        q_tr_ref[r0 + rope_lo:r0 + rope_hi, :] = roped.astype(BF16)
        q_tr_ref[r0 + rope_hi:r0 + QK_PAD, :] = jnp.zeros((QK_PAD - rope_hi, qh.shape[1]), BF16)

    heads_per_chunk = chunk // NOPE_DIM
    for c in range(C_HEADS // heads_per_chunk):
        sl = slice(c * chunk, (c + 1) * chunk)
        kn = _dot(ckv_ref[...], wkn_ref[:, sl]).astype(BF16)
        for hh in range(heads_per_chunk):
            hd = c * heads_per_chunk + hh
            k_ref[hd, :, 0:NOPE_DIM] = kn[:, hh * NOPE_DIM:(hh + 1) * NOPE_DIM]
            k_ref[hd, :, NOPE_DIM:] = k_rope
        v_tr_ref[sl, :] = _dot(wv_tr_ref[sl, :], ckv_tr_ref[...]).astype(BF16)


def _mla_proj(x2d, gmix, w_in_ext, g_cq_s, g_ckv, w_uq_tr, w_kn, w_v_tr, cos_t, sin_t, cos_tr, sin_tr,
              *, seq, tm):
    t, d = x2d.shape
    n_q = C_HEADS * QK_PAD
    assert w_uq_tr.shape[0] == C_HEADS * (NOPE_DIM + ROPE_DIM)
    n_kv = w_kn.shape[1]
    tiles_per_seq = seq // tm
    assert seq % tm == 0 and w_in_ext.shape[1] == Q_LORA + KV_LORA + 2 * ROPE_DIM
    vmem = (2 * tm * d * 4 + (w_in_ext.size + w_uq_tr.size + w_kn.size + w_v_tr.size) * 2
            + 2 * 2 * tm * (V7X_LANES + ROPE_DIM) * 4
            + 2 * tm * (n_q + C_HEADS * QK_PAD + n_kv) * 2
            + tm * d * 2 + tm * (Q_LORA + 2 * KV_LORA) * 2
            + 4 * tm * w_in_ext.shape[1] * 4)
    row = lambda i: (i, 0)
    col = lambda i: (0, i)
    return pl.pallas_call(
        functools.partial(_mla_proj_kernel, chunk=KV_PROJ_COLS),
        grid=(t // tm,),
        in_specs=[pl.BlockSpec((tm, d), row),
                  _const_spec(gmix.shape), _const_spec(w_in_ext.shape), _const_spec(g_cq_s.shape),
                  _const_spec(g_ckv.shape), _const_spec(w_uq_tr.shape), _const_spec(w_kn.shape),
                  _const_spec(w_v_tr.shape),
                  pl.BlockSpec((tm, V7X_LANES), lambda i: (i % tiles_per_seq, 0)),
                  pl.BlockSpec((tm, V7X_LANES), lambda i: (i % tiles_per_seq, 0)),
                  pl.BlockSpec((ROPE_DIM, tm), lambda i: (0, i % tiles_per_seq)),
                  pl.BlockSpec((ROPE_DIM, tm), lambda i: (0, i % tiles_per_seq))],
        out_specs=[pl.BlockSpec((n_q, tm), col), pl.BlockSpec((C_HEADS, tm, QK_PAD), lambda i: (0, i, 0)),
                   pl.BlockSpec((n_kv, tm), col)],
        out_shape=[jax.ShapeDtypeStruct((n_q, t), BF16), jax.ShapeDtypeStruct((C_HEADS, t, QK_PAD), BF16),
                   jax.ShapeDtypeStruct((n_kv, t), BF16)],
        scratch_shapes=[pltpu.VMEM((tm, d), BF16), pltpu.VMEM((Q_LORA, tm), BF16),
                        pltpu.VMEM((tm, KV_LORA), BF16), pltpu.VMEM((KV_LORA, tm), BF16)],
        compiler_params=_params(("parallel",), vmem),
        name="mla_proj",
    )(x2d, gmix, w_in_ext, g_cq_s, g_ckv, w_uq_tr, w_kn, w_v_tr, cos_t, sin_t, cos_tr, sin_tr)


def _attn_kernel(*refs, tk, tq, skew, n_cast, heads):
    q_tr_ref, k_ref, v_tr_ref = refs[0:3]
    cast_in = refs[3:3 + n_cast]
    o_ref = refs[3 + n_cast]
    cast_out = refs[4 + n_cast:4 + 2 * n_cast]
    vaug_ref, m_ref, acc_ref = refs[4 + 2 * n_cast:]
    for src_ref, dst_ref in zip(cast_in, cast_out):
        dst_ref[...] = src_ref[...].astype(BF16)
    seq = k_ref.shape[1]
    v_rows = V_DIM + ONES_ROWS
    for g in range(heads):
        vaug_ref[g * v_rows:g * v_rows + V_DIM, :] = v_tr_ref[g * V_DIM:(g + 1) * V_DIM, :]
        vaug_ref[g * v_rows + V_DIM:(g + 1) * v_rows, :] = jnp.ones((ONES_ROWS, seq), BF16)

    units = []
    for g in range(heads):
        for j in range(seq // tk):
            for i in range(seq // tq):
                k_lo, q_lo = j * tk, i * tq
                k_hi = min(k_lo + tk, q_lo + tq)
                if k_hi > k_lo:
                    units.append((g, j == 0, k_lo, k_hi, q_lo))

    def scores(unit):
        g, _, k_lo, k_hi, q_lo = unit
        s = _dot(k_ref[g, k_lo:k_hi, :], q_tr_ref[g * QK_PAD:(g + 1) * QK_PAD, q_lo:q_lo + tq])
        if k_hi > q_lo + CHUNK:
            kc = (lax.broadcasted_iota(jnp.int32, s.shape, 0) + k_lo) // CHUNK
            qc = (lax.broadcasted_iota(jnp.int32, s.shape, 1) + q_lo) // CHUNK
            s = jnp.where(kc <= qc, s, -jnp.inf)
        return s

    pending = [scores(u) for u in units[:skew]]
    for n, (g, first, k_lo, k_hi, q_lo) in enumerate(units):
        if n + skew < len(units):
            pending.append(scores(units[n + skew]))
        s = pending.pop(0)
        qs = slice(q_lo, q_lo + tq)
        rows = slice(g * v_rows, (g + 1) * v_rows)
        v_j = vaug_ref[rows, k_lo:k_hi]
        m_blk = jnp.max(s, axis=0, keepdims=True)
        if first:
            m_new = m_blk
            acc_ref[rows, qs] = _dot(v_j, jnp.exp2(s - m_new).astype(BF16))
        else:
            m_prev = m_ref[g:g + 1, qs]
            m_new = jnp.maximum(m_prev, m_blk)
            alpha = jnp.exp2(m_prev - m_new)
            acc_ref[rows, qs] = alpha * acc_ref[rows, qs] + _dot(v_j, jnp.exp2(s - m_new).astype(BF16))
        m_ref[g:g + 1, qs] = m_new

    for g in range(heads):
        for i in range(seq // tq):
            qs = slice(i * tq, (i + 1) * tq)
            num = acc_ref[g * v_rows:g * v_rows + V_DIM, qs]
            den = acc_ref[g * v_rows + V_DIM:g * v_rows + V_DIM + 1, qs]
            o_ref[g, qs, :] = (num / den).T.astype(o_ref.dtype)


def _attention(q_tr, k, v_tr, *, batch, seq, tk, tq, skew, heads, cast_next=(), cast_layer=None):
    t = k.shape[1]
    assert C_HEADS % heads == 0
    hsteps = C_HEADS // heads
    steps = batch * hsteps
    assert seq % tk == 0 and seq % tq == 0 and tk % CHUNK == 0 and tq % CHUNK == 0
    assert V_DIM == V7X_LANES and NOPE_DIM == V7X_LANES
    vmem = heads * (2 * (2 * QK_PAD + V_DIM) * seq * 2 + 2 * seq * V_DIM * 2
                    + (V_DIM + ONES_ROWS) * seq * 6 + 8 * seq * 4) + 24 * tk * tq * 4
    in_specs = [pl.BlockSpec((heads * QK_PAD, seq), lambda b, h: (h, b)),
                pl.BlockSpec((heads, seq, QK_PAD), lambda b, h: (h, b, 0)),
                pl.BlockSpec((heads * V_DIM, seq), lambda b, h: (h, b))]
    out_specs = [pl.BlockSpec((heads, seq, V_DIM), lambda b, h: (h, b, 0))]
    out_shape = [jax.ShapeDtypeStruct((C_HEADS, t, V_DIM), BF16)]
    for w in cast_next:
        _, r, c = w.shape
        assert r % (steps * 16) == 0
        in_specs.append(pl.BlockSpec((None, r // steps, c), lambda b, h: (cast_layer, b * hsteps + h, 0)))
        out_specs.append(pl.BlockSpec((r // steps, c), lambda b, h: (b * hsteps + h, 0)))
        out_shape.append(jax.ShapeDtypeStruct((r, c), BF16))
        vmem += 2 * (r // steps) * c * 6
    res = pl.pallas_call(
        functools.partial(_attn_kernel, tk=tk, tq=tq, skew=skew, n_cast=len(cast_next), heads=heads),
        grid=(batch, hsteps),
        in_specs=in_specs,
        out_specs=out_specs,
        out_shape=out_shape,
        scratch_shapes=[pltpu.VMEM((heads * (V_DIM + ONES_ROWS), seq), BF16),
                        pltpu.VMEM((heads, seq), F32),
                        pltpu.VMEM((heads * (V_DIM + ONES_ROWS), seq), F32)],
        compiler_params=_params(("parallel", "parallel"), vmem),
        name="attention",
    )(q_tr, k, v_tr, *cast_next)
    return res if cast_next else res[0]


def _proj_residual_kernel(x_ref, a_ref, w_ref, o_ref):
    a = jnp.concatenate([a_ref[hd] for hd in range(a_ref.shape[0])], axis=1)
    o_ref[...] = x_ref[...] + _dot(a, w_ref[...])


def _proj_residual(x2d, a, w, *, tm):
    t, d = x2d.shape
    heads, _, hd_dim = a.shape
    k = heads * hd_dim
    vmem = 4 * tm * d * 4 + 2 * tm * k * 2 + w.size * 2 + 2 * tm * d * 4
    return pl.pallas_call(
        _proj_residual_kernel,
        grid=(t // tm,),
        in_specs=[pl.BlockSpec((tm, d), lambda i: (i, 0)), pl.BlockSpec((heads, tm, hd_dim), lambda i: (0, i, 0)),
                  _const_spec(w.shape)],
        out_specs=pl.BlockSpec((tm, d), lambda i: (i, 0)),
        out_shape=jax.ShapeDtypeStruct((t, d), F32),
        compiler_params=_params(("parallel",), vmem),
        name="attn_out_proj",
    )(x2d, a, w)


def _swap_halves(w):
    half = w.shape[-1] // 2
    return jnp.concatenate([w[..., half:], w[..., :half]], axis=-1)


def kernel(x, g_mix, g_ffn, g_final, w_in_ab, g_v, w_s, b_s, w_pool, pool_scale, w_out_ab, w_in_c, g_cq, g_ckv, w_uq, w_ukv, w_out_c, w_gate, w_up, w_down):
    batch, seq, d = x.shape
    x2d = x.reshape(batch * seq, d)
    row = lambda a: a.reshape(1, -1)

    b_s_b = jnp.broadcast_to(b_s[0][:, :, None], (A_HEADS, GMLP_BLOCK, GMLP_BLOCK))
    x1 = _l0_mixer(x2d, row(g_mix[0]), w_in_ab[0].astype(BF16), row(g_v[0]), w_s[0], b_s_b,
                   w_pool[0].astype(BF16), row(pool_scale[0]), w_out_ab[0].astype(BF16),
                   seq=seq, tm=PROJ_ROWS)
    x2_head, wg0, wu0, wd0 = _ffn(x1, row(g_ffn[0]), w_gate, w_up, w_down, layer=0, tm=FFN_ROWS,
                                  tf=FFN_HIDDEN_F32, n_tiles=1, in_place=True, emit_bf16=True)
    x2 = _ffn(x2_head, row(g_ffn[0]), wg0, wu0, wd0, tm=FFN_ROWS, tf=FFN_HIDDEN, tile_lo=1, in_place=True)

    inv_freq = ROPE_THETA ** (-jnp.arange(0, ROPE_DIM, 2, dtype=F32) / ROPE_DIM)
    cos_t, sin_t, cos_tr, sin_tr = _rope_tables(
        jnp.tile(inv_freq, V7X_LANES // inv_freq.shape[0]).reshape(1, -1), seq, PROJ_ROWS)
    wc = w_in_c[0]
    w_kr = wc[:, Q_LORA + KV_LORA:]
    w_in_ext = jnp.concatenate([wc, _swap_halves(w_kr)], axis=1).astype(BF16)
    w_uq_tr = w_uq[0].T.astype(BF16)
    wkv = w_ukv[0].reshape(KV_LORA, C_HEADS, NOPE_DIM + V_DIM)
    w_kn = wkv[..., :NOPE_DIM].reshape(KV_LORA, -1).astype(BF16)
    w_v_tr = wkv[..., NOPE_DIM:].reshape(KV_LORA, -1).T.astype(BF16)
    score_scale = (NOPE_DIM + ROPE_DIM) ** -0.5 * LOG2_E
    q_tr, k, v_tr = _mla_proj(x2, row(g_mix[1]), w_in_ext, row(g_cq[0]) * score_scale,
                              row(g_ckv[0]), w_uq_tr, w_kn, w_v_tr, cos_t, sin_t, cos_tr, sin_tr,
                              seq=seq, tm=PROJ_ROWS)
    o, w_gate1, w_up1, w_down1 = _attention(q_tr, k, v_tr, batch=batch, seq=seq, tk=ATTN_KEYS, tq=ATTN_QUERIES,
                                            skew=ATTN_LOOKAHEAD, heads=ATTN_HEADS_PER_STEP, cast_next=(w_gate, w_up, w_down), cast_layer=1)
    x3 = _proj_residual(x2, o, w_out_c[0].astype(BF16), tm=PROJ_ROWS)
    out = _ffn(x3, row(g_ffn[1]), w_gate1, w_up1, w_down1, tm=FFN_ROWS, tf=FFN_HIDDEN, g_final=row(g_final))
    return out.reshape(batch, seq, d)
```
